```python
import math
import jax, jax.numpy as jnp
from jax import lax
import numpy as np

D_MODEL = 2048
BATCH = 1
SEQ = 16384
DEPTH = 2

N_EVEN = (DEPTH + 1) // 2
N_ODD = DEPTH // 2
EPS = 1e-6
MEM_LEN = 256

RET_HEADS = 4
RET_QK_DIM = 256
RET_V_DIM = 512
RET_CHUNK = 128
ROPE_BASE = 10000.0

DIL_HEADS = 8
DIL_DIM = 128
DIL_PATTERNS = ((128, 1), (512, 4), (2048, 16))
REL_BUCKETS = 32
REL_MAX_DIST = 2048

AR_QK = RET_HEADS * RET_QK_DIM
AR_V = RET_HEADS * RET_V_DIM
DIL_W = DIL_HEADS * DIL_DIM
AR_IN = 2 * AR_QK + 2 * AR_V + 3 * DIL_W
AR_OUT = AR_V + DIL_W

GDN_K_HEADS = 16
GDN_V_HEADS = 32
GDN_K_DIM = 128
GDN_V_DIM = 128
GDN_CONV = 4
GDN_CHUNK = 64
GDN_QK_W = GDN_K_HEADS * GDN_K_DIM
GDN_V_W = GDN_V_HEADS * GDN_V_DIM
GDN_QKV = 2 * GDN_QK_W + GDN_V_W
GDN_IN = GDN_QKV + GDN_V_W + 2 * GDN_V_HEADS

XA_HEADS = 4
XA_DIM = 128

FFN_HIDDEN = -(-8 * D_MODEL // (3 * 256)) * 256

kernel_name = "hybrid_retention_dilated_gdn_trunk"


def _rms(x):
    xf = x.astype(jnp.float32)
    return xf * lax.rsqrt(jnp.mean(xf * xf, axis=-1, keepdims=True) + EPS)


def rms_norm(x, gain):
    return (_rms(x) * gain.astype(jnp.float32)).astype(x.dtype)


def l2_norm(x):
    xf = x.astype(jnp.float32)
    return xf * lax.rsqrt(jnp.sum(xf * xf, axis=-1, keepdims=True) + EPS)


def rotary(x, pos):
    d = x.shape[-1]
    inv = ROPE_BASE ** (-jnp.arange(0, d, 2, dtype=jnp.float32) / d)
    ang = pos.astype(jnp.float32)[:, None] * inv[None, :]
    cos = jnp.cos(ang)[None, :, None, :]
    sin = jnp.sin(ang)[None, :, None, :]
    xf = x.astype(jnp.float32)
    x1, x2 = xf[..., : d // 2], xf[..., d // 2:]
    return jnp.concatenate([x1 * cos - x2 * sin, x1 * sin + x2 * cos], axis=-1)


def retention(q, k, v):
    B, S, H, Dk = q.shape
    Dv = v.shape[-1]
    C = RET_CHUNK
    N = S // C
    log_gamma = jnp.log(1.0 - 2.0 ** (-5.0 - jnp.arange(H, dtype=jnp.float32)))
    idx = jnp.arange(C, dtype=jnp.float32)
    rel = idx[:, None] - idx[None, :]
    inner_decay = jnp.where(rel >= 0, jnp.exp(log_gamma[:, None, None] * jnp.maximum(rel, 0.0)), 0.0)
    q_decay = jnp.exp(log_gamma[:, None] * (idx + 1.0))[..., None]
    k_decay = jnp.exp(log_gamma[:, None] * (C - 1.0 - idx))[..., None]
    chunk_decay = jnp.exp(log_gamma * C)[:, None, None]

    def chunks(t):
        return t.astype(jnp.float32).reshape(B, N, C, H, t.shape[-1]).transpose(1, 0, 3, 2, 4)

    qc, kc, vc = chunks(q), chunks(k) * (Dk ** -0.5), chunks(v)

    def step(state, inp):
        qi, ki, vi = inp
        scores = jnp.einsum('bhqd,bhkd->bhqk', qi, ki) * inner_decay
        o = (jnp.einsum('bhqk,bhkv->bhqv', scores, vi)
             + jnp.einsum('bhqd,bhdv->bhqv', qi * q_decay, state))
        state = state * chunk_decay + jnp.einsum('bhkd,bhkv->bhdv', ki * k_decay, vi)
        return state, o

    state0 = jnp.zeros((B, H, Dk, Dv), jnp.float32)
    _, o = lax.scan(step, state0, (qc, kc, vc))
    return o.transpose(1, 0, 3, 2, 4).reshape(B, S, H, Dv)


def t5_bucket(dist):
    exact = REL_BUCKETS // 2
    large = exact + (jnp.log(jnp.maximum(dist, exact).astype(jnp.float32) / exact)
                     / math.log(REL_MAX_DIST / exact) * (REL_BUCKETS - exact)).astype(jnp.int32)
    large = jnp.minimum(large, REL_BUCKETS - 1)
    return jnp.where(dist < exact, dist, large)


def dilated_branch(q, k, v, rel_bias, window, dilation):
    B, S, H, D = q.shape
    n = window // dilation
    L = S // dilation
    nb = -(-L // n)
    Lp = nb * n

    def to_blocks(t):
        t = t.astype(jnp.float32).reshape(B, L, dilation, H, D).transpose(0, 2, 3, 1, 4)
        t = jnp.pad(t, ((0, 0), (0, 0), (0, 0), (0, Lp - L), (0, 0)))
        return t.reshape(B, dilation, H, nb, n, D)

    def with_prev(t):
        prev = jnp.pad(t[:, :, :, :-1], ((0, 0), (0, 0), (0, 0), (1, 0), (0, 0), (0, 0)))
        return jnp.concatenate([prev, t], axis=4)

    def from_blocks(t):
        rest = t.shape[5:]
        t = t.reshape((B, dilation, H, Lp) + rest)[:, :, :, :L]
        t = t.transpose((0, 3, 1, 2) + tuple(range(4, 4 + len(rest))))
        return t.reshape((B, S, H) + rest)

    qb = to_blocks(q)
    kb = with_prev(to_blocks(k))
    vb = with_prev(to_blocks(v))

    qi = jnp.arange(n)[:, None] + n
    ki = jnp.arange(2 * n)[None, :]
    sub_dist = qi - ki
    band = (sub_dist >= 0) & (sub_dist <= n)
    blk = jnp.arange(nb)[:, None, None]
    valid = band[None] & ((blk > 0) | (ki[None] >= n))
    bias = rel_bias.astype(jnp.float32)[t5_bucket(jnp.maximum(sub_dist, 0) * dilation)]
    bias = bias.transpose(2, 0, 1)[:, None]

    s = jnp.einsum('bdhnqc,bdhnkc->bdhnqk', qb, kb) + bias
    s = jnp.where(valid, s, -jnp.inf)
    m = jnp.max(s, axis=-1)
    p = jnp.exp(s - m[..., None])
    l = jnp.sum(p, axis=-1)
    o = jnp.einsum('bdhnqk,bdhnkc->bdhnqc', p, vb) / l[..., None]
    return from_blocks(o), from_blocks(m), from_blocks(l)


def dilated_attention(q, k, v, rel_bias):
    outs = [dilated_branch(q, k, v, rel_bias, w, d) for (w, d) in DIL_PATTERNS]
    m_all = jnp.stack([m for (_, m, _) in outs])
    m_max = jnp.max(m_all, axis=0)
    wts = jnp.stack([l * jnp.exp(m - m_max) for (_, m, l) in outs])
    o_all = jnp.stack([o for (o, _, _) in outs])
    return jnp.sum(wts[..., None] * o_all, axis=0) / jnp.sum(wts, axis=0)[..., None]


def retention_dilated_mixer(h, w_in, w_out, q_gain, k_gain, rel_bias):
    B, S, _ = h.shape
    proj = h @ w_in
    qa = proj[..., :AR_QK].reshape(B, S, RET_HEADS, RET_QK_DIM)
    ka = proj[..., AR_QK:2 * AR_QK].reshape(B, S, RET_HEADS, RET_QK_DIM)
    va = proj[..., 2 * AR_QK:2 * AR_QK + AR_V].reshape(B, S, RET_HEADS, RET_V_DIM)
    ga = proj[..., 2 * AR_QK + AR_V:2 * AR_QK + 2 * AR_V]
    qkv_b = proj[..., 2 * AR_QK + 2 * AR_V:]

    pos = jnp.arange(S)
    ya = retention(rotary(qa, pos), rotary(ka, pos), va)
    ya = _rms(ya).reshape(B, S, AR_V) * jax.nn.silu(ga.astype(jnp.float32))

    qb = qkv_b[..., :DIL_W].reshape(B, S, DIL_HEADS, DIL_DIM)
    kb = qkv_b[..., DIL_W:2 * DIL_W].reshape(B, S, DIL_HEADS, DIL_DIM)
    vb = qkv_b[..., 2 * DIL_W:].reshape(B, S, DIL_HEADS, DIL_DIM)
    qb = rms_norm(qb, q_gain) * (DIL_DIM ** -0.5)
    kb = rms_norm(kb, k_gain)
    yb = dilated_attention(qb, kb, vb, rel_bias).reshape(B, S, DIL_W)

    y = jnp.concatenate([ya, yb], axis=-1).astype(h.dtype)
    return y @ w_out


def causal_depthwise_conv(x, w):
    K, ch = w.shape
    return lax.conv_general_dilated(
        x, w[:, None, :].astype(x.dtype), window_strides=(1,), padding=[(K - 1, 0)],
        dimension_numbers=('NWC', 'WIO', 'NWC'), feature_group_count=ch)


def chunk_gated_delta_rule(q, k, v, g, beta):
    B, S, H, Dk = q.shape
    Dv = v.shape[-1]
    C = GDN_CHUNK
    N = S // C

    def chunks(t):
        return t.astype(jnp.float32).reshape((B, N, C) + t.shape[2:]).swapaxes(2, 3)

    q, k, v, g, beta = chunks(q), chunks(k), chunks(v), chunks(g), chunks(beta)
    g_cum = jnp.cumsum(g, axis=-1)
    idx = jnp.arange(C)
    causal = idx[:, None] >= idx[None, :]
    strict = idx[:, None] > idx[None, :]
    decay = jnp.exp(jnp.where(causal, g_cum[..., :, None] - g_cum[..., None, :], -jnp.inf))
    k_beta = k * beta[..., None]
    v_beta = v * beta[..., None]
    a_mat = jnp.where(strict, jnp.einsum('bnhid,bnhjd->bnhij', k_beta, k) * decay, 0.0)
    eye = jnp.eye(C, dtype=jnp.float32)
    t_mat = lax.linalg.triangular_solve(eye + a_mat, jnp.broadcast_to(eye, a_mat.shape),
                                        left_side=True, lower=True, unit_diagonal=True)
    u = jnp.einsum('bnhij,bnhjv->bnhiv', t_mat, v_beta)
    w = jnp.einsum('bnhij,bnhjd->bnhid', t_mat, k_beta * jnp.exp(g_cum)[..., None])
    intra = jnp.where(causal, jnp.einsum('bnhid,bnhjd->bnhij', q, k) * decay, 0.0)

    def step(state, inp):
        q_c, k_c, u_c, w_c, intra_c, g_c = inp
        v_new = u_c - jnp.einsum('bhcd,bhdv->bhcv', w_c, state)
        o = (jnp.einsum('bhcd,bhdv->bhcv', q_c * jnp.exp(g_c)[..., None], state)
             + jnp.einsum('bhij,bhjv->bhiv', intra_c, v_new))
        g_last = g_c[..., -1]
        state = (state * jnp.exp(g_last)[..., None, None]
                 + jnp.einsum('bhcd,bhcv->bhdv', k_c * jnp.exp(g_last[..., None] - g_c)[..., None], v_new))
        return state, o

    xs = tuple(jnp.moveaxis(t, 1, 0) for t in (q, k, u, w, intra, g_cum))
    _, o = lax.scan(step, jnp.zeros((B, H, Dk, Dv), jnp.float32), xs)
    return jnp.moveaxis(o, 0, 1).swapaxes(2, 3).reshape(B, S, H, Dv)


def gated_deltanet_mixer(h, w_in, conv_w, a_log, dt_bias, norm_gain, w_out):
    B, S, _ = h.shape
    proj = h @ w_in
    qkv = jax.nn.silu(causal_depthwise_conv(proj[..., :GDN_QKV], conv_w))
    z = proj[..., GDN_QKV:GDN_QKV + GDN_V_W].reshape(B, S, GDN_V_HEADS, GDN_V_DIM)
    b = proj[..., GDN_QKV + GDN_V_W:GDN_QKV + GDN_V_W + GDN_V_HEADS]
    a = proj[..., GDN_QKV + GDN_V_W + GDN_V_HEADS:]
    rep = GDN_V_HEADS // GDN_K_HEADS
    q = l2_norm(qkv[..., :GDN_QK_W].reshape(B, S, GDN_K_HEADS, GDN_K_DIM)) * (GDN_K_DIM ** -0.5)
    k = l2_norm(qkv[..., GDN_QK_W:2 * GDN_QK_W].reshape(B, S, GDN_K_HEADS, GDN_K_DIM))
    v = qkv[..., 2 * GDN_QK_W:].reshape(B, S, GDN_V_HEADS, GDN_V_DIM)
    q = jnp.repeat(q, rep, axis=2)
    k = jnp.repeat(k, rep, axis=2)
    beta = jax.nn.sigmoid(b.astype(jnp.float32))
    g = -jnp.exp(a_log.astype(jnp.float32)) * jax.nn.softplus(a.astype(jnp.float32) + dt_bias.astype(jnp.float32))
    o = chunk_gated_delta_rule(q, k, v, g, beta)
    o = _rms(o) * norm_gain.astype(jnp.float32) * jax.nn.silu(z.astype(jnp.float32))
    return o.reshape(B, S, GDN_V_W).astype(h.dtype) @ w_out


def memory_cross_attention(h, mem_n, w_q, w_kv, w_o, q_gain, k_gain):
    B, S, _ = h.shape
    M = mem_n.shape[1]
    q = rms_norm((h @ w_q).reshape(B, S, XA_HEADS, XA_DIM), q_gain) * (XA_DIM ** -0.5)
    kv = (mem_n @ w_kv).reshape(B, M, 2, XA_HEADS, XA_DIM)
    k = rms_norm(kv[:, :, 0], k_gain)
    v = kv[:, :, 1]
    s = jnp.einsum('bqhd,bkhd->bhqk', q, k).astype(jnp.float32)
    p = jax.nn.softmax(s, axis=-1).astype(v.dtype)
    o = jnp.einsum('bhqk,bkhd->bqhd', p, v).reshape(B, S, XA_HEADS * XA_DIM)
    return o @ w_o


def swiglu(h, w1, w3, w2):
    return (jax.nn.silu(h @ w1) * (h @ w3)) @ w2


def setup_inputs(seed: int = 0) -> dict:
    key = jax.random.key(seed)
    ks = jax.random.split(key, 32)
    D = D_MODEL

    def w(k, shape, fan_in):
        return jax.random.normal(k, shape, jnp.float32) * fan_in ** -0.5

    def gain(k, shape):
        return 1.0 + 0.05 * jax.random.normal(k, shape, jnp.float32)

    return {
        "x": jax.random.normal(ks[0], (BATCH, SEQ, D), jnp.float32),
        "mem": jax.random.normal(ks[1], (BATCH, MEM_LEN, D), jnp.float32),
        "norm_mix": gain(ks[2], (DEPTH, D)),
        "norm_xa": gain(ks[3], (DEPTH, D)),
        "norm_ffn": gain(ks[4], (DEPTH, D)),
        "mem_norm": gain(ks[5], (D,)),
        "rel_bias": 0.5 * jax.random.normal(ks[6], (REL_BUCKETS, DIL_HEADS), jnp.float32),
        "ar_w_in": w(ks[7], (N_EVEN, D, AR_IN), D),
        "ar_w_out": w(ks[8], (N_EVEN, AR_OUT, D), AR_OUT),
        "dil_q_gain": gain(ks[9], (N_EVEN, DIL_DIM)),
        "dil_k_gain": gain(ks[10], (N_EVEN, DIL_DIM)),
        "gdn_w_in": w(ks[11], (N_ODD, D, GDN_IN), D),
        "gdn_conv": w(ks[12], (N_ODD, GDN_CONV, GDN_QKV), GDN_CONV),
        "gdn_a_log": jnp.log(jax.random.uniform(ks[13], (N_ODD, GDN_V_HEADS), jnp.float32, 1.0, 16.0)),
        "gdn_dt_bias": 0.1 * jax.random.normal(ks[14], (N_ODD, GDN_V_HEADS), jnp.float32),
        "gdn_norm": gain(ks[15], (N_ODD, GDN_V_DIM)),
        "gdn_w_out": w(ks[16], (N_ODD, GDN_V_W, D), GDN_V_W),
        "xa_w_q": w(ks[17], (DEPTH, D, XA_HEADS * XA_DIM), D),
        "xa_w_kv": w(ks[18], (DEPTH, D, 2 * XA_HEADS * XA_DIM), D),
        "xa_w_o": w(ks[19], (DEPTH, XA_HEADS * XA_DIM, D), XA_HEADS * XA_DIM),
        "xa_q_gain": gain(ks[20], (DEPTH, XA_DIM)),
        "xa_k_gain": gain(ks[21], (DEPTH, XA_DIM)),
        "ffn_w1": w(ks[22], (DEPTH, D, FFN_HIDDEN), D),
        "ffn_w3": w(ks[23], (DEPTH, D, FFN_HIDDEN), D),
        "ffn_w2": w(ks[24], (DEPTH, FFN_HIDDEN, D), FFN_HIDDEN),
    }


def reference(x, mem, norm_mix, norm_xa, norm_ffn, mem_norm, rel_bias,
              ar_w_in, ar_w_out, dil_q_gain, dil_k_gain,
              gdn_w_in, gdn_conv, gdn_a_log, gdn_dt_bias, gdn_norm, gdn_w_out,
              xa_w_q, xa_w_kv, xa_w_o, xa_q_gain, xa_k_gain,
              ffn_w1, ffn_w3, ffn_w2):
    mem_n = rms_norm(mem, mem_norm)
    for layer in range(DEPTH):
        i = layer // 2
        h = rms_norm(x, norm_mix[layer])
        if layer % 2 == 0:
            y = retention_dilated_mixer(h, ar_w_in[i], ar_w_out[i], dil_q_gain[i], dil_k_gain[i], rel_bias)
        else:
            y = gated_deltanet_mixer(h, gdn_w_in[i], gdn_conv[i], gdn_a_log[i], gdn_dt_bias[i],
                                     gdn_norm[i], gdn_w_out[i])
        x = x + y.astype(x.dtype)
        h = rms_norm(x, norm_xa[layer])
        x = x + memory_cross_attention(h, mem_n, xa_w_q[layer], xa_w_kv[layer], xa_w_o[layer],
                                       xa_q_gain[layer], xa_k_gain[layer]).astype(x.dtype)
        h = rms_norm(x, norm_ffn[layer])
        x = x + swiglu(h, ffn_w1[layer], ffn_w3[layer], ffn_w2[layer]).astype(x.dtype)
    return x
```

```python
import functools
import math

import jax
import jax.numpy as jnp
from jax import lax
from jax.experimental import pallas as pl
from jax.experimental.pallas import tpu as pltpu

F32 = jnp.float32
BF16 = jnp.bfloat16
EPS = 1e-6
NEG = -1e30

VMEM_LIMIT_BYTES = 56 * 1024 * 1024
LANES = 128

RET_HEADS = 4
RET_QK_DIM = 256
RET_V_DIM = 512
RET_CHUNK = 128
ROPE_BASE = 10000.0
DIL_HEADS = 8
DIL_DIM = 128
DIL_PATTERNS = ((128, 1), (512, 4), (2048, 16))
DIL_STEPS = 128
REL_BUCKETS = 32
REL_MAX_DIST = 2048
AR_QK = RET_HEADS * RET_QK_DIM
AR_V = RET_HEADS * RET_V_DIM
DIL_W = DIL_HEADS * DIL_DIM
GDN_K_HEADS = 16
GDN_V_HEADS = 32
GDN_DIM = 128
GDN_CONV = 4
GDN_CHUNK = 64
GDN_QK_W = GDN_K_HEADS * GDN_DIM
GDN_V_W = GDN_V_HEADS * GDN_DIM
GDN_QKV = 2 * GDN_QK_W + GDN_V_W
XA_HEADS = 4
XA_DIM = 128


def _params(*semantics):
    return pltpu.CompilerParams(dimension_semantics=semantics, vmem_limit_bytes=VMEM_LIMIT_BYTES)


def _dot(a, b):
    return jnp.dot(a, b, preferred_element_type=F32)


def _dot_nt(a, b):
    return lax.dot_general(a, b, (((1,), (1,)), ((), ())), preferred_element_type=F32)


def _dot_tn(a, b):
    return lax.dot_general(a, b, (((0,), (0,)), ((), ())), preferred_element_type=F32)


def _dot_f32(a, b):
    return jnp.dot(a, b, preferred_element_type=F32, precision=lax.Precision.HIGHEST)


def _dot_nt_f32(a, b):
    return lax.dot_general(a, b, (((1,), (1,)), ((), ())), preferred_element_type=F32,
                           precision=lax.Precision.HIGHEST)


def _rms_rows(x):
    return x * lax.rsqrt(jnp.mean(x * x, axis=-1, keepdims=True) + EPS)


def _silu(x):
    return x * jax.nn.sigmoid(x)


def _norm_matmul_kernel(x_ref, g_ref, w_ref, o_ref, h_ref):
    @pl.when(pl.program_id(1) == 0)
    def _():
        h_ref[...] = (_rms_rows(x_ref[...]) * g_ref[...]).astype(BF16)

    o_ref[...] = _dot(h_ref[...], w_ref[...]).astype(o_ref.dtype)


def norm_matmul(x, gain, w, out_dtype, tm, tn):
    m, k = x.shape
    n = w.shape[1]
    return pl.pallas_call(
        _norm_matmul_kernel,
        grid=(m // tm, n // tn),
        in_specs=[pl.BlockSpec((tm, k), lambda i, j: (i, 0)),
                  pl.BlockSpec((1, k), lambda i, j: (0, 0)),
                  pl.BlockSpec((k, tn), lambda i, j: (0, j))],
        out_specs=pl.BlockSpec((tm, tn), lambda i, j: (i, j)),
        out_shape=jax.ShapeDtypeStruct((m, n), out_dtype),
        scratch_shapes=[pltpu.VMEM((tm, k), BF16)],
        compiler_params=_params("parallel", "arbitrary"),
        name="norm_matmul",
    )(x, gain.reshape(1, k), w)


def _norm_swiglu_kernel(x_ref, g_ref, w1_ref, w3_ref, o_ref, h_ref):
    @pl.when(pl.program_id(1) == 0)
    def _():
        h_ref[...] = (_rms_rows(x_ref[...]) * g_ref[...]).astype(BF16)

    h = h_ref[...]
    a = _dot(h, w1_ref[...])
    b = _dot(h, w3_ref[...])
    o_ref[...] = (_silu(a) * b).astype(o_ref.dtype)


def norm_swiglu(x, gain, w1, w3, tm, tn):
    m, k = x.shape
    n = w1.shape[1]
    return pl.pallas_call(
        _norm_swiglu_kernel,
        grid=(m // tm, n // tn),
        in_specs=[pl.BlockSpec((tm, k), lambda i, j: (i, 0)),
                  pl.BlockSpec((1, k), lambda i, j: (0, 0)),
                  pl.BlockSpec((k, tn), lambda i, j: (0, j)),
                  pl.BlockSpec((k, tn), lambda i, j: (0, j))],
        out_specs=pl.BlockSpec((tm, tn), lambda i, j: (i, j)),
        out_shape=jax.ShapeDtypeStruct((m, n), BF16),
        scratch_shapes=[pltpu.VMEM((tm, k), BF16)],
        compiler_params=_params("parallel", "arbitrary"),
        name="norm_swiglu",
    )(x, gain.reshape(1, k), w1, w3)


def _matmul_residual_kernel(y_ref, w_ref, x_ref, o_ref):
    @pl.when(pl.program_id(2) == 0)
    def _():
        o_ref[...] = x_ref[...]

    o_ref[...] += _dot(y_ref[...], w_ref[...])


def matmul_residual(y, w, x, tm, tn, tk):
    m, k = y.shape
    n = w.shape[1]
    return pl.pallas_call(
        _matmul_residual_kernel,
        grid=(m // tm, n // tn, k // tk),
        in_specs=[pl.BlockSpec((tm, tk), lambda i, j, kk: (i, kk)),
                  pl.BlockSpec((tk, tn), lambda i, j, kk: (kk, j)),
                  pl.BlockSpec((tm, tn), lambda i, j, kk: (i, j))],
        out_specs=pl.BlockSpec((tm, tn), lambda i, j, kk: (i, j)),
        out_shape=jax.ShapeDtypeStruct((m, n), F32),
        compiler_params=_params("parallel", "parallel", "arbitrary"),
        name="matmul_residual",
    )(y, w, x)


def _retention_kernel(lg_ref, q_ref, k_ref, v_ref, g_ref, cos_ref, sin_ref, o_ref, state_ref,
                      *, chunk, n_chunks):
    @pl.when(pl.program_id(1) == 0)
    def _():
        state_ref[...] = jnp.zeros_like(state_ref)

    c = chunk
    half = RET_QK_DIM // 2
    lg = lg_ref[0]
    lg_l = lg[:, :LANES]
    ii = lax.broadcasted_iota(jnp.int32, (c, c), 0)
    jj = lax.broadcasted_iota(jnp.int32, (c, c), 1)
    rel = (ii - jj).astype(F32)
    inner = jnp.where(rel >= 0, jnp.exp(lg[:, :c] * jnp.maximum(rel, 0.0)), 0.0)
    idx = lax.broadcasted_iota(jnp.int32, (c, LANES), 0).astype(F32)
    q_dec = jnp.exp(lg_l * (idx + 1.0))
    k_dec = jnp.exp(lg_l * (c - 1.0 - idx))
    c_dec = jnp.exp(lg * float(c))
    k_scale = RET_QK_DIM ** -0.5

    def rot(t, cos, sin):
        t1, t2 = t[:, :half], t[:, half:]
        return t1 * cos - t2 * sin, t1 * sin + t2 * cos

    for ci in range(n_chunks):
        rows = pl.ds(ci * c, c)
        cos = cos_ref[rows, :]
        sin = sin_ref[rows, :]
        q1, q2 = rot(q_ref[rows, :].astype(F32), cos, sin)
        k1, k2 = rot(k_ref[rows, :].astype(F32) * k_scale, cos, sin)
        v = v_ref[rows, :]
        qb = jnp.concatenate([q1, q2], axis=-1).astype(BF16)
        kb = jnp.concatenate([k1, k2], axis=-1).astype(BF16)
        qd = jnp.concatenate([q1 * q_dec, q2 * q_dec], axis=-1).astype(BF16)
        kd = jnp.concatenate([k1 * k_dec, k2 * k_dec], axis=-1).astype(BF16)
        state = state_ref[...]
        scores = _dot_nt(qb, kb) * inner
        o = _dot(scores.astype(BF16), v) + _dot(qd, state.astype(BF16))
        state_ref[...] = state * c_dec + _dot_tn(kd, v)
        gate = _silu(g_ref[rows, :].astype(F32))
        o_ref[rows, :] = (_rms_rows(o) * gate).astype(o_ref.dtype)


def retention_mixer(proj, cos, sin, rows_per_step):
    s = proj.shape[0]
    t = rows_per_step
    log_gamma = jnp.log(1.0 - 2.0 ** (-5.0 - jnp.arange(RET_HEADS, dtype=F32)))
    lg = jnp.broadcast_to(log_gamma[:, None, None], (RET_HEADS, 1, RET_V_DIM))
    qk_blocks = AR_QK // RET_QK_DIM
    v_off = 2 * AR_QK // RET_V_DIM
    g_off = v_off + AR_V // RET_V_DIM
    kern = functools.partial(_retention_kernel, chunk=RET_CHUNK, n_chunks=t // RET_CHUNK)
    return pl.pallas_call(
        kern,
        grid=(RET_HEADS, s // t),
        in_specs=[pl.BlockSpec((1, 1, RET_V_DIM), lambda h, i: (h, 0, 0)),
                  pl.BlockSpec((t, RET_QK_DIM), lambda h, i: (i, h)),
                  pl.BlockSpec((t, RET_QK_DIM), lambda h, i: (i, qk_blocks + h)),
                  pl.BlockSpec((t, RET_V_DIM), lambda h, i: (i, v_off + h)),
                  pl.BlockSpec((t, RET_V_DIM), lambda h, i: (i, g_off + h)),
                  pl.BlockSpec((t, RET_QK_DIM // 2), lambda h, i: (i, 0)),
                  pl.BlockSpec((t, RET_QK_DIM // 2), lambda h, i: (i, 0))],
        out_specs=pl.BlockSpec((t, RET_V_DIM), lambda h, i: (i, h)),
        out_shape=jax.ShapeDtypeStruct((s, AR_V), BF16),
        scratch_shapes=[pltpu.VMEM((RET_QK_DIM, RET_V_DIM), F32)],
        compiler_params=_params("parallel", "arbitrary"),
        name="retention",
    )(lg, proj, proj, proj, proj, cos, sin)


def _dil_prep_kernel(p_ref, g_ref, o_ref):
    c = pl.program_id(0)
    x = p_ref[...].astype(F32)

    @pl.when(c < 2 * DIL_HEADS)
    def _():
        o_ref[...] = (_rms_rows(x) * g_ref[0]).astype(o_ref.dtype)

    @pl.when(c >= 2 * DIL_HEADS)
    def _():
        o_ref[...] = p_ref[...]


def dilated_prep(proj, q_gain, k_gain, tm):
    s = proj.shape[0]
    col0 = (2 * AR_QK + 2 * AR_V) // DIL_DIM
    gains = jnp.concatenate([
        jnp.broadcast_to(q_gain[None, :] * (DIL_DIM ** -0.5), (DIL_HEADS, DIL_DIM)),
        jnp.broadcast_to(k_gain[None, :], (DIL_HEADS, DIL_DIM)),
        jnp.ones((DIL_HEADS, DIL_DIM), F32)], axis=0).reshape(3 * DIL_HEADS, 1, DIL_DIM)
    return pl.pallas_call(
        _dil_prep_kernel,
        grid=(3 * DIL_HEADS, s // tm),
        in_specs=[pl.BlockSpec((tm, DIL_DIM), lambda c, i: (i, col0 + c)),
                  pl.BlockSpec((1, 1, DIL_DIM), lambda c, i: (c, 0, 0))],
        out_specs=pl.BlockSpec((None, tm, DIL_DIM), lambda c, i: (c, i, 0)),
        out_shape=jax.ShapeDtypeStruct((3 * DIL_HEADS, s, DIL_DIM), BF16),
        compiler_params=_params("parallel", "parallel"),
        name="dilated_prep",
    )(proj, gains)


def _dilated_kernel(bias_ref,
                    q1, k1, k1p, v1, v1p,
                    q4, k4, k4p, v4, v4p,
                    q16, k16, k16p, v16, v16p,
                    o_ref, acc_ref, m_ref, l_ref, *, tile):
    n = DIL_STEPS
    d_dim = DIL_DIM
    first = pl.program_id(1) == 0

    def block(q, kp, kc, vp, vc, bias, mask_prev):
        sp = _dot_nt(q, kp) + bias[:, :n]
        sc = _dot_nt(q, kc) + bias[:, n:]
        if mask_prev:
            sp = jnp.where(first, NEG, sp)
        m = jnp.maximum(jnp.max(sp, axis=-1, keepdims=True), jnp.max(sc, axis=-1, keepdims=True))
        pp = jnp.exp(sp - m)
        pc = jnp.exp(sc - m)
        l = jnp.sum(pp, axis=-1, keepdims=True) + jnp.sum(pc, axis=-1, keepdims=True)
        o = _dot(pp.astype(BF16), vp) + _dot(pc.astype(BF16), vc)
        return m, l, o

    def merge(rows, m, l, o):
        m_old = m_ref[rows, :]
        m_new = jnp.maximum(m_old, m)
        a_old = jnp.exp(m_old - m_new)
        a_cur = jnp.exp(m - m_new)
        return m_new, l_ref[rows, :] * a_old + l * a_cur, acc_ref[rows, :] * a_old + o * a_cur

    bias = bias_ref[2]
    for r in range(16):
        ls = slice(r * d_dim, (r + 1) * d_dim)
        m, l, o = block(q16[:, ls], k16p[:, ls], k16[:, ls], v16p[:, ls], v16[:, ls], bias, True)
        rows = pl.ds(r, n, stride=16)
        m_ref[rows, :] = jnp.broadcast_to(m, (n, d_dim))
        l_ref[rows, :] = jnp.broadcast_to(l, (n, d_dim))
        acc_ref[rows, :] = o

    bias = bias_ref[1]
    nb4 = tile // (4 * n)
    for r in range(4):
        ls = slice(r * d_dim, (r + 1) * d_dim)
        for b in range(nb4):
            cur = pl.ds(b * n, n)
            if b == 0:
                kp, vp = k4p[:, ls], v4p[:, ls]
            else:
                prev = pl.ds((b - 1) * n, n)
                kp, vp = k4[prev, ls], v4[prev, ls]
            m, l, o = block(q4[cur, ls], kp, k4[cur, ls], vp, v4[cur, ls], bias, b == 0)
            rows = pl.ds(b * 4 * n + r, n, stride=4)
            m_new, l_new, acc_new = merge(rows, m, l, o)
            m_ref[rows, :] = m_new
            l_ref[rows, :] = l_new
            acc_ref[rows, :] = acc_new

    bias = bias_ref[0]
    for b in range(tile // n):
        cur = pl.ds(b * n, n)
        if b == 0:
            kp, vp = k1p[...], v1p[...]
        else:
            prev = pl.ds((b - 1) * n, n)
            kp, vp = k1[prev, :], v1[prev, :]
        m, l, o = block(q1[cur, :], kp, k1[cur, :], vp, v1[cur, :], bias, b == 0)
        _, l_new, acc_new = merge(cur, m, l, o)
        o_ref[cur, :] = (acc_new / l_new).astype(o_ref.dtype)


def _t5_bucket(dist):
    exact = REL_BUCKETS // 2
    large = exact + (jnp.log(jnp.maximum(dist, exact).astype(F32) / exact)
                     / math.log(REL_MAX_DIST / exact) * (REL_BUCKETS - exact)).astype(jnp.int32)
    large = jnp.minimum(large, REL_BUCKETS - 1)
    return jnp.where(dist < exact, dist, large)


def _dilated_bias(rel_bias):
    n = DIL_STEPS
    sub = (jnp.arange(n)[:, None] + n) - jnp.arange(2 * n)[None, :]
    band = (sub >= 0) & (sub <= n)
    out = []
    for (_, dil) in DIL_PATTERNS:
        b = rel_bias.astype(F32)[_t5_bucket(jnp.maximum(sub, 0) * dil)]
        out.append(jnp.where(band[None], b.transpose(2, 0, 1), NEG))
    return jnp.stack(out)


def dilated_attention(qkv, rel_bias, tile):
    _, s, dd = qkv.shape
    n = DIL_STEPS
    h_n = DIL_HEADS
    bias = _dilated_bias(rel_bias)
    args, specs = [bias], [pl.BlockSpec((len(DIL_PATTERNS), None, n, 2 * n), lambda h, i: (0, h, 0, 0))]
    for dil in (1, 4, 16):
        view = qkv.reshape(3 * h_n, s // dil, dil * dd)
        rows = tile // dil
        per = rows // n

        def cur_spec(off, rows=rows, dil=dil):
            return pl.BlockSpec((None, rows, dil * dd), lambda h, i: (off + h, i, 0))

        def prev_spec(off, per=per, dil=dil):
            return pl.BlockSpec((None, n, dil * dd), lambda h, i: (off + h, jnp.maximum(i * per - 1, 0), 0))

        args += [view, view, view, view, view]
        specs += [cur_spec(0), cur_spec(h_n), prev_spec(h_n), cur_spec(2 * h_n), prev_spec(2 * h_n)]
    return pl.pallas_call(
        functools.partial(_dilated_kernel, tile=tile),
        grid=(h_n, s // tile),
        in_specs=specs,
        out_specs=pl.BlockSpec((tile, dd), lambda h, i: (i, h)),
        out_shape=jax.ShapeDtypeStruct((s, h_n * dd), BF16),
        scratch_shapes=[pltpu.VMEM((tile, dd), F32)] * 3,
        compiler_params=_params("parallel", "arbitrary"),
        name="dilated_attention",
    )(*args)


def _gdn_gates_kernel(ab_ref, alog_ref, dt_ref, beta_ref, gcum_ref, *, chunk):
    t = ab_ref.shape[0]
    hv = GDN_V_HEADS
    ab = ab_ref[...]
    beta_ref[...] = jax.nn.sigmoid(ab[:, :hv])
    z = ab[:, hv:] + dt_ref[...]
    softplus = jnp.maximum(z, 0.0) + jnp.log1p(jnp.exp(-jnp.abs(z)))
    g = -jnp.exp(alog_ref[...]) * softplus
    ii = lax.broadcasted_iota(jnp.int32, (t, t), 0)
    jj = lax.broadcasted_iota(jnp.int32, (t, t), 1)
    tri = ((ii >= jj) & (ii // chunk == jj // chunk)).astype(F32)
    gcum_ref[...] = _dot_f32(tri, g)


def gdn_gates(ab, a_log, dt_bias, tm):
    s = ab.shape[0]
    hv = GDN_V_HEADS
    return pl.pallas_call(
        functools.partial(_gdn_gates_kernel, chunk=GDN_CHUNK),
        grid=(s // tm,),
        in_specs=[pl.BlockSpec((tm, 2 * hv), lambda i: (i, 0)),
                  pl.BlockSpec((1, hv), lambda i: (0, 0)),
                  pl.BlockSpec((1, hv), lambda i: (0, 0))],
        out_specs=[pl.BlockSpec((tm, hv), lambda i: (i, 0)),
                   pl.BlockSpec((tm, hv), lambda i: (i, 0))],
        out_shape=[jax.ShapeDtypeStruct((s, hv), F32), jax.ShapeDtypeStruct((s, hv), F32)],
        compiler_params=_params("parallel"),
        name="gdn_gates",
    )(ab, a_log.reshape(1, hv), dt_bias.reshape(1, hv))


def _unit_lower_inverse(a):
    c = a.shape[0]
    ii = lax.broadcasted_iota(jnp.int32, (c, c), 0)
    jj = lax.broadcasted_iota(jnp.int32, (c, c), 1)
    p = -a
    t = jnp.where(ii == jj, 1.0, 0.0) + p
    span = 2
    while span < c:
        pb = p.astype(BF16)
        p = _dot(pb, pb)
        t = t + _dot(t.astype(BF16), p.astype(BF16))
        span *= 2
    return t


def _gdn_kernel(q_ref, qp_ref, k_ref, kp_ref, v_ref, vp_ref, z_ref, beta_ref, gcum_ref,
                wq_ref, wk_ref, wv_ref, ng_ref, o_ref,
                qbuf, kbuf, vbuf, state_ref, *, tb, chunk):
    hk = pl.program_id(0)
    first = pl.program_id(1) == 0
    c = chunk
    dd = GDN_DIM
    halo = 8

    @pl.when(first)
    def _():
        state_ref[...] = jnp.zeros_like(state_ref)

    def conv_silu(x_ref, xp_ref, w_ref, buf):
        prev = xp_ref[...].astype(F32)
        buf[0:halo, :] = jnp.where(first, 0.0, prev)
        buf[halo:halo + tb, :] = x_ref[...].astype(F32)
        y = buf[halo:halo + tb, :] * w_ref[GDN_CONV - 1:GDN_CONV, :]
        for j in range(GDN_CONV - 1):
            off = halo - (GDN_CONV - 1) + j
            y = y + buf[off:off + tb, :] * w_ref[j:j + 1, :]
        return _silu(y)

    def l2n(x):
        return x * lax.rsqrt(jnp.sum(x * x, axis=-1, keepdims=True) + EPS)

    qn = l2n(conv_silu(q_ref, qp_ref, wq_ref, qbuf)) * (dd ** -0.5)
    kn = l2n(conv_silu(k_ref, kp_ref, wk_ref, kbuf))
    vv = conv_silu(v_ref, vp_ref, wv_ref, vbuf)

    lane_h = lax.broadcasted_iota(jnp.int32, (1, GDN_V_HEADS), 1)
    ii = lax.broadcasted_iota(jnp.int32, (c, c), 0)
    jj = lax.broadcasted_iota(jnp.int32, (c, c), 1)
    causal = ii >= jj
    strict = ii > jj
    lane = lax.broadcasted_iota(jnp.int32, (c, dd), 1)
    ng = ng_ref[...]

    for j in range(2):
        hv = 2 * hk + j
        sel = lane_h == hv
        beta_col = jnp.sum(jnp.where(sel, beta_ref[...], 0.0), axis=-1, keepdims=True)
        gc_col = jnp.sum(jnp.where(sel, gcum_ref[...], 0.0), axis=-1, keepdims=True)
        for ci in range(tb // c):
            rows = slice(ci * c, (ci + 1) * c)
            q_c, k_c = qn[rows], kn[rows]
            v_c = vv[rows, j * dd:(j + 1) * dd]
            beta = beta_col[rows]
            gc = gc_col[rows]
            xa = jnp.where(lane == 0, gc, jnp.where(lane == 1, 1.0, 0.0))
            ya = jnp.where(lane == 0, 1.0, jnp.where(lane == 1, -gc, 0.0))
            dlog = _dot_nt_f32(xa, ya)
            decay = jnp.where(causal, jnp.exp(jnp.where(causal, dlog, 0.0)), 0.0)
            kb16 = k_c.astype(BF16)
            kk = _dot_nt(kb16, kb16)
            qk = _dot_nt(q_c.astype(BF16), kb16)
            a_mat = jnp.where(strict, kk * beta * decay, 0.0)
            t_mat = _unit_lower_inverse(a_mat).astype(BF16)
            eg = jnp.exp(gc)
            u = _dot(t_mat, (v_c * beta).astype(BF16))
            w = _dot(t_mat, (k_c * (beta * eg)).astype(BF16))
            intra = jnp.where(causal, qk * decay, 0.0)

            state = state_ref[j]
            s16 = state.astype(BF16)
            v_new = u - _dot(w.astype(BF16), s16)
            vn16 = v_new.astype(BF16)
            o = _dot((q_c * eg).astype(BF16), s16) + _dot(intra.astype(BF16), vn16)
            g_last = gc[c - 1:c, :]
            kd = k_c * jnp.exp(g_last - gc)
            state_ref[j] = state * jnp.exp(g_last) + _dot_tn(kd.astype(BF16), vn16)

            zg = _silu(z_ref[rows, j * dd:(j + 1) * dd].astype(F32))
            o_ref[rows, j * dd:(j + 1) * dd] = (_rms_rows(o) * ng * zg).astype(o_ref.dtype)


def gdn_mixer(proj, beta, gcum, conv_w, norm_gain, tb):
    s = proj.shape[0]
    dd = GDN_DIM
    kh = GDN_K_HEADS
    per = tb // 8
    v0 = 2 * GDN_QK_W // (2 * dd)
    z0 = GDN_QKV // (2 * dd)

    def prev_rows(i):
        return jnp.maximum(i * per - 1, 0)

    return pl.pallas_call(
        functools.partial(_gdn_kernel, tb=tb, chunk=GDN_CHUNK),
        grid=(kh, s // tb),
        in_specs=[pl.BlockSpec((tb, dd), lambda h, i: (i, h)),
                  pl.BlockSpec((8, dd), lambda h, i: (prev_rows(i), h)),
                  pl.BlockSpec((tb, dd), lambda h, i: (i, kh + h)),
                  pl.BlockSpec((8, dd), lambda h, i: (prev_rows(i), kh + h)),
                  pl.BlockSpec((tb, 2 * dd), lambda h, i: (i, v0 + h)),
                  pl.BlockSpec((8, 2 * dd), lambda h, i: (prev_rows(i), v0 + h)),
                  pl.BlockSpec((tb, 2 * dd), lambda h, i: (i, z0 + h)),
                  pl.BlockSpec((tb, GDN_V_HEADS), lambda h, i: (i, 0)),
                  pl.BlockSpec((tb, GDN_V_HEADS), lambda h, i: (i, 0)),
                  pl.BlockSpec((GDN_CONV, dd), lambda h, i: (0, h)),
                  pl.BlockSpec((GDN_CONV, dd), lambda h, i: (0, kh + h)),
                  pl.BlockSpec((GDN_CONV, 2 * dd), lambda h, i: (0, v0 + h)),
                  pl.BlockSpec((1, dd), lambda h, i: (0, 0))],
        out_specs=pl.BlockSpec((tb, 2 * dd), lambda h, i: (i, h)),
        out_shape=jax.ShapeDtypeStruct((s, GDN_V_W), BF16),
        scratch_shapes=[pltpu.VMEM((8 + tb, dd), F32),
                        pltpu.VMEM((8 + tb, dd), F32),
                        pltpu.VMEM((8 + tb, 2 * dd), F32),
                        pltpu.VMEM((2, dd, dd), F32)],
        compiler_params=_params("parallel", "arbitrary"),
        name="gated_deltanet",
    )(proj, proj, proj, proj, proj, proj, proj, beta, gcum, conv_w, conv_w, conv_w,
      norm_gain.reshape(1, dd))


def _xa_kv_kernel(mem_ref, mg_ref, w_ref, kg_ref, k_ref, v_ref):
    mem_n = (_rms_rows(mem_ref[...]) * mg_ref[...]).astype(BF16)
    kv = _dot(mem_n, w_ref[...])
    for h in range(XA_HEADS):
        k = kv[:, h * XA_DIM:(h + 1) * XA_DIM]
        k_ref[h] = (_rms_rows(k) * kg_ref[...]).astype(k_ref.dtype)
        v_ref[h] = kv[:, (XA_HEADS + h) * XA_DIM:(XA_HEADS + h + 1) * XA_DIM].astype(v_ref.dtype)


def xa_keys_values(mem, mem_gain, w_kv, k_gain):
    m, d = mem.shape
    shape = jax.ShapeDtypeStruct((XA_HEADS, m, XA_DIM), BF16)
    return pl.pallas_call(
        _xa_kv_kernel,
        out_shape=[shape, shape],
        compiler_params=pltpu.CompilerParams(vmem_limit_bytes=VMEM_LIMIT_BYTES),
        name="xa_keys_values",
    )(mem, mem_gain.reshape(1, d), w_kv, k_gain.reshape(1, XA_DIM))


def _xattn_kernel(x_ref, g_ref, wq_ref, qg_ref, k_ref, v_ref, wo_ref, o_ref):
    x = x_ref[...]
    h = (_rms_rows(x) * g_ref[...]).astype(BF16)
    q = _dot(h, wq_ref[...])
    qg = qg_ref[...] * (XA_DIM ** -0.5)
    outs = []
    for hd in range(XA_HEADS):
        qh = (_rms_rows(q[:, hd * XA_DIM:(hd + 1) * XA_DIM]) * qg).astype(BF16)
        s = _dot_nt(qh, k_ref[hd])
        p = jnp.exp(s - jnp.max(s, axis=-1, keepdims=True))
        p = p / jnp.sum(p, axis=-1, keepdims=True)
        outs.append(_dot(p.astype(BF16), v_ref[hd]).astype(BF16))
    o = jnp.concatenate(outs, axis=-1)
    o_ref[...] = x + _dot(o, wo_ref[...])


def cross_attention(x, gain, w_q, q_gain, k, v, w_o, tm):
    s, d = x.shape
    hw = XA_HEADS * XA_DIM
    m = k.shape[1]
    return pl.pallas_call(
        _xattn_kernel,
        grid=(s // tm,),
        in_specs=[pl.BlockSpec((tm, d), lambda i: (i, 0)),
                  pl.BlockSpec((1, d), lambda i: (0, 0)),
                  pl.BlockSpec((d, hw), lambda i: (0, 0)),
                  pl.BlockSpec((1, XA_DIM), lambda i: (0, 0)),
                  pl.BlockSpec((XA_HEADS, m, XA_DIM), lambda i: (0, 0, 0)),
                  pl.BlockSpec((XA_HEADS, m, XA_DIM), lambda i: (0, 0, 0)),
                  pl.BlockSpec((hw, d), lambda i: (0, 0))],
        out_specs=pl.BlockSpec((tm, d), lambda i: (i, 0)),
        out_shape=jax.ShapeDtypeStruct((s, d), F32),
        compiler_params=_params("parallel"),
        name="cross_attention",
    )(x, gain.reshape(1, d), w_q, q_gain.reshape(1, XA_DIM), k, v, w_o)


def _pick_tile(total, target):
    t = min(total, target)
    while total % t:
        t //= 2
    return t


def _split_k(k, target):
    best = LANES
    for t in range(LANES, min(k, target) + 1, LANES):
        if k % t == 0:
            best = t
    return best


def _rotary_tables(s):
    inv = ROPE_BASE ** (-jnp.arange(0, RET_QK_DIM, 2, dtype=F32) / RET_QK_DIM)
    ang = jnp.arange(s).astype(F32)[:, None] * inv[None, :]
    return jnp.cos(ang), jnp.sin(ang)


def _tail(x, layer, mem, mem_norm, norm_xa, norm_ffn, xa_w_q, xa_w_kv, xa_w_o, xa_q_gain, xa_k_gain,
          ffn_w1, ffn_w3, ffn_w2, tm):
    k, v = xa_keys_values(mem, mem_norm, xa_w_kv[layer].astype(BF16), xa_k_gain[layer])
    x = cross_attention(x, norm_xa[layer], xa_w_q[layer].astype(BF16), xa_q_gain[layer], k, v,
                        xa_w_o[layer].astype(BF16), _pick_tile(x.shape[0], 512))
    hid = norm_swiglu(x, norm_ffn[layer], ffn_w1[layer].astype(BF16), ffn_w3[layer].astype(BF16),
                      tm, _pick_tile(ffn_w1.shape[-1], 512))
    return matmul_residual(hid, ffn_w2[layer].astype(BF16), x, tm, _pick_tile(x.shape[1], 1024),
                           _split_k(hid.shape[1], 1536))


def kernel(x, mem, norm_mix, norm_xa, norm_ffn, mem_norm, rel_bias, ar_w_in, ar_w_out, dil_q_gain, dil_k_gain, gdn_w_in, gdn_conv, gdn_a_log, gdn_dt_bias, gdn_norm, gdn_w_out, xa_w_q, xa_w_kv, xa_w_o, xa_q_gain, xa_k_gain, ffn_w1, ffn_w3, ffn_w2):
    b, s, d = x.shape
    assert b == 1
    xs = x.reshape(s, d)
    mem2 = mem.reshape(mem.shape[1], d)
    tm = _pick_tile(s, 1024)
    tail = functools.partial(
        _tail, mem=mem2, mem_norm=mem_norm, norm_xa=norm_xa, norm_ffn=norm_ffn, xa_w_q=xa_w_q,
        xa_w_kv=xa_w_kv, xa_w_o=xa_w_o, xa_q_gain=xa_q_gain, xa_k_gain=xa_k_gain,
        ffn_w1=ffn_w1, ffn_w3=ffn_w3, ffn_w2=ffn_w2, tm=tm)

    proj = norm_matmul(xs, norm_mix[0], ar_w_in[0].astype(BF16), BF16, tm, 1024)
    cos, sin = _rotary_tables(s)
    ya = retention_mixer(proj, cos, sin, _pick_tile(s, 1024))
    qkv = dilated_prep(proj, dil_q_gain[0], dil_k_gain[0], _pick_tile(s, 2048))
    yb = dilated_attention(qkv, rel_bias, _pick_tile(s, 2048))
    y = jnp.concatenate([ya, yb], axis=-1)
    w_out = ar_w_out[0].astype(BF16)
    xs = matmul_residual(y, w_out, xs, tm, _pick_tile(d, 1024), _split_k(y.shape[1], 1536))
    xs = tail(xs, 0)

    w_in = gdn_w_in[0]
    n_main = GDN_QKV + GDN_V_W
    proj = norm_matmul(xs, norm_mix[1], w_in[:, :n_main].astype(BF16), BF16, tm, 1024)
    ab = norm_matmul(xs, norm_mix[1], w_in[:, n_main:].astype(BF16), F32, tm, 2 * GDN_V_HEADS)
    beta, gcum = gdn_gates(ab, gdn_a_log[0], gdn_dt_bias[0], _pick_tile(s, 256))
    og = gdn_mixer(proj, beta, gcum, gdn_conv[0], gdn_norm[0], _pick_tile(s, 256))
    xs = matmul_residual(og, gdn_w_out[0].astype(BF16), xs, tm, _pick_tile(d, 1024),
                         _split_k(og.shape[1], 1536))
    xs = tail(xs, 1)
    return xs.reshape(b, s, d)
```

```python
import functools
import math

import jax
import jax.numpy as jnp
from jax import lax
from jax.experimental import pallas as pl
from jax.experimental.pallas import tpu as pltpu

F32 = jnp.float32
BF16 = jnp.bfloat16
EPS = 1e-6
NEG = -1e30

VMEM_LIMIT_BYTES = 56 * 1024 * 1024
LANES = 128

RET_HEADS = 4
RET_QK_DIM = 256
RET_V_DIM = 512
RET_CHUNK = 128
ROPE_BASE = 10000.0
DIL_HEADS = 8
DIL_DIM = 128
DIL_PATTERNS = ((128, 1), (512, 4), (2048, 16))
DIL_STEPS = 128
REL_BUCKETS = 32
REL_MAX_DIST = 2048
AR_QK = RET_HEADS * RET_QK_DIM
AR_V = RET_HEADS * RET_V_DIM
DIL_W = DIL_HEADS * DIL_DIM
GDN_K_HEADS = 16
GDN_V_HEADS = 32
GDN_DIM = 128
GDN_CONV = 4
GDN_CHUNK = 128
GDN_QK_W = GDN_K_HEADS * GDN_DIM
GDN_V_W = GDN_V_HEADS * GDN_DIM
GDN_QKV = 2 * GDN_QK_W + GDN_V_W
XA_HEADS = 4
XA_DIM = 128


def _params(*semantics):
    return pltpu.CompilerParams(dimension_semantics=semantics, vmem_limit_bytes=VMEM_LIMIT_BYTES)


def _dot(a, b):
    return jnp.dot(a, b, preferred_element_type=F32)


def _dot_nt(a, b):
    return lax.dot_general(a, b, (((1,), (1,)), ((), ())), preferred_element_type=F32)


def _dot_tn(a, b):
    return lax.dot_general(a, b, (((0,), (0,)), ((), ())), preferred_element_type=F32)


def _dot_f32(a, b):
    return jnp.dot(a, b, preferred_element_type=F32, precision=lax.Precision.HIGHEST)


def _dot_nt_f32(a, b):
    return lax.dot_general(a, b, (((1,), (1,)), ((), ())), preferred_element_type=F32,
                           precision=lax.Precision.HIGHEST)


def _rms_rows(x):
    return x * lax.rsqrt(jnp.mean(x * x, axis=-1, keepdims=True) + EPS)


def _silu(x):
    return x * jax.nn.sigmoid(x)


def _norm_matmul_kernel(x_ref, g_ref, w_ref, o_ref, h_ref):
    @pl.when(pl.program_id(1) == 0)
    def _():
        h_ref[...] = (_rms_rows(x_ref[...]) * g_ref[...]).astype(BF16)

    o_ref[...] = _dot(h_ref[...], w_ref[...]).astype(o_ref.dtype)


def norm_matmul(x, gain, w, out_dtype, tm, tn):
    m, k = x.shape
    n = w.shape[1]
    return pl.pallas_call(
        _norm_matmul_kernel,
        grid=(m // tm, n // tn),
        in_specs=[pl.BlockSpec((tm, k), lambda i, j: (i, 0)),
                  pl.BlockSpec((1, k), lambda i, j: (0, 0)),
                  pl.BlockSpec((k, tn), lambda i, j: (0, j))],
        out_specs=pl.BlockSpec((tm, tn), lambda i, j: (i, j)),
        out_shape=jax.ShapeDtypeStruct((m, n), out_dtype),
        scratch_shapes=[pltpu.VMEM((tm, k), BF16)],
        compiler_params=_params("parallel", "arbitrary"),
        name="norm_matmul",
    )(x, gain.reshape(1, k), w)


def _norm_swiglu_kernel(x_ref, g_ref, w1_ref, w3_ref, o_ref, h_ref):
    @pl.when(pl.program_id(1) == 0)
    def _():
        h_ref[...] = (_rms_rows(x_ref[...]) * g_ref[...]).astype(BF16)

    h = h_ref[...]
    a = _dot(h, w1_ref[...])
    b = _dot(h, w3_ref[...])
    o_ref[...] = (_silu(a) * b).astype(o_ref.dtype)


def norm_swiglu(x, gain, w1, w3, tm, tn):
    m, k = x.shape
    n = w1.shape[1]
    return pl.pallas_call(
        _norm_swiglu_kernel,
        grid=(m // tm, n // tn),
        in_specs=[pl.BlockSpec((tm, k), lambda i, j: (i, 0)),
                  pl.BlockSpec((1, k), lambda i, j: (0, 0)),
                  pl.BlockSpec((k, tn), lambda i, j: (0, j)),
                  pl.BlockSpec((k, tn), lambda i, j: (0, j))],
        out_specs=pl.BlockSpec((tm, tn), lambda i, j: (i, j)),
        out_shape=jax.ShapeDtypeStruct((m, n), BF16),
        scratch_shapes=[pltpu.VMEM((tm, k), BF16)],
        compiler_params=_params("parallel", "arbitrary"),
        name="norm_swiglu",
    )(x, gain.reshape(1, k), w1, w3)


def _matmul_residual_kernel(y_ref, w_ref, x_ref, o_ref):
    @pl.when(pl.program_id(2) == 0)
    def _():
        o_ref[...] = x_ref[...]

    o_ref[...] += _dot(y_ref[...], w_ref[...])


def matmul_residual(y, w, x, tm, tn, tk):
    m, k = y.shape
    n = w.shape[1]
    return pl.pallas_call(
        _matmul_residual_kernel,
        grid=(m // tm, n // tn, k // tk),
        in_specs=[pl.BlockSpec((tm, tk), lambda i, j, kk: (i, kk)),
                  pl.BlockSpec((tk, tn), lambda i, j, kk: (kk, j)),
                  pl.BlockSpec((tm, tn), lambda i, j, kk: (i, j))],
        out_specs=pl.BlockSpec((tm, tn), lambda i, j, kk: (i, j)),
        out_shape=jax.ShapeDtypeStruct((m, n), F32),
        compiler_params=_params("parallel", "parallel", "arbitrary"),
        name="matmul_residual",
    )(y, w, x)


def _retention_kernel(lg_ref, q_ref, k_ref, v_ref, g_ref, cos_ref, sin_ref, o_ref, state_ref,
                      *, chunk, n_chunks):
    @pl.when(pl.program_id(1) == 0)
    def _():
        state_ref[...] = jnp.zeros_like(state_ref)

    c = chunk
    half = RET_QK_DIM // 2
    lg = lg_ref[0]
    lg_l = lg[:, :LANES]
    ii = lax.broadcasted_iota(jnp.int32, (c, c), 0)
    jj = lax.broadcasted_iota(jnp.int32, (c, c), 1)
    rel = (ii - jj).astype(F32)
    inner = jnp.where(rel >= 0, jnp.exp(lg[:, :c] * jnp.maximum(rel, 0.0)), 0.0)
    idx = lax.broadcasted_iota(jnp.int32, (c, LANES), 0).astype(F32)
    q_dec = jnp.exp(lg_l * (idx + 1.0))
    k_dec = jnp.exp(lg_l * (c - 1.0 - idx))
    c_dec = jnp.exp(lg * float(c))
    k_scale = RET_QK_DIM ** -0.5

    def rot(t, cos, sin):
        t1, t2 = t[:, :half], t[:, half:]
        return t1 * cos - t2 * sin, t1 * sin + t2 * cos

    for ci in range(n_chunks):
        rows = pl.ds(ci * c, c)
        cos = cos_ref[rows, :]
        sin = sin_ref[rows, :]
        q1, q2 = rot(q_ref[rows, :].astype(F32), cos, sin)
        k1, k2 = rot(k_ref[rows, :].astype(F32) * k_scale, cos, sin)
        v = v_ref[rows, :]
        qb = jnp.concatenate([q1, q2], axis=-1).astype(BF16)
        kb = jnp.concatenate([k1, k2], axis=-1).astype(BF16)
        qd = jnp.concatenate([q1 * q_dec, q2 * q_dec], axis=-1).astype(BF16)
        kd = jnp.concatenate([k1 * k_dec, k2 * k_dec], axis=-1).astype(BF16)
        state = state_ref[...]
        scores = _dot_nt(qb, kb) * inner
        o = _dot(scores.astype(BF16), v) + _dot(qd, state.astype(BF16))
        state_ref[...] = state * c_dec + _dot_tn(kd, v)
        gate = _silu(g_ref[rows, :].astype(F32))
        o_ref[rows, :] = (_rms_rows(o) * gate).astype(o_ref.dtype)


def retention_mixer(proj, cos, sin, rows_per_step):
    s = proj.shape[0]
    t = rows_per_step
    log_gamma = jnp.log(1.0 - 2.0 ** (-5.0 - jnp.arange(RET_HEADS, dtype=F32)))
    lg = jnp.broadcast_to(log_gamma[:, None, None], (RET_HEADS, 1, RET_V_DIM))
    qk_blocks = AR_QK // RET_QK_DIM
    v_off = 2 * AR_QK // RET_V_DIM
    g_off = v_off + AR_V // RET_V_DIM
    kern = functools.partial(_retention_kernel, chunk=RET_CHUNK, n_chunks=t // RET_CHUNK)
    return pl.pallas_call(
        kern,
        grid=(RET_HEADS, s // t),
        in_specs=[pl.BlockSpec((1, 1, RET_V_DIM), lambda h, i: (h, 0, 0)),
                  pl.BlockSpec((t, RET_QK_DIM), lambda h, i: (i, h)),
                  pl.BlockSpec((t, RET_QK_DIM), lambda h, i: (i, qk_blocks + h)),
                  pl.BlockSpec((t, RET_V_DIM), lambda h, i: (i, v_off + h)),
                  pl.BlockSpec((t, RET_V_DIM), lambda h, i: (i, g_off + h)),
                  pl.BlockSpec((t, RET_QK_DIM // 2), lambda h, i: (i, 0)),
                  pl.BlockSpec((t, RET_QK_DIM // 2), lambda h, i: (i, 0))],
        out_specs=pl.BlockSpec((t, RET_V_DIM), lambda h, i: (i, h)),
        out_shape=jax.ShapeDtypeStruct((s, AR_V), BF16),
        scratch_shapes=[pltpu.VMEM((RET_QK_DIM, RET_V_DIM), F32)],
        compiler_params=_params("parallel", "arbitrary"),
        name="retention",
    )(lg, proj, proj, proj, proj, cos, sin)


def _dil_prep_kernel(p_ref, g_ref, o_ref):
    c = pl.program_id(0)
    x = p_ref[...].astype(F32)

    @pl.when(c < 2 * DIL_HEADS)
    def _():
        o_ref[...] = (_rms_rows(x) * g_ref[0]).astype(o_ref.dtype)

    @pl.when(c >= 2 * DIL_HEADS)
    def _():
        o_ref[...] = p_ref[...]


def dilated_prep(proj, q_gain, k_gain, tm):
    s = proj.shape[0]
    col0 = (2 * AR_QK + 2 * AR_V) // DIL_DIM
    gains = jnp.concatenate([
        jnp.broadcast_to(q_gain[None, :] * (DIL_DIM ** -0.5), (DIL_HEADS, DIL_DIM)),
        jnp.broadcast_to(k_gain[None, :], (DIL_HEADS, DIL_DIM)),
        jnp.ones((DIL_HEADS, DIL_DIM), F32)], axis=0).reshape(3 * DIL_HEADS, 1, DIL_DIM)
    return pl.pallas_call(
        _dil_prep_kernel,
        grid=(3 * DIL_HEADS, s // tm),
        in_specs=[pl.BlockSpec((tm, DIL_DIM), lambda c, i: (i, col0 + c)),
                  pl.BlockSpec((1, 1, DIL_DIM), lambda c, i: (c, 0, 0))],
        out_specs=pl.BlockSpec((None, tm, DIL_DIM), lambda c, i: (c, i, 0)),
        out_shape=jax.ShapeDtypeStruct((3 * DIL_HEADS, s, DIL_DIM), BF16),
        compiler_params=_params("parallel", "parallel"),
        name="dilated_prep",
    )(proj, gains)


def _dilated_kernel(bias_ref,
                    q1, k1, k1p, v1, v1p,
                    q4, k4, k4p, v4, v4p,
                    q16, k16, k16p, v16, v16p,
                    o_ref, acc_ref, m_ref, l_ref, *, tile):
    n = DIL_STEPS
    d_dim = DIL_DIM
    first = pl.program_id(1) == 0

    def block(q, kp, kc, vp, vc, bias, mask_prev):
        sp = _dot_nt(q, kp) + bias[:, :n]
        sc = _dot_nt(q, kc) + bias[:, n:]
        if mask_prev:
            sp = jnp.where(first, NEG, sp)
        m = jnp.maximum(jnp.max(sp, axis=-1, keepdims=True), jnp.max(sc, axis=-1, keepdims=True))
        pp = jnp.exp(sp - m)
        pc = jnp.exp(sc - m)
        l = jnp.sum(pp, axis=-1, keepdims=True) + jnp.sum(pc, axis=-1, keepdims=True)
        o = _dot(pp.astype(BF16), vp) + _dot(pc.astype(BF16), vc)
        return m, l, o

    def merge(rows, m, l, o):
        m_old = m_ref[rows, :]
        m_new = jnp.maximum(m_old, m)
        a_old = jnp.exp(m_old - m_new)
        a_cur = jnp.exp(m - m_new)
        return m_new, l_ref[rows, :] * a_old + l * a_cur, acc_ref[rows, :] * a_old + o * a_cur

    bias = bias_ref[2]
    for r in range(16):
        ls = slice(r * d_dim, (r + 1) * d_dim)
        m, l, o = block(q16[:, ls], k16p[:, ls], k16[:, ls], v16p[:, ls], v16[:, ls], bias, True)
        rows = pl.ds(r, n, stride=16)
        m_ref[rows, :] = jnp.broadcast_to(m, (n, d_dim))
        l_ref[rows, :] = jnp.broadcast_to(l, (n, d_dim))
        acc_ref[rows, :] = o

    bias = bias_ref[1]
    nb4 = tile // (4 * n)
    for r in range(4):
        ls = slice(r * d_dim, (r + 1) * d_dim)
        for b in range(nb4):
            cur = pl.ds(b * n, n)
            if b == 0:
                kp, vp = k4p[:, ls], v4p[:, ls]
            else:
                prev = pl.ds((b - 1) * n, n)
                kp, vp = k4[prev, ls], v4[prev, ls]
            m, l, o = block(q4[cur, ls], kp, k4[cur, ls], vp, v4[cur, ls], bias, b == 0)
            rows = pl.ds(b * 4 * n + r, n, stride=4)
            m_new, l_new, acc_new = merge(rows, m, l, o)
            m_ref[rows, :] = m_new
            l_ref[rows, :] = l_new
            acc_ref[rows, :] = acc_new

    bias = bias_ref[0]
    for b in range(tile // n):
        cur = pl.ds(b * n, n)
        if b == 0:
            kp, vp = k1p[...], v1p[...]
        else:
            prev = pl.ds((b - 1) * n, n)
            kp, vp = k1[prev, :], v1[prev, :]
        m, l, o = block(q1[cur, :], kp, k1[cur, :], vp, v1[cur, :], bias, b == 0)
        _, l_new, acc_new = merge(cur, m, l, o)
        o_ref[cur, :] = (acc_new / l_new).astype(o_ref.dtype)


def _t5_bucket(dist):
    exact = REL_BUCKETS // 2
    large = exact + (jnp.log(jnp.maximum(dist, exact).astype(F32) / exact)
                     / math.log(REL_MAX_DIST / exact) * (REL_BUCKETS - exact)).astype(jnp.int32)
    large = jnp.minimum(large, REL_BUCKETS - 1)
    return jnp.where(dist < exact, dist, large)


def _dilated_bias(rel_bias):
    n = DIL_STEPS
    sub = (jnp.arange(n)[:, None] + n) - jnp.arange(2 * n)[None, :]
    band = (sub >= 0) & (sub <= n)
    out = []
    for (_, dil) in DIL_PATTERNS:
        b = rel_bias.astype(F32)[_t5_bucket(jnp.maximum(sub, 0) * dil)]
        out.append(jnp.where(band[None], b.transpose(2, 0, 1), NEG))
    return jnp.stack(out)


def dilated_attention(qkv, rel_bias, tile):
    _, s, dd = qkv.shape
    n = DIL_STEPS
    h_n = DIL_HEADS
    bias = _dilated_bias(rel_bias)
    args, specs = [bias], [pl.BlockSpec((len(DIL_PATTERNS), None, n, 2 * n), lambda h, i: (0, h, 0, 0))]
    for dil in (1, 4, 16):
        view = qkv.reshape(3 * h_n, s // dil, dil * dd)
        rows = tile // dil
        per = rows // n

        def cur_spec(off, rows=rows, dil=dil):
            return pl.BlockSpec((None, rows, dil * dd), lambda h, i: (off + h, i, 0))

        def prev_spec(off, per=per, dil=dil):
            return pl.BlockSpec((None, n, dil * dd), lambda h, i: (off + h, jnp.maximum(i * per - 1, 0), 0))

        args += [view, view, view, view, view]
        specs += [cur_spec(0), cur_spec(h_n), prev_spec(h_n), cur_spec(2 * h_n), prev_spec(2 * h_n)]
    return pl.pallas_call(
        functools.partial(_dilated_kernel, tile=tile),
        grid=(h_n, s // tile),
        in_specs=specs,
        out_specs=pl.BlockSpec((tile, dd), lambda h, i: (i, h)),
        out_shape=jax.ShapeDtypeStruct((s, h_n * dd), BF16),
        scratch_shapes=[pltpu.VMEM((tile, dd), F32)] * 3,
        compiler_params=_params("parallel", "arbitrary"),
        name="dilated_attention",
    )(*args)


def _gdn_gates_kernel(ab_ref, alog_ref, dt_ref, beta_ref, gcum_ref, gct_ref, *, chunk):
    t = ab_ref.shape[0]
    hv = GDN_V_HEADS
    ab = ab_ref[...]
    beta_ref[...] = jax.nn.sigmoid(ab[:, :hv])
    z = ab[:, hv:] + dt_ref[...]
    softplus = jnp.maximum(z, 0.0) + jnp.log1p(jnp.exp(-jnp.abs(z)))
    g = -jnp.exp(alog_ref[...]) * softplus
    ii = lax.broadcasted_iota(jnp.int32, (t, t), 0)
    jj = lax.broadcasted_iota(jnp.int32, (t, t), 1)
    tri = ((ii >= jj) & (ii // chunk == jj // chunk)).astype(F32)
    gcum = _dot_f32(tri, g)
    gcum_ref[...] = gcum
    eh = lax.broadcasted_iota(jnp.int32, (hv, hv), 0) == lax.broadcasted_iota(jnp.int32, (hv, hv), 1)
    gct_ref[...] = _dot_nt_f32(eh.astype(F32), gcum)


def gdn_gates(ab, a_log, dt_bias, tm):
    s = ab.shape[0]
    hv = GDN_V_HEADS
    return pl.pallas_call(
        functools.partial(_gdn_gates_kernel, chunk=GDN_CHUNK),
        grid=(s // tm,),
        in_specs=[pl.BlockSpec((tm, 2 * hv), lambda i: (i, 0)),
                  pl.BlockSpec((1, hv), lambda i: (0, 0)),
                  pl.BlockSpec((1, hv), lambda i: (0, 0))],
        out_specs=[pl.BlockSpec((tm, hv), lambda i: (i, 0)),
                   pl.BlockSpec((tm, hv), lambda i: (i, 0)),
                   pl.BlockSpec((hv, tm), lambda i: (0, i))],
        out_shape=[jax.ShapeDtypeStruct((s, hv), F32), jax.ShapeDtypeStruct((s, hv), F32),
                   jax.ShapeDtypeStruct((hv, s), F32)],
        compiler_params=_params("parallel"),
        name="gdn_gates",
    )(ab, a_log.reshape(1, hv), dt_bias.reshape(1, hv))


def _unit_lower_inverses(a_list):
    c = a_list[0].shape[0]
    ii = lax.broadcasted_iota(jnp.int32, (c, c), 0)
    jj = lax.broadcasted_iota(jnp.int32, (c, c), 1)
    eye = jnp.where(ii == jj, 1.0, 0.0)
    p = [-a for a in a_list]
    t = [eye + x for x in p]
    pb = [x.astype(BF16) for x in p]
    p = [_dot(x, x) for x in pb]
    span = 4
    while span < c:
        pb = [x.astype(BF16) for x in p]
        st = [_dot(jnp.concatenate([x, y.astype(BF16)], axis=0), x) for x, y in zip(pb, t)]
        p = [x[:c] for x in st]
        t = [y + x[c:] for x, y in zip(st, t)]
        span *= 2
    pb = [x.astype(BF16) for x in p]
    return [y + _dot(y.astype(BF16), x) for x, y in zip(pb, t)]


def _gdn_kernel(q_ref, qp_ref, k_ref, kp_ref, v_ref, vp_ref, z_ref, beta_ref, gcum_ref, gct_ref,
                wq_ref, wk_ref, wv_ref, ng_ref, o_ref,
                qbuf, kbuf, vbuf, state_ref, *, tb, chunk, kheads):
    grp = pl.program_id(0)
    first = pl.program_id(1) == 0
    c = chunk
    dd = GDN_DIM
    halo = 8
    nh = 2 * kheads
    nc = tb // c

    @pl.when(first)
    def _():
        state_ref[...] = jnp.zeros_like(state_ref)

    def conv_silu(x_ref, xp_ref, w_ref, buf):
        prev = xp_ref[...].astype(F32)
        buf[0:halo, :] = jnp.where(first, 0.0, prev)
        buf[halo:halo + tb, :] = x_ref[...].astype(F32)
        y = buf[halo:halo + tb, :] * w_ref[GDN_CONV - 1:GDN_CONV, :]
        for j in range(GDN_CONV - 1):
            off = halo - (GDN_CONV - 1) + j
            y = y + buf[off:off + tb, :] * w_ref[j:j + 1, :]
        return _silu(y)

    def l2n(x):
        return x * lax.rsqrt(jnp.sum(x * x, axis=-1, keepdims=True) + EPS)

    qc = conv_silu(q_ref, qp_ref, wq_ref, qbuf)
    kc = conv_silu(k_ref, kp_ref, wk_ref, kbuf)
    vv = conv_silu(v_ref, vp_ref, wv_ref, vbuf)
    qn = [l2n(qc[:, h * dd:(h + 1) * dd]) * (dd ** -0.5) for h in range(kheads)]
    kn = [l2n(kc[:, h * dd:(h + 1) * dd]) for h in range(kheads)]

    lane_h = lax.broadcasted_iota(jnp.int32, (1, GDN_V_HEADS), 1)
    ii = lax.broadcasted_iota(jnp.int32, (c, c), 0)
    jj = lax.broadcasted_iota(jnp.int32, (c, c), 1)
    causal = ii >= jj
    strict = ii > jj
    ng = ng_ref[...]

    beta_t = beta_ref[...]
    gcum_t = gcum_ref[...]
    beta_col, gc_col, gc_row = [], [], []
    for h in range(nh):
        hv = grp * nh + h
        sel = lane_h == hv
        beta_col.append(jnp.sum(jnp.where(sel, beta_t, 0.0), axis=-1, keepdims=True))
        gc_col.append(jnp.sum(jnp.where(sel, gcum_t, 0.0), axis=-1, keepdims=True))
        gc_row.append(gct_ref[pl.ds(hv, 1), :])

    items = [(h, ci) for ci in range(nc) for h in range(nh)]
    pos = {it: n for n, it in enumerate(items)}

    def rows_of(ci):
        return slice(ci * c, (ci + 1) * c)

    qkk = {}
    for ci in range(nc):
        for kh in range(kheads):
            k16 = kn[kh][rows_of(ci)].astype(BF16)
            qk16 = jnp.concatenate([qn[kh][rows_of(ci)].astype(BF16), k16], axis=0)
            qkk[(kh, ci)] = _dot_nt(qk16, k16)

    decay, a_list = [], []
    for (h, ci) in items:
        r = rows_of(ci)
        dlog = gc_col[h][r] - gc_row[h][:, r]
        dec = jnp.where(causal, jnp.exp(jnp.where(causal, dlog, 0.0)), 0.0)
        decay.append(dec)
        a_list.append(jnp.where(strict, qkk[(h // 2, ci)][c:] * beta_col[h][r] * dec, 0.0))
    t_list = _unit_lower_inverses(a_list)

    uw = []
    for (h, ci), t_mat in zip(items, t_list):
        r = rows_of(ci)
        beta = beta_col[h][r]
        rhs = jnp.concatenate([vv[r, h * dd:(h + 1) * dd] * beta,
                               kn[h // 2][r] * (beta * jnp.exp(gc_col[h][r]))], axis=1)
        uw.append(_dot(t_mat.astype(BF16), rhs.astype(BF16)))

    for ci in range(nc):
        r = rows_of(ci)
        states = [state_ref[h] for h in range(nh)]
        s16 = [s.astype(BF16) for s in states]
        wq = []
        for h in range(nh):
            lhs = jnp.concatenate([uw[pos[(h, ci)]][:, dd:],
                                   qn[h // 2][r] * jnp.exp(gc_col[h][r])], axis=0)
            wq.append(_dot(lhs.astype(BF16), s16[h]))
        vn16 = [(uw[pos[(h, ci)]][:, :dd] - wq[h][:c]).astype(BF16) for h in range(nh)]
        outs = []
        for h in range(nh):
            intra = jnp.where(causal, qkk[(h // 2, ci)][:c] * decay[pos[(h, ci)]], 0.0)
            outs.append(wq[h][c:] + _dot(intra.astype(BF16), vn16[h]))
        for h in range(nh):
            gc = gc_col[h][r]
            g_last = gc[c - 1:c, :]
            kd = kn[h // 2][r] * jnp.exp(g_last - gc)
            state_ref[h] = states[h] * jnp.exp(g_last) + _dot_tn(kd.astype(BF16), vn16[h])
        for h in range(nh):
            zg = _silu(z_ref[r, h * dd:(h + 1) * dd].astype(F32))
            o_ref[r, h * dd:(h + 1) * dd] = (_rms_rows(outs[h]) * ng * zg).astype(o_ref.dtype)


def gdn_mixer(proj, beta, gcum, gct, conv_w, norm_gain, tb, kheads):
    s = proj.shape[0]
    dd = GDN_DIM
    wk = kheads * dd
    wv = 2 * wk
    k0 = GDN_QK_W // wk
    v0 = 2 * GDN_QK_W // wv
    z0 = GDN_QKV // wv
    per = tb // 8

    def prev_rows(i):
        return jnp.maximum(i * per - 1, 0)

    return pl.pallas_call(
        functools.partial(_gdn_kernel, tb=tb, chunk=GDN_CHUNK, kheads=kheads),
        grid=(GDN_K_HEADS // kheads, s // tb),
        in_specs=[pl.BlockSpec((tb, wk), lambda g, i: (i, g)),
                  pl.BlockSpec((8, wk), lambda g, i: (prev_rows(i), g)),
                  pl.BlockSpec((tb, wk), lambda g, i: (i, k0 + g)),
                  pl.BlockSpec((8, wk), lambda g, i: (prev_rows(i), k0 + g)),
                  pl.BlockSpec((tb, wv), lambda g, i: (i, v0 + g)),
                  pl.BlockSpec((8, wv), lambda g, i: (prev_rows(i), v0 + g)),
                  pl.BlockSpec((tb, wv), lambda g, i: (i, z0 + g)),
                  pl.BlockSpec((tb, GDN_V_HEADS), lambda g, i: (i, 0)),
                  pl.BlockSpec((tb, GDN_V_HEADS), lambda g, i: (i, 0)),
                  pl.BlockSpec((GDN_V_HEADS, tb), lambda g, i: (0, i)),
                  pl.BlockSpec((GDN_CONV, wk), lambda g, i: (0, g)),
                  pl.BlockSpec((GDN_CONV, wk), lambda g, i: (0, k0 + g)),
                  pl.BlockSpec((GDN_CONV, wv), lambda g, i: (0, v0 + g)),
                  pl.BlockSpec((1, dd), lambda g, i: (0, 0))],
        out_specs=pl.BlockSpec((tb, wv), lambda g, i: (i, g)),
        out_shape=jax.ShapeDtypeStruct((s, GDN_V_W), BF16),
        scratch_shapes=[pltpu.VMEM((8 + tb, wk), F32),
                        pltpu.VMEM((8 + tb, wk), F32),
                        pltpu.VMEM((8 + tb, wv), F32),
                        pltpu.VMEM((2 * kheads, dd, dd), F32)],
        compiler_params=_params("parallel", "arbitrary"),
        name="gated_deltanet",
    )(proj, proj, proj, proj, proj, proj, proj, beta, gcum, gct, conv_w, conv_w, conv_w,
      norm_gain.reshape(1, dd))


def _xa_kv_kernel(mem_ref, mg_ref, w_ref, kg_ref, k_ref, v_ref):
    mem_n = (_rms_rows(mem_ref[...]) * mg_ref[...]).astype(BF16)
    kv = _dot(mem_n, w_ref[...])
    for h in range(XA_HEADS):
        k = kv[:, h * XA_DIM:(h + 1) * XA_DIM]
        k_ref[h] = (_rms_rows(k) * kg_ref[...]).astype(k_ref.dtype)
        v_ref[h] = kv[:, (XA_HEADS + h) * XA_DIM:(XA_HEADS + h + 1) * XA_DIM].astype(v_ref.dtype)


def xa_keys_values(mem, mem_gain, w_kv, k_gain):
    m, d = mem.shape
    shape = jax.ShapeDtypeStruct((XA_HEADS, m, XA_DIM), BF16)
    return pl.pallas_call(
        _xa_kv_kernel,
        out_shape=[shape, shape],
        compiler_params=pltpu.CompilerParams(vmem_limit_bytes=VMEM_LIMIT_BYTES),
        name="xa_keys_values",
    )(mem, mem_gain.reshape(1, d), w_kv, k_gain.reshape(1, XA_DIM))


def _xattn_kernel(x_ref, g_ref, wq_ref, qg_ref, k_ref, v_ref, wo_ref, o_ref):
    x = x_ref[...]
    h = (_rms_rows(x) * g_ref[...]).astype(BF16)
    q = _dot(h, wq_ref[...])
    qg = qg_ref[...] * (XA_DIM ** -0.5)
    outs = []
    for hd in range(XA_HEADS):
        qh = (_rms_rows(q[:, hd * XA_DIM:(hd + 1) * XA_DIM]) * qg).astype(BF16)
        s = _dot_nt(qh, k_ref[hd])
        p = jnp.exp(s - jnp.max(s, axis=-1, keepdims=True))
        p = p / jnp.sum(p, axis=-1, keepdims=True)
        outs.append(_dot(p.astype(BF16), v_ref[hd]).astype(BF16))
    o = jnp.concatenate(outs, axis=-1)
    o_ref[...] = x + _dot(o, wo_ref[...])


def cross_attention(x, gain, w_q, q_gain, k, v, w_o, tm):
    s, d = x.shape
    hw = XA_HEADS * XA_DIM
    m = k.shape[1]
    return pl.pallas_call(
        _xattn_kernel,
        grid=(s // tm,),
        in_specs=[pl.BlockSpec((tm, d), lambda i: (i, 0)),
                  pl.BlockSpec((1, d), lambda i: (0, 0)),
                  pl.BlockSpec((d, hw), lambda i: (0, 0)),
                  pl.BlockSpec((1, XA_DIM), lambda i: (0, 0)),
                  pl.BlockSpec((XA_HEADS, m, XA_DIM), lambda i: (0, 0, 0)),
                  pl.BlockSpec((XA_HEADS, m, XA_DIM), lambda i: (0, 0, 0)),
                  pl.BlockSpec((hw, d), lambda i: (0, 0))],
        out_specs=pl.BlockSpec((tm, d), lambda i: (i, 0)),
        out_shape=jax.ShapeDtypeStruct((s, d), F32),
        compiler_params=_params("parallel"),
        name="cross_attention",
    )(x, gain.reshape(1, d), w_q, q_gain.reshape(1, XA_DIM), k, v, w_o)


def _pick_tile(total, target):
    t = min(total, target)
    while total % t:
        t //= 2
    return t


def _split_k(k, target):
    best = LANES
    for t in range(LANES, min(k, target) + 1, LANES):
        if k % t == 0:
            best = t
    return best


def _rotary_tables(s):
    inv = ROPE_BASE ** (-jnp.arange(0, RET_QK_DIM, 2, dtype=F32) / RET_QK_DIM)
    ang = jnp.arange(s).astype(F32)[:, None] * inv[None, :]
    return jnp.cos(ang), jnp.sin(ang)


def _tail(x, layer, mem, mem_norm, norm_xa, norm_ffn, xa_w_q, xa_w_kv, xa_w_o, xa_q_gain, xa_k_gain,
          ffn_w1, ffn_w3, ffn_w2, tm):
    k, v = xa_keys_values(mem, mem_norm, xa_w_kv[layer].astype(BF16), xa_k_gain[layer])
    x = cross_attention(x, norm_xa[layer], xa_w_q[layer].astype(BF16), xa_q_gain[layer], k, v,
                        xa_w_o[layer].astype(BF16), _pick_tile(x.shape[0], 512))
    hid = norm_swiglu(x, norm_ffn[layer], ffn_w1[layer].astype(BF16), ffn_w3[layer].astype(BF16),
                      tm, _pick_tile(ffn_w1.shape[-1], 512))
    return matmul_residual(hid, ffn_w2[layer].astype(BF16), x, tm, _pick_tile(x.shape[1], 1024),
                           _split_k(hid.shape[1], 1536))


def kernel(x, mem, norm_mix, norm_xa, norm_ffn, mem_norm, rel_bias, ar_w_in, ar_w_out, dil_q_gain, dil_k_gain, gdn_w_in, gdn_conv, gdn_a_log, gdn_dt_bias, gdn_norm, gdn_w_out, xa_w_q, xa_w_kv, xa_w_o, xa_q_gain, xa_k_gain, ffn_w1, ffn_w3, ffn_w2):
    b, s, d = x.shape
    assert b == 1
    xs = x.reshape(s, d)
    mem2 = mem.reshape(mem.shape[1], d)
    tm = _pick_tile(s, 1024)
    tail = functools.partial(
        _tail, mem=mem2, mem_norm=mem_norm, norm_xa=norm_xa, norm_ffn=norm_ffn, xa_w_q=xa_w_q,
        xa_w_kv=xa_w_kv, xa_w_o=xa_w_o, xa_q_gain=xa_q_gain, xa_k_gain=xa_k_gain,
        ffn_w1=ffn_w1, ffn_w3=ffn_w3, ffn_w2=ffn_w2, tm=tm)

    proj = norm_matmul(xs, norm_mix[0], ar_w_in[0].astype(BF16), BF16, tm, 1024)
    cos, sin = _rotary_tables(s)
    ya = retention_mixer(proj, cos, sin, _pick_tile(s, 1024))
    qkv = dilated_prep(proj, dil_q_gain[0], dil_k_gain[0], _pick_tile(s, 2048))
    yb = dilated_attention(qkv, rel_bias, _pick_tile(s, 2048))
    y = jnp.concatenate([ya, yb], axis=-1)
    w_out = ar_w_out[0].astype(BF16)
    xs = matmul_residual(y, w_out, xs, tm, _pick_tile(d, 1024), _split_k(y.shape[1], 1536))
    xs = tail(xs, 0)

    w_in = gdn_w_in[0]
    n_main = GDN_QKV + GDN_V_W
    proj = norm_matmul(xs, norm_mix[1], w_in[:, :n_main].astype(BF16), BF16, tm, 1024)
    ab = norm_matmul(xs, norm_mix[1], w_in[:, n_main:].astype(BF16), F32, tm, 2 * GDN_V_HEADS)
    beta, gcum, gct = gdn_gates(ab, gdn_a_log[0], gdn_dt_bias[0], _pick_tile(s, 256))
    og = gdn_mixer(proj, beta, gcum, gct, gdn_conv[0], gdn_norm[0], GDN_CHUNK, 4)
    xs = matmul_residual(og, gdn_w_out[0].astype(BF16), xs, tm, _pick_tile(d, 1024),
                         _split_k(og.shape[1], 1536))
    xs = tail(xs, 1)
    return xs.reshape(b, s, d)
```

```python
import functools
import math

import jax
import jax.numpy as jnp
from jax import lax
from jax.experimental import pallas as pl
from jax.experimental.pallas import tpu as pltpu

F32 = jnp.float32
BF16 = jnp.bfloat16
EPS = 1e-6
NEG = -1e30

VMEM_LIMIT_BYTES = 56 * 1024 * 1024
LANES = 128

RET_HEADS = 4
RET_QK_DIM = 256
RET_V_DIM = 512
RET_CHUNK = 128
ROPE_BASE = 10000.0
DIL_HEADS = 8
DIL_DIM = 128
DIL_PATTERNS = ((128, 1), (512, 4), (2048, 16))
DIL_STEPS = 128
REL_BUCKETS = 32
REL_MAX_DIST = 2048
AR_QK = RET_HEADS * RET_QK_DIM
AR_V = RET_HEADS * RET_V_DIM
DIL_W = DIL_HEADS * DIL_DIM
GDN_K_HEADS = 16
GDN_V_HEADS = 32
GDN_DIM = 128
GDN_CONV = 4
GDN_CHUNK = 128
GDN_QK_W = GDN_K_HEADS * GDN_DIM
GDN_V_W = GDN_V_HEADS * GDN_DIM
GDN_QKV = 2 * GDN_QK_W + GDN_V_W
XA_HEADS = 4
XA_DIM = 128


def _params(*semantics):
    return pltpu.CompilerParams(dimension_semantics=semantics, vmem_limit_bytes=VMEM_LIMIT_BYTES)


def _dot(a, b):
    return jnp.dot(a, b, preferred_element_type=F32)


def _dot_nt(a, b):
    return lax.dot_general(a, b, (((1,), (1,)), ((), ())), preferred_element_type=F32)


def _dot_tn(a, b):
    return lax.dot_general(a, b, (((0,), (0,)), ((), ())), preferred_element_type=F32)


def _dot_f32(a, b):
    return jnp.dot(a, b, preferred_element_type=F32, precision=lax.Precision.HIGHEST)


def _dot_nt_f32(a, b):
    return lax.dot_general(a, b, (((1,), (1,)), ((), ())), preferred_element_type=F32,
                           precision=lax.Precision.HIGHEST)


def _rms_rows(x):
    return x * lax.rsqrt(jnp.mean(x * x, axis=-1, keepdims=True) + EPS)


def _silu(x):
    return x * jax.nn.sigmoid(x)


def _norm_matmul_kernel(x_ref, g_ref, w_ref, o_ref, h_ref):
    @pl.when(pl.program_id(1) == 0)
    def _():
        h_ref[...] = (_rms_rows(x_ref[...]) * g_ref[...]).astype(BF16)

    o_ref[...] = _dot(h_ref[...], w_ref[...]).astype(o_ref.dtype)


def norm_matmul(x, gain, w, out_dtype, tm, tn, col0=0, n_out=None):
    m, k = x.shape
    n = w.shape[1] if n_out is None else n_out
    return pl.pallas_call(
        _norm_matmul_kernel,
        grid=(m // tm, n // tn),
        in_specs=[pl.BlockSpec((tm, k), lambda i, j: (i, 0)),
                  pl.BlockSpec((1, k), lambda i, j: (0, 0)),
                  pl.BlockSpec((k, tn), lambda i, j: (0, col0 + j))],
        out_specs=pl.BlockSpec((tm, tn), lambda i, j: (i, j)),
        out_shape=jax.ShapeDtypeStruct((m, n), out_dtype),
        scratch_shapes=[pltpu.VMEM((tm, k), BF16)],
        compiler_params=_params("parallel", "arbitrary"),
        name="norm_matmul",
    )(x, gain.reshape(1, k), w)


def _norm_swiglu_kernel(x_ref, g_ref, w1_ref, w3_ref, o_ref, h_ref):
    @pl.when(pl.program_id(1) == 0)
    def _():
        h_ref[...] = (_rms_rows(x_ref[...]) * g_ref[...]).astype(BF16)

    h = h_ref[...]
    a = _dot(h, w1_ref[...])
    b = _dot(h, w3_ref[...])
    o_ref[...] = (_silu(a) * b).astype(o_ref.dtype)


def norm_swiglu(x, gain, w1, w3, tm, tn):
    m, k = x.shape
    n = w1.shape[1]
    return pl.pallas_call(
        _norm_swiglu_kernel,
        grid=(m // tm, n // tn),
        in_specs=[pl.BlockSpec((tm, k), lambda i, j: (i, 0)),
                  pl.BlockSpec((1, k), lambda i, j: (0, 0)),
                  pl.BlockSpec((k, tn), lambda i, j: (0, j)),
                  pl.BlockSpec((k, tn), lambda i, j: (0, j))],
        out_specs=pl.BlockSpec((tm, tn), lambda i, j: (i, j)),
        out_shape=jax.ShapeDtypeStruct((m, n), BF16),
        scratch_shapes=[pltpu.VMEM((tm, k), BF16)],
        compiler_params=_params("parallel", "arbitrary"),
        name="norm_swiglu",
    )(x, gain.reshape(1, k), w1, w3)


def _matmul_residual_kernel(y_ref, w_ref, x_ref, o_ref):
    o_ref[...] = x_ref[...] + _dot(y_ref[...], w_ref[...])


def matmul_residual(y, w, x, tm, tn):
    m, k = y.shape
    n = w.shape[1]
    return pl.pallas_call(
        _matmul_residual_kernel,
        grid=(m // tm, n // tn),
        in_specs=[pl.BlockSpec((tm, k), lambda i, j: (i, 0)),
                  pl.BlockSpec((k, tn), lambda i, j: (0, j)),
                  pl.BlockSpec((tm, tn), lambda i, j: (i, j))],
        out_specs=pl.BlockSpec((tm, tn), lambda i, j: (i, j)),
        out_shape=jax.ShapeDtypeStruct((m, n), F32),
        compiler_params=_params("parallel", "parallel"),
        name="matmul_residual",
    )(y, w, x)


def _matmul2_residual_kernel(ya_ref, yb_ref, wa_ref, wb_ref, x_ref, o_ref):
    o_ref[...] = x_ref[...] + _dot(ya_ref[...], wa_ref[...]) + _dot(yb_ref[...], wb_ref[...])


def matmul2_residual(ya, yb, w, x, tm, tn):
    m, ka = ya.shape
    kb = yb.shape[1]
    n = w.shape[1]
    assert ka % kb == 0
    return pl.pallas_call(
        _matmul2_residual_kernel,
        grid=(m // tm, n // tn),
        in_specs=[pl.BlockSpec((tm, ka), lambda i, j: (i, 0)),
                  pl.BlockSpec((tm, kb), lambda i, j: (i, 0)),
                  pl.BlockSpec((ka, tn), lambda i, j: (0, j)),
                  pl.BlockSpec((kb, tn), lambda i, j: (ka // kb, j)),
                  pl.BlockSpec((tm, tn), lambda i, j: (i, j))],
        out_specs=pl.BlockSpec((tm, tn), lambda i, j: (i, j)),
        out_shape=jax.ShapeDtypeStruct((m, n), F32),
        compiler_params=_params("parallel", "parallel"),
        name="matmul2_residual",
    )(ya, yb, w, w, x)


def _retention_kernel(lg_ref, q_ref, k_ref, v_ref, g_ref, cos_ref, sin_ref, o_ref, state_ref,
                      *, chunk, n_chunks):
    @pl.when(pl.program_id(1) == 0)
    def _():
        state_ref[...] = jnp.zeros_like(state_ref)

    c = chunk
    half = RET_QK_DIM // 2
    lg = lg_ref[0]
    lg_l = lg[:, :LANES]
    ii = lax.broadcasted_iota(jnp.int32, (c, c), 0)
    jj = lax.broadcasted_iota(jnp.int32, (c, c), 1)
    rel = (ii - jj).astype(F32)
    inner = jnp.where(rel >= 0, jnp.exp(lg[:, :c] * jnp.maximum(rel, 0.0)), 0.0)
    idx = lax.broadcasted_iota(jnp.int32, (c, LANES), 0).astype(F32)
    q_dec = jnp.exp(lg_l * (idx + 1.0))
    k_dec = jnp.exp(lg_l * (c - 1.0 - idx))
    c_dec = jnp.exp(lg * float(c))
    k_scale = RET_QK_DIM ** -0.5

    def rot(t, cos, sin):
        t1, t2 = t[:, :half], t[:, half:]
        return t1 * cos - t2 * sin, t1 * sin + t2 * cos

    for ci in range(n_chunks):
        rows = pl.ds(ci * c, c)
        cos = cos_ref[rows, :]
        sin = sin_ref[rows, :]
        q1, q2 = rot(q_ref[rows, :].astype(F32), cos, sin)
        k1, k2 = rot(k_ref[rows, :].astype(F32) * k_scale, cos, sin)
        v = v_ref[rows, :]
        qb = jnp.concatenate([q1, q2], axis=-1).astype(BF16)
        kb = jnp.concatenate([k1, k2], axis=-1).astype(BF16)
        qd = jnp.concatenate([q1 * q_dec, q2 * q_dec], axis=-1).astype(BF16)
        kd = jnp.concatenate([k1 * k_dec, k2 * k_dec], axis=-1).astype(BF16)
        state = state_ref[...]
        scores = _dot_nt(qb, kb) * inner
        o = _dot(scores.astype(BF16), v) + _dot(qd, state.astype(BF16))
        state_ref[...] = state * c_dec + _dot_tn(kd, v)
        gate = _silu(g_ref[rows, :].astype(F32))
        o_ref[rows, :] = (_rms_rows(o) * gate).astype(o_ref.dtype)


def retention_mixer(proj, cos, sin, rows_per_step):
    s = proj.shape[0]
    t = rows_per_step
    log_gamma = jnp.log(1.0 - 2.0 ** (-5.0 - jnp.arange(RET_HEADS, dtype=F32)))
    lg = jnp.broadcast_to(log_gamma[:, None, None], (RET_HEADS, 1, RET_V_DIM))
    qk_blocks = AR_QK // RET_QK_DIM
    v_off = 2 * AR_QK // RET_V_DIM
    g_off = v_off + AR_V // RET_V_DIM
    kern = functools.partial(_retention_kernel, chunk=RET_CHUNK, n_chunks=t // RET_CHUNK)
    return pl.pallas_call(
        kern,
        grid=(RET_HEADS, s // t),
        in_specs=[pl.BlockSpec((1, 1, RET_V_DIM), lambda h, i: (h, 0, 0)),
                  pl.BlockSpec((t, RET_QK_DIM), lambda h, i: (i, h)),
                  pl.BlockSpec((t, RET_QK_DIM), lambda h, i: (i, qk_blocks + h)),
                  pl.BlockSpec((t, RET_V_DIM), lambda h, i: (i, v_off + h)),
                  pl.BlockSpec((t, RET_V_DIM), lambda h, i: (i, g_off + h)),
                  pl.BlockSpec((t, RET_QK_DIM // 2), lambda h, i: (i, 0)),
                  pl.BlockSpec((t, RET_QK_DIM // 2), lambda h, i: (i, 0))],
        out_specs=pl.BlockSpec((t, RET_V_DIM), lambda h, i: (i, h)),
        out_shape=jax.ShapeDtypeStruct((s, AR_V), BF16),
        scratch_shapes=[pltpu.VMEM((RET_QK_DIM, RET_V_DIM), F32)],
        compiler_params=_params("parallel", "arbitrary"),
        name="retention",
    )(lg, proj, proj, proj, proj, cos, sin)


def _dil_prep_kernel(p_ref, g_ref, o1_ref, o4_ref, o16_ref, buf):
    c = pl.program_id(0)
    tm = p_ref.shape[0]
    dd = DIL_DIM
    x = p_ref[...].astype(F32)

    @pl.when(c < 2 * DIL_HEADS)
    def _():
        buf[...] = _rms_rows(x) * g_ref[0]

    @pl.when(c >= 2 * DIL_HEADS)
    def _():
        buf[...] = x

    o1_ref[...] = buf[...].astype(o1_ref.dtype)
    for r in range(4):
        o4_ref[:, r * dd:(r + 1) * dd] = buf[pl.ds(r, tm // 4, stride=4), :].astype(o4_ref.dtype)
    for r in range(16):
        o16_ref[:, r * dd:(r + 1) * dd] = buf[pl.ds(r, tm // 16, stride=16), :].astype(o16_ref.dtype)


def dilated_prep(proj, q_gain, k_gain, tm):
    s = proj.shape[0]
    dd = DIL_DIM
    col0 = (2 * AR_QK + 2 * AR_V) // dd
    nh = 3 * DIL_HEADS
    gains = jnp.concatenate([
        jnp.broadcast_to(q_gain[None, :] * (dd ** -0.5), (DIL_HEADS, dd)),
        jnp.broadcast_to(k_gain[None, :], (DIL_HEADS, dd)),
        jnp.ones((DIL_HEADS, dd), F32)], axis=0).reshape(nh, 1, dd)
    return pl.pallas_call(
        _dil_prep_kernel,
        grid=(nh, s // tm),
        in_specs=[pl.BlockSpec((tm, dd), lambda c, i: (i, col0 + c)),
                  pl.BlockSpec((1, 1, dd), lambda c, i: (c, 0, 0))],
        out_specs=[pl.BlockSpec((None, tm // d, d * dd), lambda c, i: (c, i, 0)) for d in (1, 4, 16)],
        out_shape=[jax.ShapeDtypeStruct((nh, s // d, d * dd), BF16) for d in (1, 4, 16)],
        scratch_shapes=[pltpu.VMEM((tm, dd), F32)],
        compiler_params=_params("parallel", "parallel"),
        name="dilated_prep",
    )(proj, gains)


def _dilated_kernel(bias_ref,
                    q1, k1, k1p, v1, v1p,
                    q4, k4, k4p, v4, v4p,
                    q16, k16, k16p, v16, v16p,
                    o_ref, acc_ref, m_ref, l_ref, *, tile, group):
    n = DIL_STEPS
    d_dim = DIL_DIM
    first = pl.program_id(1) == 0

    def run_group(bias_idx, blocks, mode):
        def ld(p):
            return p[0][p[1]]

        bias_p = bias_ref[bias_idx, :, :n]
        bias_c = bias_ref[bias_idx, :, n:]
        sp = [_dot_nt(ld(b[0]), ld(b[1])) + bias_p for b in blocks]
        sc = [_dot_nt(ld(b[0]), ld(b[2])) + bias_c for b in blocks]
        sp = [jnp.where(first, NEG, x) if b[5] else x for x, b in zip(sp, blocks)]
        m = [jnp.maximum(jnp.max(x, axis=-1, keepdims=True), jnp.max(y, axis=-1, keepdims=True))
             for x, y in zip(sp, sc)]
        pp = [jnp.exp(x - mm) for x, mm in zip(sp, m)]
        pc = [jnp.exp(x - mm) for x, mm in zip(sc, m)]
        l = [jnp.sum(x, axis=-1, keepdims=True) + jnp.sum(y, axis=-1, keepdims=True)
             for x, y in zip(pp, pc)]
        o = [_dot(x.astype(BF16), ld(b[3])) + _dot(y.astype(BF16), ld(b[4]))
             for x, y, b in zip(pp, pc, blocks)]
        for b, mm, ll, oo in zip(blocks, m, l, o):
            rows = b[6]
            if mode == "init":
                m_ref[rows, :] = jnp.broadcast_to(mm, (n, d_dim))
                l_ref[rows, :] = jnp.broadcast_to(ll, (n, d_dim))
                acc_ref[rows, :] = oo
                continue
            m_old = m_ref[rows, :]
            m_new = jnp.maximum(m_old, mm)
            a_old = jnp.exp(m_old - m_new)
            a_cur = jnp.exp(mm - m_new)
            l_new = l_ref[rows, :] * a_old + ll * a_cur
            acc_new = acc_ref[rows, :] * a_old + oo * a_cur
            if mode == "merge":
                m_ref[rows, :] = m_new
                l_ref[rows, :] = l_new
                acc_ref[rows, :] = acc_new
            else:
                o_ref[rows, :] = (acc_new / l_new).astype(o_ref.dtype)

    every = slice(None)

    blocks = []
    for r in range(16):
        ls = slice(r * d_dim, (r + 1) * d_dim)
        blocks.append(((q16, (every, ls)), (k16p, (every, ls)), (k16, (every, ls)),
                       (v16p, (every, ls)), (v16, (every, ls)), True, pl.ds(r, n, stride=16)))
    for g0 in range(0, len(blocks), group):
        run_group(2, blocks[g0:g0 + group], "init")

    blocks = []
    for r in range(4):
        ls = slice(r * d_dim, (r + 1) * d_dim)
        for b in range(tile // (4 * n)):
            cur = pl.ds(b * n, n)
            prev = pl.ds((b - 1) * n, n)
            kp, vp = (((k4p, (every, ls)), (v4p, (every, ls))) if b == 0
                      else ((k4, (prev, ls)), (v4, (prev, ls))))
            blocks.append(((q4, (cur, ls)), kp, (k4, (cur, ls)), vp, (v4, (cur, ls)), b == 0,
                           pl.ds(b * 4 * n + r, n, stride=4)))
    for g0 in range(0, len(blocks), group):
        run_group(1, blocks[g0:g0 + group], "merge")

    blocks = []
    for b in range(tile // n):
        cur = pl.ds(b * n, n)
        prev = pl.ds((b - 1) * n, n)
        kp, vp = (((k1p, (every, every)), (v1p, (every, every))) if b == 0
                  else ((k1, (prev, every)), (v1, (prev, every))))
        blocks.append(((q1, (cur, every)), kp, (k1, (cur, every)), vp, (v1, (cur, every)), b == 0, cur))
    for g0 in range(0, len(blocks), group):
        run_group(0, blocks[g0:g0 + group], "final")


def _t5_bucket(dist):
    exact = REL_BUCKETS // 2
    large = exact + (jnp.log(jnp.maximum(dist, exact).astype(F32) / exact)
                     / math.log(REL_MAX_DIST / exact) * (REL_BUCKETS - exact)).astype(jnp.int32)
    large = jnp.minimum(large, REL_BUCKETS - 1)
    return jnp.where(dist < exact, dist, large)


def _dilated_bias(rel_bias):
    n = DIL_STEPS
    steps = jnp.arange(n + 1)
    vec = jnp.stack([rel_bias.astype(F32)[_t5_bucket(steps * dil)] for (_, dil) in DIL_PATTERNS])
    vec = vec.transpose(0, 2, 1)
    width = 3 * n
    p = jnp.full(vec.shape[:2] + (width,), NEG, F32)
    p = p.at[:, :, n - 1:2 * n].set(vec[:, :, ::-1])
    tiled = jnp.broadcast_to(p[:, :, None, :], vec.shape[:2] + (n, width))
    skew = tiled.reshape(vec.shape[:2] + (n * width,))[:, :, :n * (width - 1)]
    skew = skew.reshape(vec.shape[:2] + (n, width - 1))
    return skew[:, :, :, n - 1:3 * n - 1]


def dilated_attention(qkv1, qkv4, qkv16, rel_bias, tile, group):
    _, s, dd = qkv1.shape
    n = DIL_STEPS
    h_n = DIL_HEADS
    bias = _dilated_bias(rel_bias)
    args, specs = [bias], [pl.BlockSpec((len(DIL_PATTERNS), None, n, 2 * n), lambda h, i: (0, h, 0, 0))]
    for dil, view in ((1, qkv1), (4, qkv4), (16, qkv16)):
        rows = tile // dil
        per = rows // n

        def cur_spec(off, rows=rows, dil=dil):
            return pl.BlockSpec((None, rows, dil * dd), lambda h, i: (off + h, i, 0))

        def prev_spec(off, per=per, dil=dil):
            return pl.BlockSpec((None, n, dil * dd), lambda h, i: (off + h, jnp.maximum(i * per - 1, 0), 0))

        args += [view, view, view, view, view]
        specs += [cur_spec(0), cur_spec(h_n), prev_spec(h_n), cur_spec(2 * h_n), prev_spec(2 * h_n)]
    return pl.pallas_call(
        functools.partial(_dilated_kernel, tile=tile, group=group),
        grid=(h_n, s // tile),
        in_specs=specs,
        out_specs=pl.BlockSpec((tile, dd), lambda h, i: (i, h)),
        out_shape=jax.ShapeDtypeStruct((s, h_n * dd), BF16),
        scratch_shapes=[pltpu.VMEM((tile, dd), F32)] * 3,
        compiler_params=_params("parallel", "arbitrary"),
        name="dilated_attention",
    )(*args)


def _gdn_gates_kernel(ab_ref, alog_ref, dt_ref, beta_ref, gcum_ref, gct_ref, *, chunk):
    t = ab_ref.shape[0]
    hv = GDN_V_HEADS
    ab = ab_ref[...]
    beta_ref[...] = jax.nn.sigmoid(ab[:, :hv])
    z = ab[:, hv:] + dt_ref[...]
    softplus = jnp.maximum(z, 0.0) + jnp.log1p(jnp.exp(-jnp.abs(z)))
    g = -jnp.exp(alog_ref[...]) * softplus
    ii = lax.broadcasted_iota(jnp.int32, (t, t), 0)
    jj = lax.broadcasted_iota(jnp.int32, (t, t), 1)
    tri = ((ii >= jj) & (ii // chunk == jj // chunk)).astype(F32)
    gcum = _dot_f32(tri, g)
    gcum_ref[...] = gcum
    eh = lax.broadcasted_iota(jnp.int32, (hv, hv), 0) == lax.broadcasted_iota(jnp.int32, (hv, hv), 1)
    gct_ref[...] = _dot_nt_f32(eh.astype(F32), gcum)


def gdn_gates(ab, a_log, dt_bias, tm):
    s = ab.shape[0]
    hv = GDN_V_HEADS
    return pl.pallas_call(
        functools.partial(_gdn_gates_kernel, chunk=GDN_CHUNK),
        grid=(s // tm,),
        in_specs=[pl.BlockSpec((tm, 2 * hv), lambda i: (i, 0)),
                  pl.BlockSpec((1, hv), lambda i: (0, 0)),
                  pl.BlockSpec((1, hv), lambda i: (0, 0))],
        out_specs=[pl.BlockSpec((tm, hv), lambda i: (i, 0)),
                   pl.BlockSpec((tm, hv), lambda i: (i, 0)),
                   pl.BlockSpec((hv, tm), lambda i: (0, i))],
        out_shape=[jax.ShapeDtypeStruct((s, hv), F32), jax.ShapeDtypeStruct((s, hv), F32),
                   jax.ShapeDtypeStruct((hv, s), F32)],
        compiler_params=_params("parallel"),
        name="gdn_gates",
    )(ab, a_log.reshape(1, hv), dt_bias.reshape(1, hv))


def _unit_lower_inverses(a_list):
    c = a_list[0].shape[0]
    ii = lax.broadcasted_iota(jnp.int32, (c, c), 0)
    jj = lax.broadcasted_iota(jnp.int32, (c, c), 1)
    eye = jnp.where(ii == jj, 1.0, 0.0)
    p = [-a for a in a_list]
    t = [eye + x for x in p]
    pb = [x.astype(BF16) for x in p]
    p = [_dot(x, x) for x in pb]
    span = 4
    while span < c:
        pb = [x.astype(BF16) for x in p]
        st = [_dot(jnp.concatenate([x, y.astype(BF16)], axis=0), x) for x, y in zip(pb, t)]
        p = [x[:c] for x in st]
        t = [y + x[c:] for x, y in zip(st, t)]
        span *= 2
    pb = [x.astype(BF16) for x in p]
    return [y + _dot(y.astype(BF16), x) for x, y in zip(pb, t)]


def _gdn_kernel(q_ref, qp_ref, k_ref, kp_ref, v_ref, vp_ref, z_ref, beta_ref, gcum_ref, gct_ref,
                wq_ref, wk_ref, wv_ref, ng_ref, o_ref,
                qbuf, kbuf, vbuf, state_ref, *, tb, chunk, kheads):
    grp = pl.program_id(0)
    first = pl.program_id(1) == 0
    c = chunk
    dd = GDN_DIM
    halo = 8
    nh = 2 * kheads
    nc = tb // c

    @pl.when(first)
    def _():
        state_ref[...] = jnp.zeros_like(state_ref)

    def conv_silu(x_ref, xp_ref, w_ref, buf):
        prev = xp_ref[...].astype(F32)
        buf[0:halo, :] = jnp.where(first, 0.0, prev)
        buf[halo:halo + tb, :] = x_ref[...].astype(F32)
        y = buf[halo:halo + tb, :] * w_ref[GDN_CONV - 1:GDN_CONV, :]
        for j in range(GDN_CONV - 1):
            off = halo - (GDN_CONV - 1) + j
            y = y + buf[off:off + tb, :] * w_ref[j:j + 1, :]
        return _silu(y)

    def l2n(x):
        return x * lax.rsqrt(jnp.sum(x * x, axis=-1, keepdims=True) + EPS)

    qc = conv_silu(q_ref, qp_ref, wq_ref, qbuf)
    kc = conv_silu(k_ref, kp_ref, wk_ref, kbuf)
    vv = conv_silu(v_ref, vp_ref, wv_ref, vbuf)
    qn = [l2n(qc[:, h * dd:(h + 1) * dd]) * (dd ** -0.5) for h in range(kheads)]
    kn = [l2n(kc[:, h * dd:(h + 1) * dd]) for h in range(kheads)]

    lane_h = lax.broadcasted_iota(jnp.int32, (1, GDN_V_HEADS), 1)
    ii = lax.broadcasted_iota(jnp.int32, (c, c), 0)
    jj = lax.broadcasted_iota(jnp.int32, (c, c), 1)
    causal = ii >= jj
    strict = ii > jj
    ng = ng_ref[...]

    beta_t = beta_ref[...]
    gcum_t = gcum_ref[...]
    beta_col, gc_col, gc_row = [], [], []
    for h in range(nh):
        hv = grp * nh + h
        sel = lane_h == hv
        beta_col.append(jnp.sum(jnp.where(sel, beta_t, 0.0), axis=-1, keepdims=True))
        gc_col.append(jnp.sum(jnp.where(sel, gcum_t, 0.0), axis=-1, keepdims=True))
        gc_row.append(gct_ref[pl.ds(hv, 1), :])

    items = [(h, ci) for ci in range(nc) for h in range(nh)]
    pos = {it: n for n, it in enumerate(items)}

    def rows_of(ci):
        return slice(ci * c, (ci + 1) * c)

    qkk = {}
    for ci in range(nc):
        for kh in range(kheads):
            k16 = kn[kh][rows_of(ci)].astype(BF16)
            qk16 = jnp.concatenate([qn[kh][rows_of(ci)].astype(BF16), k16], axis=0)
            qkk[(kh, ci)] = _dot_nt(qk16, k16)

    decay, a_list = [], []
    for (h, ci) in items:
        r = rows_of(ci)
        dlog = gc_col[h][r] - gc_row[h][:, r]
        dec = jnp.where(causal, jnp.exp(jnp.where(causal, dlog, 0.0)), 0.0)
        decay.append(dec)
        a_list.append(jnp.where(strict, qkk[(h // 2, ci)][c:] * beta_col[h][r] * dec, 0.0))
    t_list = _unit_lower_inverses(a_list)

    uw = []
    for (h, ci), t_mat in zip(items, t_list):
        r = rows_of(ci)
        beta = beta_col[h][r]
        rhs = jnp.concatenate([vv[r, h * dd:(h + 1) * dd] * beta,
                               kn[h // 2][r] * (beta * jnp.exp(gc_col[h][r]))], axis=1)
        uw.append(_dot(t_mat.astype(BF16), rhs.astype(BF16)))

    for ci in range(nc):
        r = rows_of(ci)
        states = [state_ref[h] for h in range(nh)]
        s16 = [s.astype(BF16) for s in states]
        wq = []
        for h in range(nh):
            lhs = jnp.concatenate([uw[pos[(h, ci)]][:, dd:],
                                   qn[h // 2][r] * jnp.exp(gc_col[h][r])], axis=0)
            wq.append(_dot(lhs.astype(BF16), s16[h]))
        vn16 = [(uw[pos[(h, ci)]][:, :dd] - wq[h][:c]).astype(BF16) for h in range(nh)]
        outs = []
        for h in range(nh):
            intra = jnp.where(causal, qkk[(h // 2, ci)][:c] * decay[pos[(h, ci)]], 0.0)
            outs.append(wq[h][c:] + _dot(intra.astype(BF16), vn16[h]))
        for h in range(nh):
            gc = gc_col[h][r]
            g_last = gc[c - 1:c, :]
            kd = kn[h // 2][r] * jnp.exp(g_last - gc)
            state_ref[h] = states[h] * jnp.exp(g_last) + _dot_tn(kd.astype(BF16), vn16[h])
        for h in range(nh):
            zg = _silu(z_ref[r, h * dd:(h + 1) * dd].astype(F32))
            o_ref[r, h * dd:(h + 1) * dd] = (_rms_rows(outs[h]) * ng * zg).astype(o_ref.dtype)


def gdn_mixer(proj, beta, gcum, gct, conv_w, norm_gain, tb, kheads):
    s = proj.shape[0]
    dd = GDN_DIM
    wk = kheads * dd
    wv = 2 * wk
    k0 = GDN_QK_W // wk
    v0 = 2 * GDN_QK_W // wv
    z0 = GDN_QKV // wv
    per = tb // 8

    def prev_rows(i):
        return jnp.maximum(i * per - 1, 0)

    return pl.pallas_call(
        functools.partial(_gdn_kernel, tb=tb, chunk=GDN_CHUNK, kheads=kheads),
        grid=(GDN_K_HEADS // kheads, s // tb),
        in_specs=[pl.BlockSpec((tb, wk), lambda g, i: (i, g)),
                  pl.BlockSpec((8, wk), lambda g, i: (prev_rows(i), g)),
                  pl.BlockSpec((tb, wk), lambda g, i: (i, k0 + g)),
                  pl.BlockSpec((8, wk), lambda g, i: (prev_rows(i), k0 + g)),
                  pl.BlockSpec((tb, wv), lambda g, i: (i, v0 + g)),
                  pl.BlockSpec((8, wv), lambda g, i: (prev_rows(i), v0 + g)),
                  pl.BlockSpec((tb, wv), lambda g, i: (i, z0 + g)),
                  pl.BlockSpec((tb, GDN_V_HEADS), lambda g, i: (i, 0)),
                  pl.BlockSpec((tb, GDN_V_HEADS), lambda g, i: (i, 0)),
                  pl.BlockSpec((GDN_V_HEADS, tb), lambda g, i: (0, i)),
                  pl.BlockSpec((GDN_CONV, wk), lambda g, i: (0, g)),
                  pl.BlockSpec((GDN_CONV, wk), lambda g, i: (0, k0 + g)),
                  pl.BlockSpec((GDN_CONV, wv), lambda g, i: (0, v0 + g)),
                  pl.BlockSpec((1, dd), lambda g, i: (0, 0))],
        out_specs=pl.BlockSpec((tb, wv), lambda g, i: (i, g)),
        out_shape=jax.ShapeDtypeStruct((s, GDN_V_W), BF16),
        scratch_shapes=[pltpu.VMEM((8 + tb, wk), F32),
                        pltpu.VMEM((8 + tb, wk), F32),
                        pltpu.VMEM((8 + tb, wv), F32),
                        pltpu.VMEM((2 * kheads, dd, dd), F32)],
        compiler_params=_params("parallel", "arbitrary"),
        name="gated_deltanet",
    )(proj, proj, proj, proj, proj, proj, proj, beta, gcum, gct, conv_w, conv_w, conv_w,
      norm_gain.reshape(1, dd))


def _xa_kv_kernel(mem_ref, mg_ref, w_ref, kg_ref, k_ref, v_ref):
    mem_n = (_rms_rows(mem_ref[...]) * mg_ref[...]).astype(BF16)
    kv = _dot(mem_n, w_ref[...])
    for h in range(XA_HEADS):
        k = kv[:, h * XA_DIM:(h + 1) * XA_DIM]
        k_ref[h] = (_rms_rows(k) * kg_ref[...]).astype(k_ref.dtype)
        v_ref[h] = kv[:, (XA_HEADS + h) * XA_DIM:(XA_HEADS + h + 1) * XA_DIM].astype(v_ref.dtype)


def xa_keys_values(mem, mem_gain, w_kv, k_gain):
    m, d = mem.shape
    shape = jax.ShapeDtypeStruct((XA_HEADS, m, XA_DIM), BF16)
    return pl.pallas_call(
        _xa_kv_kernel,
        out_shape=[shape, shape],
        compiler_params=pltpu.CompilerParams(vmem_limit_bytes=VMEM_LIMIT_BYTES),
        name="xa_keys_values",
    )(mem, mem_gain.reshape(1, d), w_kv, k_gain.reshape(1, XA_DIM))


def _xattn_kernel(x_ref, g_ref, wq_ref, qg_ref, k_ref, v_ref, wo_ref, o_ref):
    x = x_ref[...]
    h = (_rms_rows(x) * g_ref[...]).astype(BF16)
    q = _dot(h, wq_ref[...])
    qg = qg_ref[...] * (XA_DIM ** -0.5)
    outs = []
    for hd in range(XA_HEADS):
        qh = (_rms_rows(q[:, hd * XA_DIM:(hd + 1) * XA_DIM]) * qg).astype(BF16)
        s = _dot_nt(qh, k_ref[hd])
        p = jnp.exp(s - jnp.max(s, axis=-1, keepdims=True))
        p = p / jnp.sum(p, axis=-1, keepdims=True)
        outs.append(_dot(p.astype(BF16), v_ref[hd]).astype(BF16))
    o = jnp.concatenate(outs, axis=-1)
    o_ref[...] = x + _dot(o, wo_ref[...])


def cross_attention(x, gain, w_q, q_gain, k, v, w_o, tm):
    s, d = x.shape
    hw = XA_HEADS * XA_DIM
    m = k.shape[1]
    return pl.pallas_call(
        _xattn_kernel,
        grid=(s // tm,),
        in_specs=[pl.BlockSpec((tm, d), lambda i: (i, 0)),
                  pl.BlockSpec((1, d), lambda i: (0, 0)),
                  pl.BlockSpec((d, hw), lambda i: (0, 0)),
                  pl.BlockSpec((1, XA_DIM), lambda i: (0, 0)),
                  pl.BlockSpec((XA_HEADS, m, XA_DIM), lambda i: (0, 0, 0)),
                  pl.BlockSpec((XA_HEADS, m, XA_DIM), lambda i: (0, 0, 0)),
                  pl.BlockSpec((hw, d), lambda i: (0, 0))],
        out_specs=pl.BlockSpec((tm, d), lambda i: (i, 0)),
        out_shape=jax.ShapeDtypeStruct((s, d), F32),
        compiler_params=_params("parallel"),
        name="cross_attention",
    )(x, gain.reshape(1, d), w_q, q_gain.reshape(1, XA_DIM), k, v, w_o)


def _pick_tile(total, target):
    t = min(total, target)
    while total % t:
        t //= 2
    return t


def _rotary_tables(s):
    inv = ROPE_BASE ** (-jnp.arange(0, RET_QK_DIM, 2, dtype=F32) / RET_QK_DIM)
    ang = jnp.arange(s).astype(F32)[:, None] * inv[None, :]
    return jnp.cos(ang), jnp.sin(ang)


def _tail(x, layer, mem, mem_norm, norm_xa, norm_ffn, xa_w_q, xa_w_kv, xa_w_o, xa_q_gain, xa_k_gain,
          ffn_w1, ffn_w3, ffn_w2, tm):
    k, v = xa_keys_values(mem, mem_norm, xa_w_kv[layer].astype(BF16), xa_k_gain[layer])
    x = cross_attention(x, norm_xa[layer], xa_w_q[layer].astype(BF16), xa_q_gain[layer], k, v,
                        xa_w_o[layer].astype(BF16), _pick_tile(x.shape[0], 512))
    hid = norm_swiglu(x, norm_ffn[layer], ffn_w1[layer].astype(BF16), ffn_w3[layer].astype(BF16),
                      tm, _pick_tile(ffn_w1.shape[-1], 512))
    return matmul_residual(hid, ffn_w2[layer].astype(BF16), x, tm, _pick_tile(x.shape[1], 512))


def kernel(x, mem, norm_mix, norm_xa, norm_ffn, mem_norm, rel_bias, ar_w_in, ar_w_out, dil_q_gain, dil_k_gain, gdn_w_in, gdn_conv, gdn_a_log, gdn_dt_bias, gdn_norm, gdn_w_out, xa_w_q, xa_w_kv, xa_w_o, xa_q_gain, xa_k_gain, ffn_w1, ffn_w3, ffn_w2):
    b, s, d = x.shape
    assert b == 1
    xs = x.reshape(s, d)
    mem2 = mem.reshape(mem.shape[1], d)
    tm = _pick_tile(s, 1024)
    tail = functools.partial(
        _tail, mem=mem2, mem_norm=mem_norm, norm_xa=norm_xa, norm_ffn=norm_ffn, xa_w_q=xa_w_q,
        xa_w_kv=xa_w_kv, xa_w_o=xa_w_o, xa_q_gain=xa_q_gain, xa_k_gain=xa_k_gain,
        ffn_w1=ffn_w1, ffn_w3=ffn_w3, ffn_w2=ffn_w2, tm=tm)

    proj = norm_matmul(xs, norm_mix[0], ar_w_in[0].astype(BF16), BF16, tm, 1024)
    cos, sin = _rotary_tables(s)
    ya = retention_mixer(proj, cos, sin, _pick_tile(s, 1024))
    qkv1, qkv4, qkv16 = dilated_prep(proj, dil_q_gain[0], dil_k_gain[0], _pick_tile(s, 2048))
    yb = dilated_attention(qkv1, qkv4, qkv16, rel_bias, _pick_tile(s, 2048), 8)
    xs = matmul2_residual(ya, yb, ar_w_out[0].astype(BF16), xs, tm, _pick_tile(d, 1024))
    xs = tail(xs, 0)

    w_in = gdn_w_in[0].astype(BF16)
    n_main = GDN_QKV + GDN_V_W
    n_gate = 2 * GDN_V_HEADS
    proj = norm_matmul(xs, norm_mix[1], w_in, BF16, tm, 1024, n_out=n_main)
    ab = norm_matmul(xs, norm_mix[1], w_in[:, n_main:], F32, tm, n_gate)
    beta, gcum, gct = gdn_gates(ab, gdn_a_log[0], gdn_dt_bias[0], _pick_tile(s, 256))
    og = gdn_mixer(proj, beta, gcum, gct, gdn_conv[0], gdn_norm[0], 2 * GDN_CHUNK, 4)
    xs = matmul_residual(og, gdn_w_out[0].astype(BF16), xs, tm, _pick_tile(d, 512))
    xs = tail(xs, 1)
    return xs.reshape(b, s, d)
```

```python
import functools
import math

import jax
import jax.numpy as jnp
from jax import lax
from jax.experimental import pallas as pl
from jax.experimental.pallas import tpu as pltpu

F32 = jnp.float32
BF16 = jnp.bfloat16
EPS = 1e-6
NEG = -1e30

VMEM_LIMIT_BYTES = 56 * 1024 * 1024
LANES = 128

RET_HEADS = 4
RET_QK_DIM = 256
RET_V_DIM = 512
RET_CHUNK = 128
ROPE_BASE = 10000.0
DIL_HEADS = 8
DIL_DIM = 128
DIL_PATTERNS = ((128, 1), (512, 4), (2048, 16))
DIL_STEPS = 128
REL_BUCKETS = 32
REL_MAX_DIST = 2048
AR_QK = RET_HEADS * RET_QK_DIM
AR_V = RET_HEADS * RET_V_DIM
DIL_W = DIL_HEADS * DIL_DIM
GDN_K_HEADS = 16
GDN_V_HEADS = 32
GDN_DIM = 128
GDN_CONV = 4
GDN_CHUNK = 128
GDN_QK_W = GDN_K_HEADS * GDN_DIM
GDN_V_W = GDN_V_HEADS * GDN_DIM
GDN_QKV = 2 * GDN_QK_W + GDN_V_W
XA_HEADS = 4
XA_DIM = 128


def _params(*semantics):
    return pltpu.CompilerParams(dimension_semantics=semantics, vmem_limit_bytes=VMEM_LIMIT_BYTES)


def _dot(a, b):
    return jnp.dot(a, b, preferred_element_type=F32)


def _dot_nt(a, b):
    return lax.dot_general(a, b, (((1,), (1,)), ((), ())), preferred_element_type=F32)


def _dot_tn(a, b):
    return lax.dot_general(a, b, (((0,), (0,)), ((), ())), preferred_element_type=F32)


def _dot_f32(a, b):
    return jnp.dot(a, b, preferred_element_type=F32, precision=lax.Precision.HIGHEST)


def _dot_nt_f32(a, b):
    return lax.dot_general(a, b, (((1,), (1,)), ((), ())), preferred_element_type=F32,
                           precision=lax.Precision.HIGHEST)


def _rms_rows(x):
    return x * lax.rsqrt(jnp.mean(x * x, axis=-1, keepdims=True) + EPS)


def _silu(x):
    return x * jax.nn.sigmoid(x)


def _norm_matmul_kernel(x_ref, g_ref, w_ref, o_ref, h_ref):
    @pl.when(pl.program_id(1) == 0)
    def _():
        h_ref[...] = (_rms_rows(x_ref[...]) * g_ref[...]).astype(BF16)

    o_ref[...] = _dot(h_ref[...], w_ref[...]).astype(o_ref.dtype)


def _norm_matmul_side_kernel(x_ref, g_ref, w_ref, ws_ref, o_ref, os_ref, h_ref):
    @pl.when(pl.program_id(1) == 0)
    def _():
        h = (_rms_rows(x_ref[...]) * g_ref[...]).astype(BF16)
        h_ref[...] = h
        os_ref[...] = _dot(h, ws_ref[...])

    o_ref[...] = _dot(h_ref[...], w_ref[...]).astype(o_ref.dtype)


def norm_matmul(x, gain, w, layer, out_dtype, tm, tn, n_out=None, w_side=None):
    m, k = x.shape
    n = w.shape[2] if n_out is None else n_out
    in_specs = [pl.BlockSpec((tm, k), lambda i, j: (i, 0)),
                pl.BlockSpec((1, k), lambda i, j: (0, 0)),
                pl.BlockSpec((None, k, tn), lambda i, j: (layer, 0, j))]
    out_specs = pl.BlockSpec((tm, tn), lambda i, j: (i, j))
    out_shape = jax.ShapeDtypeStruct((m, n), out_dtype)
    args = [x, gain.reshape(1, k), w]
    kern = _norm_matmul_kernel
    if w_side is not None:
        ns = w_side.shape[1]
        in_specs.append(pl.BlockSpec((k, ns), lambda i, j: (0, 0)))
        out_specs = [out_specs, pl.BlockSpec((tm, ns), lambda i, j: (i, 0))]
        out_shape = [out_shape, jax.ShapeDtypeStruct((m, ns), F32)]
        args.append(w_side)
        kern = _norm_matmul_side_kernel
    return pl.pallas_call(
        kern,
        grid=(m // tm, n // tn),
        in_specs=in_specs,
        out_specs=out_specs,
        out_shape=out_shape,
        scratch_shapes=[pltpu.VMEM((tm, k), BF16)],
        compiler_params=_params("parallel", "arbitrary"),
        name="norm_matmul",
    )(*args)


def _norm_swiglu_kernel(x_ref, g_ref, w1_ref, w3_ref, o_ref, h_ref):
    @pl.when(pl.program_id(1) == 0)
    def _():
        h_ref[...] = (_rms_rows(x_ref[...]) * g_ref[...]).astype(BF16)

    h = h_ref[...]
    a = _dot(h, w1_ref[...])
    b = _dot(h, w3_ref[...])
    o_ref[...] = (_silu(a) * b).astype(o_ref.dtype)


def norm_swiglu(x, gain, w1, w3, layer, tm, tn):
    m, k = x.shape
    n = w1.shape[2]
    return pl.pallas_call(
        _norm_swiglu_kernel,
        grid=(m // tm, n // tn),
        in_specs=[pl.BlockSpec((tm, k), lambda i, j: (i, 0)),
                  pl.BlockSpec((1, k), lambda i, j: (0, 0)),
                  pl.BlockSpec((None, k, tn), lambda i, j: (layer, 0, j)),
                  pl.BlockSpec((None, k, tn), lambda i, j: (layer, 0, j))],
        out_specs=pl.BlockSpec((tm, tn), lambda i, j: (i, j)),
        out_shape=jax.ShapeDtypeStruct((m, n), BF16),
        scratch_shapes=[pltpu.VMEM((tm, k), BF16)],
        compiler_params=_params("parallel", "arbitrary"),
        name="norm_swiglu",
    )(x, gain.reshape(1, k), w1, w3)


def _matmul_residual_kernel(y_ref, w_ref, x_ref, o_ref):
    o_ref[...] = x_ref[...] + _dot(y_ref[...], w_ref[...])


def matmul_residual(y, w, layer, x, tm, tn):
    m, k = y.shape
    n = w.shape[2]
    return pl.pallas_call(
        _matmul_residual_kernel,
        grid=(m // tm, n // tn),
        in_specs=[pl.BlockSpec((tm, k), lambda i, j: (i, 0)),
                  pl.BlockSpec((None, k, tn), lambda i, j: (layer, 0, j)),
                  pl.BlockSpec((tm, tn), lambda i, j: (i, j))],
        out_specs=pl.BlockSpec((tm, tn), lambda i, j: (i, j)),
        out_shape=jax.ShapeDtypeStruct((m, n), F32),
        compiler_params=_params("parallel", "parallel"),
        name="matmul_residual",
    )(y, w, x)


def _matmul2_residual_kernel(ya_ref, yb_ref, wa_ref, wb_ref, x_ref, o_ref):
    o_ref[...] = x_ref[...] + _dot(ya_ref[...], wa_ref[...]) + _dot(yb_ref[...], wb_ref[...])


def matmul2_residual(ya, yb, w, layer, x, tm, tn):
    m, ka = ya.shape
    kb = yb.shape[1]
    n = w.shape[2]
    assert ka % kb == 0
    return pl.pallas_call(
        _matmul2_residual_kernel,
        grid=(m // tm, n // tn),
        in_specs=[pl.BlockSpec((tm, ka), lambda i, j: (i, 0)),
                  pl.BlockSpec((tm, kb), lambda i, j: (i, 0)),
                  pl.BlockSpec((None, ka, tn), lambda i, j: (layer, 0, j)),
                  pl.BlockSpec((None, kb, tn), lambda i, j: (layer, ka // kb, j)),
                  pl.BlockSpec((tm, tn), lambda i, j: (i, j))],
        out_specs=pl.BlockSpec((tm, tn), lambda i, j: (i, j)),
        out_shape=jax.ShapeDtypeStruct((m, n), F32),
        compiler_params=_params("parallel", "parallel"),
        name="matmul2_residual",
    )(ya, yb, w, w, x)


def _retention_kernel(lg_ref, q_ref, k_ref, v_ref, g_ref, cos_ref, sin_ref, o_ref, state_ref,
                      *, chunk, n_chunks):
    @pl.when(pl.program_id(1) == 0)
    def _():
        state_ref[...] = jnp.zeros_like(state_ref)

    c = chunk
    half = RET_QK_DIM // 2
    lg = lg_ref[0]
    lg_l = lg[:, :LANES]
    ii = lax.broadcasted_iota(jnp.int32, (c, c), 0)
    jj = lax.broadcasted_iota(jnp.int32, (c, c), 1)
    rel = (ii - jj).astype(F32)
    inner = jnp.where(rel >= 0, jnp.exp(lg[:, :c] * jnp.maximum(rel, 0.0)), 0.0)
    idx = lax.broadcasted_iota(jnp.int32, (c, LANES), 0).astype(F32)
    q_dec = jnp.exp(lg_l * (idx + 1.0))
    k_dec = jnp.exp(lg_l * (c - 1.0 - idx))
    c_dec = jnp.exp(lg * float(c))
    k_scale = RET_QK_DIM ** -0.5

    def rot(t, cos, sin):
        t1, t2 = t[:, :half], t[:, half:]
        return t1 * cos - t2 * sin, t1 * sin + t2 * cos

    rows = [pl.ds(ci * c, c) for ci in range(n_chunks)]
    qb, kb, qd, kd = [], [], [], []
    for r in rows:
        cos = cos_ref[r, :]
        sin = sin_ref[r, :]
        q1, q2 = rot(q_ref[r, :].astype(F32), cos, sin)
        k1, k2 = rot(k_ref[r, :].astype(F32) * k_scale, cos, sin)
        qb.append(jnp.concatenate([q1, q2], axis=-1).astype(BF16))
        kb.append(jnp.concatenate([k1, k2], axis=-1).astype(BF16))
        qd.append(jnp.concatenate([q1 * q_dec, q2 * q_dec], axis=-1).astype(BF16))
        kd.append(jnp.concatenate([k1 * k_dec, k2 * k_dec], axis=-1).astype(BF16))
    scores = [(_dot_nt(a, b) * inner).astype(BF16) for a, b in zip(qb, kb)]
    kv = [_dot_tn(a, v_ref[r, :]) for a, r in zip(kd, rows)]
    states = [state_ref[...]]
    for inc in kv:
        states.append(states[-1] * c_dec + inc)
    state_ref[...] = states[-1]
    outs = [_dot(sc, v_ref[r, :]) + _dot(a, st.astype(BF16))
            for sc, r, a, st in zip(scores, rows, qd, states)]
    for o, r in zip(outs, rows):
        gate = _silu(g_ref[r, :].astype(F32))
        o_ref[r, :] = (_rms_rows(o) * gate).astype(o_ref.dtype)


def retention_mixer(proj, cos, sin, rows_per_step):
    s = proj.shape[0]
    t = rows_per_step
    log_gamma = jnp.log(1.0 - 2.0 ** (-5.0 - jnp.arange(RET_HEADS, dtype=F32)))
    lg = jnp.broadcast_to(log_gamma[:, None, None], (RET_HEADS, 1, RET_V_DIM))
    qk_blocks = AR_QK // RET_QK_DIM
    v_off = 2 * AR_QK // RET_V_DIM
    g_off = v_off + AR_V // RET_V_DIM
    kern = functools.partial(_retention_kernel, chunk=RET_CHUNK, n_chunks=t // RET_CHUNK)
    return pl.pallas_call(
        kern,
        grid=(RET_HEADS, s // t),
        in_specs=[pl.BlockSpec((1, 1, RET_V_DIM), lambda h, i: (h, 0, 0)),
                  pl.BlockSpec((t, RET_QK_DIM), lambda h, i: (i, h)),
                  pl.BlockSpec((t, RET_QK_DIM), lambda h, i: (i, qk_blocks + h)),
                  pl.BlockSpec((t, RET_V_DIM), lambda h, i: (i, v_off + h)),
                  pl.BlockSpec((t, RET_V_DIM), lambda h, i: (i, g_off + h)),
                  pl.BlockSpec((t, RET_QK_DIM // 2), lambda h, i: (i, 0)),
                  pl.BlockSpec((t, RET_QK_DIM // 2), lambda h, i: (i, 0))],
        out_specs=pl.BlockSpec((t, RET_V_DIM), lambda h, i: (i, h)),
        out_shape=jax.ShapeDtypeStruct((s, AR_V), BF16),
        scratch_shapes=[pltpu.VMEM((RET_QK_DIM, RET_V_DIM), F32)],
        compiler_params=_params("parallel", "arbitrary"),
        name="retention",
    )(lg, proj, proj, proj, proj, cos, sin)


def _dil_prep_kernel(p_ref, g_ref, o1_ref, o4_ref, o16_ref, buf):
    c = pl.program_id(0)
    tm = p_ref.shape[0]
    dd = DIL_DIM
    x = p_ref[...].astype(F32)

    @pl.when(c < 2 * DIL_HEADS)
    def _():
        buf[...] = _rms_rows(x) * g_ref[0]

    @pl.when(c >= 2 * DIL_HEADS)
    def _():
        buf[...] = x

    o1_ref[...] = buf[...].astype(o1_ref.dtype)
    for r in range(4):
        o4_ref[:, r * dd:(r + 1) * dd] = buf[pl.ds(r, tm // 4, stride=4), :].astype(o4_ref.dtype)
    for r in range(16):
        o16_ref[:, r * dd:(r + 1) * dd] = buf[pl.ds(r, tm // 16, stride=16), :].astype(o16_ref.dtype)


def dilated_prep(proj, q_gain, k_gain, tm):
    s = proj.shape[0]
    dd = DIL_DIM
    col0 = (2 * AR_QK + 2 * AR_V) // dd
    nh = 3 * DIL_HEADS
    gains = jnp.concatenate([
        jnp.broadcast_to(q_gain[None, :] * (dd ** -0.5), (DIL_HEADS, dd)),
        jnp.broadcast_to(k_gain[None, :], (DIL_HEADS, dd)),
        jnp.ones((DIL_HEADS, dd), F32)], axis=0).reshape(nh, 1, dd)
    return pl.pallas_call(
        _dil_prep_kernel,
        grid=(nh, s // tm),
        in_specs=[pl.BlockSpec((tm, dd), lambda c, i: (i, col0 + c)),
                  pl.BlockSpec((1, 1, dd), lambda c, i: (c, 0, 0))],
        out_specs=[pl.BlockSpec((None, tm // d, d * dd), lambda c, i: (c, i, 0)) for d in (1, 4, 16)],
        out_shape=[jax.ShapeDtypeStruct((nh, s // d, d * dd), BF16) for d in (1, 4, 16)],
        scratch_shapes=[pltpu.VMEM((tm, dd), F32)],
        compiler_params=_params("parallel", "parallel"),
        name="dilated_prep",
    )(proj, gains)


def _dilated_kernel(bias_ref,
                    q1, k1, k1p, v1, v1p,
                    q4, k4, k4p, v4, v4p,
                    q16, k16, k16p, v16, v16p,
                    o_ref, acc_ref, m_ref, l_ref, *, tile, group):
    n = DIL_STEPS
    d_dim = DIL_DIM
    first = pl.program_id(1) == 0

    def run_group(bias_idx, blocks, mode):
        def ld(p):
            return p[0][p[1]]

        bias_p = bias_ref[bias_idx, :, :n]
        bias_c = bias_ref[bias_idx, :, n:]
        sp = [_dot_nt(ld(b[0]), ld(b[1])) + bias_p for b in blocks]
        sc = [_dot_nt(ld(b[0]), ld(b[2])) + bias_c for b in blocks]
        sp = [jnp.where(first, NEG, x) if b[5] else x for x, b in zip(sp, blocks)]
        m = [jnp.maximum(jnp.max(x, axis=-1, keepdims=True), jnp.max(y, axis=-1, keepdims=True))
             for x, y in zip(sp, sc)]
        pp = [jnp.exp(x - mm) for x, mm in zip(sp, m)]
        pc = [jnp.exp(x - mm) for x, mm in zip(sc, m)]
        l = [jnp.sum(x, axis=-1, keepdims=True) + jnp.sum(y, axis=-1, keepdims=True)
             for x, y in zip(pp, pc)]
        o = [_dot(x.astype(BF16), ld(b[3])) + _dot(y.astype(BF16), ld(b[4]))
             for x, y, b in zip(pp, pc, blocks)]
        for b, mm, ll, oo in zip(blocks, m, l, o):
            rows = b[6]
            if mode == "init":
                m_ref[rows, :] = jnp.broadcast_to(mm, (n, d_dim))
                l_ref[rows, :] = jnp.broadcast_to(ll, (n, d_dim))
                acc_ref[rows, :] = oo
                continue
            m_old = m_ref[rows, :]
            m_new = jnp.maximum(m_old, mm)
            a_old = jnp.exp(m_old - m_new)
            a_cur = jnp.exp(mm - m_new)
            l_new = l_ref[rows, :] * a_old + ll * a_cur
            acc_new = acc_ref[rows, :] * a_old + oo * a_cur
            if mode == "merge":
                m_ref[rows, :] = m_new
                l_ref[rows, :] = l_new
                acc_ref[rows, :] = acc_new
            else:
                o_ref[rows, :] = (acc_new / l_new).astype(o_ref.dtype)

    every = slice(None)

    blocks = []
    for r in range(16):
        ls = slice(r * d_dim, (r + 1) * d_dim)
        blocks.append(((q16, (every, ls)), (k16p, (every, ls)), (k16, (every, ls)),
                       (v16p, (every, ls)), (v16, (every, ls)), True, pl.ds(r, n, stride=16)))
    for g0 in range(0, len(blocks), group):
        run_group(2, blocks[g0:g0 + group], "init")

    blocks = []
    for r in range(4):
        ls = slice(r * d_dim, (r + 1) * d_dim)
        for b in range(tile // (4 * n)):
            cur = pl.ds(b * n, n)
            prev = pl.ds((b - 1) * n, n)
            kp, vp = (((k4p, (every, ls)), (v4p, (every, ls))) if b == 0
                      else ((k4, (prev, ls)), (v4, (prev, ls))))
            blocks.append(((q4, (cur, ls)), kp, (k4, (cur, ls)), vp, (v4, (cur, ls)), b == 0,
                           pl.ds(b * 4 * n + r, n, stride=4)))
    for g0 in range(0, len(blocks), group):
        run_group(1, blocks[g0:g0 + group], "merge")

    blocks = []
    for b in range(tile // n):
        cur = pl.ds(b * n, n)
        prev = pl.ds((b - 1) * n, n)
        kp, vp = (((k1p, (every, every)), (v1p, (every, every))) if b == 0
                  else ((k1, (prev, every)), (v1, (prev, every))))
        blocks.append(((q1, (cur, every)), kp, (k1, (cur, every)), vp, (v1, (cur, every)), b == 0, cur))
    for g0 in range(0, len(blocks), group):
        run_group(0, blocks[g0:g0 + group], "final")


def _t5_bucket(dist):
    exact = REL_BUCKETS // 2
    large = exact + (jnp.log(jnp.maximum(dist, exact).astype(F32) / exact)
                     / math.log(REL_MAX_DIST / exact) * (REL_BUCKETS - exact)).astype(jnp.int32)
    large = jnp.minimum(large, REL_BUCKETS - 1)
    return jnp.where(dist < exact, dist, large)


def _dilated_bias(rel_bias):
    n = DIL_STEPS
    steps = jnp.arange(n + 1)
    vec = jnp.stack([rel_bias.astype(F32)[_t5_bucket(steps * dil)] for (_, dil) in DIL_PATTERNS])
    vec = vec.transpose(0, 2, 1)
    width = 3 * n
    p = jnp.full(vec.shape[:2] + (width,), NEG, F32)
    p = p.at[:, :, n - 1:2 * n].set(vec[:, :, ::-1])
    tiled = jnp.broadcast_to(p[:, :, None, :], vec.shape[:2] + (n, width))
    skew = tiled.reshape(vec.shape[:2] + (n * width,))[:, :, :n * (width - 1)]
    skew = skew.reshape(vec.shape[:2] + (n, width - 1))
    return skew[:, :, :, n - 1:3 * n - 1]


def dilated_attention(qkv1, qkv4, qkv16, rel_bias, tile, group):
    _, s, dd = qkv1.shape
    n = DIL_STEPS
    h_n = DIL_HEADS
    bias = _dilated_bias(rel_bias)
    args, specs = [bias], [pl.BlockSpec((len(DIL_PATTERNS), None, n, 2 * n), lambda h, i: (0, h, 0, 0))]
    for dil, view in ((1, qkv1), (4, qkv4), (16, qkv16)):
        rows = tile // dil
        per = rows // n

        def cur_spec(off, rows=rows, dil=dil):
            return pl.BlockSpec((None, rows, dil * dd), lambda h, i: (off + h, i, 0))

        def prev_spec(off, per=per, dil=dil):
            return pl.BlockSpec((None, n, dil * dd), lambda h, i: (off + h, jnp.maximum(i * per - 1, 0), 0))

        args += [view, view, view, view, view]
        specs += [cur_spec(0), cur_spec(h_n), prev_spec(h_n), cur_spec(2 * h_n), prev_spec(2 * h_n)]
    return pl.pallas_call(
        functools.partial(_dilated_kernel, tile=tile, group=group),
        grid=(h_n, s // tile),
        in_specs=specs,
        out_specs=pl.BlockSpec((tile, dd), lambda h, i: (i, h)),
        out_shape=jax.ShapeDtypeStruct((s, h_n * dd), BF16),
        scratch_shapes=[pltpu.VMEM((tile, dd), F32)] * 3,
        compiler_params=_params("parallel", "arbitrary"),
        name="dilated_attention",
    )(*args)


def _gdn_gates_kernel(ab_ref, alog_ref, dt_ref, beta_ref, gcum_ref, gct_ref, *, chunk):
    t = ab_ref.shape[0]
    hv = GDN_V_HEADS
    ab = ab_ref[...]
    beta_ref[...] = jax.nn.sigmoid(ab[:, :hv])
    z = ab[:, hv:] + dt_ref[...]
    softplus = jnp.maximum(z, 0.0) + jnp.log1p(jnp.exp(-jnp.abs(z)))
    g = -jnp.exp(alog_ref[...]) * softplus
    ii = lax.broadcasted_iota(jnp.int32, (t, t), 0)
    jj = lax.broadcasted_iota(jnp.int32, (t, t), 1)
    tri = ((ii >= jj) & (ii // chunk == jj // chunk)).astype(F32)
    gcum = _dot_f32(tri, g)
    gcum_ref[...] = gcum
    eh = lax.broadcasted_iota(jnp.int32, (hv, hv), 0) == lax.broadcasted_iota(jnp.int32, (hv, hv), 1)
    gct_ref[...] = _dot_nt_f32(eh.astype(F32), gcum)


def gdn_gates(ab, a_log, dt_bias, tm):
    s = ab.shape[0]
    hv = GDN_V_HEADS
    return pl.pallas_call(
        functools.partial(_gdn_gates_kernel, chunk=GDN_CHUNK),
        grid=(s // tm,),
        in_specs=[pl.BlockSpec((tm, 2 * hv), lambda i: (i, 0)),
                  pl.BlockSpec((1, hv), lambda i: (0, 0)),
                  pl.BlockSpec((1, hv), lambda i: (0, 0))],
        out_specs=[pl.BlockSpec((tm, hv), lambda i: (i, 0)),
                   pl.BlockSpec((tm, hv), lambda i: (i, 0)),
                   pl.BlockSpec((hv, tm), lambda i: (0, i))],
        out_shape=[jax.ShapeDtypeStruct((s, hv), F32), jax.ShapeDtypeStruct((s, hv), F32),
                   jax.ShapeDtypeStruct((hv, s), F32)],
        compiler_params=_params("parallel"),
        name="gdn_gates",
    )(ab, a_log.reshape(1, hv), dt_bias.reshape(1, hv))


def _unit_lower_inverses(a_list, tick):
    c = a_list[0].shape[0]
    ii = lax.broadcasted_iota(jnp.int32, (c, c), 0)
    jj = lax.broadcasted_iota(jnp.int32, (c, c), 1)
    eye = jnp.where(ii == jj, 1.0, 0.0)
    p = [-a for a in a_list]
    t = [eye + x for x in p]
    pb = [x.astype(BF16) for x in p]
    p = [_dot(x, x) for x in pb]
    tick()
    span = 4
    while span < c:
        pb = [x.astype(BF16) for x in p]
        st = [_dot(jnp.concatenate([x, y.astype(BF16)], axis=0), x) for x, y in zip(pb, t)]
        tick()
        p = [x[:c] for x in st]
        t = [y + x[c:] for x, y in zip(st, t)]
        span *= 2
    pb = [x.astype(BF16) for x in p]
    return [y + _dot(y.astype(BF16), x) for x, y in zip(pb, t)]


def _gdn_kernel(q_ref, qp_ref, k_ref, kp_ref, v_ref, vp_ref, z_ref, beta_ref, gcum_ref, gct_ref,
                wq_ref, wk_ref, wv_ref, ng_ref, o_ref,
                qbuf, kbuf, vbuf, qs, ks, vs, state_ref, *, tb, chunk, kheads, piece):
    grp = pl.program_id(0)
    first = pl.program_id(1) == 0
    c = chunk
    dd = GDN_DIM
    halo = 8
    nh = 2 * kheads
    nc = tb // c

    @pl.when(first)
    def _():
        state_ref[...] = jnp.zeros_like(state_ref)

    def rows_of(ci):
        return slice(ci * c, (ci + 1) * c)

    def conv_pieces(ci):
        out = []
        for x_ref, xp_ref, w_ref, buf, dst, scale in ((q_ref, qp_ref, wq_ref, qbuf, qs, dd ** -0.5),
                                                      (k_ref, kp_ref, wk_ref, kbuf, ks, 1.0),
                                                      (v_ref, vp_ref, wv_ref, vbuf, vs, None)):
            width = x_ref.shape[1]
            for c0 in range(0, width, piece):
                def run(x_ref=x_ref, xp_ref=xp_ref, w_ref=w_ref, buf=buf, dst=dst, scale=scale, c0=c0):
                    cols = slice(c0, c0 + piece)
                    base = halo + ci * c
                    if ci == 0:
                        buf[0:halo, cols] = jnp.where(first, 0.0, xp_ref[:, cols].astype(F32))
                    buf[base:base + c, cols] = x_ref[rows_of(ci), cols].astype(F32)
                    y = buf[base:base + c, cols] * w_ref[GDN_CONV - 1:GDN_CONV, cols]
                    for j in range(GDN_CONV - 1):
                        off = base - (GDN_CONV - 1) + j
                        y = y + buf[off:off + c, cols] * w_ref[j:j + 1, cols]
                    y = _silu(y)
                    if scale is None:
                        dst[rows_of(ci), cols] = y
                    else:
                        for h0 in range(0, piece, dd):
                            yh = y[:, h0:h0 + dd]
                            yh = yh * (lax.rsqrt(jnp.sum(yh * yh, axis=-1, keepdims=True) + EPS) * scale)
                            dst[rows_of(ci), c0 + h0:c0 + h0 + dd] = yh
                out.append(run)
        return out

    lane_h = lax.broadcasted_iota(jnp.int32, (1, GDN_V_HEADS), 1)
    ii = lax.broadcasted_iota(jnp.int32, (c, c), 0)
    jj = lax.broadcasted_iota(jnp.int32, (c, c), 1)
    causal = ii >= jj
    strict = ii > jj
    ng = ng_ref[...]

    for run in conv_pieces(0):
        run()

    for ci in range(nc):
        r = rows_of(ci)
        pending = conv_pieces(ci + 1) if ci + 1 < nc else []

        def tick(count=1):
            for _ in range(count):
                if pending:
                    pending.pop(0)()

        def head_cols(h):
            return slice(h * dd, (h + 1) * dd)

        beta_t = beta_ref[r, :]
        gcum_t = gcum_ref[r, :]
        beta_col, gc_col, gc_row = [], [], []
        for h in range(nh):
            hv = grp * nh + h
            sel = lane_h == hv
            beta_col.append(jnp.sum(jnp.where(sel, beta_t, 0.0), axis=-1, keepdims=True))
            gc_col.append(jnp.sum(jnp.where(sel, gcum_t, 0.0), axis=-1, keepdims=True))
            gc_row.append(gct_ref[pl.ds(hv, 1), :][:, r])

        qkk = []
        for kh in range(kheads):
            k16 = ks[r, head_cols(kh)].astype(BF16)
            qk16 = jnp.concatenate([qs[r, head_cols(kh)].astype(BF16), k16], axis=0)
            qkk.append(_dot_nt(qk16, k16))
        tick()

        decay, a_list = [], []
        for h in range(nh):
            dlog = gc_col[h] - gc_row[h]
            dec = jnp.where(causal, jnp.exp(jnp.where(causal, dlog, 0.0)), 0.0)
            decay.append(dec)
            a_list.append(jnp.where(strict, qkk[h // 2][c:] * beta_col[h] * dec, 0.0))
        tick()
        t_list = _unit_lower_inverses(a_list, tick)

        uw = []
        for h in range(nh):
            rhs = jnp.concatenate([vs[r, head_cols(h)] * beta_col[h],
                                   ks[r, head_cols(h // 2)] * (beta_col[h] * jnp.exp(gc_col[h]))], axis=1)
            uw.append(_dot(t_list[h].astype(BF16), rhs.astype(BF16)))
        tick()

        states = [state_ref[h] for h in range(nh)]
        s16 = [s.astype(BF16) for s in states]
        wq = []
        for h in range(nh):
            lhs = jnp.concatenate([uw[h][:, dd:], qs[r, head_cols(h // 2)] * jnp.exp(gc_col[h])], axis=0)
            wq.append(_dot(lhs.astype(BF16), s16[h]))
        tick()
        vn16 = [(uw[h][:, :dd] - wq[h][:c]).astype(BF16) for h in range(nh)]
        outs = []
        for h in range(nh):
            intra = jnp.where(causal, qkk[h // 2][:c] * decay[h], 0.0)
            outs.append(wq[h][c:] + _dot(intra.astype(BF16), vn16[h]))
        tick()
        for h in range(nh):
            g_last = gc_col[h][c - 1:c, :]
            kd = ks[r, head_cols(h // 2)] * jnp.exp(g_last - gc_col[h])
            state_ref[h] = states[h] * jnp.exp(g_last) + _dot_tn(kd.astype(BF16), vn16[h])
        tick()
        for h in range(nh):
            zg = _silu(z_ref[r, head_cols(h)].astype(F32))
            o_ref[r, head_cols(h)] = (_rms_rows(outs[h]) * ng * zg).astype(o_ref.dtype)
        tick(len(pending))


def gdn_mixer(proj, beta, gcum, gct, conv_w, norm_gain, tb, kheads, piece=256):
    s = proj.shape[0]
    dd = GDN_DIM
    wk = kheads * dd
    wv = 2 * wk
    k0 = GDN_QK_W // wk
    v0 = 2 * GDN_QK_W // wv
    z0 = GDN_QKV // wv
    per = tb // 8

    def prev_rows(i):
        return jnp.maximum(i * per - 1, 0)

    return pl.pallas_call(
        functools.partial(_gdn_kernel, tb=tb, chunk=GDN_CHUNK, kheads=kheads, piece=piece),
        grid=(GDN_K_HEADS // kheads, s // tb),
        in_specs=[pl.BlockSpec((tb, wk), lambda g, i: (i, g)),
                  pl.BlockSpec((8, wk), lambda g, i: (prev_rows(i), g)),
                  pl.BlockSpec((tb, wk), lambda g, i: (i, k0 + g)),
                  pl.BlockSpec((8, wk), lambda g, i: (prev_rows(i), k0 + g)),
                  pl.BlockSpec((tb, wv), lambda g, i: (i, v0 + g)),
                  pl.BlockSpec((8, wv), lambda g, i: (prev_rows(i), v0 + g)),
                  pl.BlockSpec((tb, wv), lambda g, i: (i, z0 + g)),
                  pl.BlockSpec((tb, GDN_V_HEADS), lambda g, i: (i, 0)),
                  pl.BlockSpec((tb, GDN_V_HEADS), lambda g, i: (i, 0)),
                  pl.BlockSpec((GDN_V_HEADS, tb), lambda g, i: (0, i)),
                  pl.BlockSpec((GDN_CONV, wk), lambda g, i: (0, g)),
                  pl.BlockSpec((GDN_CONV, wk), lambda g, i: (0, k0 + g)),
                  pl.BlockSpec((GDN_CONV, wv), lambda g, i: (0, v0 + g)),
                  pl.BlockSpec((1, dd), lambda g, i: (0, 0))],
        out_specs=pl.BlockSpec((tb, wv), lambda g, i: (i, g)),
        out_shape=jax.ShapeDtypeStruct((s, GDN_V_W), BF16),
        scratch_shapes=[pltpu.VMEM((8 + tb, wk), F32),
                        pltpu.VMEM((8 + tb, wk), F32),
                        pltpu.VMEM((8 + tb, wv), F32),
                        pltpu.VMEM((tb, wk), F32),
                        pltpu.VMEM((tb, wk), F32),
                        pltpu.VMEM((tb, wv), F32),
                        pltpu.VMEM((2 * kheads, dd, dd), F32)],
        compiler_params=_params("parallel", "arbitrary"),
        name="gated_deltanet",
    )(proj, proj, proj, proj, proj, proj, proj, beta, gcum, gct, conv_w, conv_w, conv_w,
      norm_gain.reshape(1, dd))


def _xa_kv_kernel(mem_ref, mg_ref, w_ref, kg_ref, k_ref, v_ref):
    mem_n = (_rms_rows(mem_ref[...]) * mg_ref[...]).astype(BF16)
    kv = _dot(mem_n, w_ref[...])
    for h in range(XA_HEADS):
        k = kv[:, h * XA_DIM:(h + 1) * XA_DIM]
        k_ref[h] = (_rms_rows(k) * kg_ref[...]).astype(k_ref.dtype)
        v_ref[h] = kv[:, (XA_HEADS + h) * XA_DIM:(XA_HEADS + h + 1) * XA_DIM].astype(v_ref.dtype)


def xa_keys_values(mem, mem_gain, w_kv, layer, k_gain):
    m, d = mem.shape
    shape = jax.ShapeDtypeStruct((XA_HEADS, m, XA_DIM), BF16)
    nkv = w_kv.shape[2]
    return pl.pallas_call(
        _xa_kv_kernel,
        grid=(1,),
        in_specs=[pl.BlockSpec((m, d), lambda i: (0, 0)),
                  pl.BlockSpec((1, d), lambda i: (0, 0)),
                  pl.BlockSpec((None, d, nkv), lambda i: (layer, 0, 0)),
                  pl.BlockSpec((1, XA_DIM), lambda i: (0, 0))],
        out_specs=[pl.BlockSpec((XA_HEADS, m, XA_DIM), lambda i: (0, 0, 0))] * 2,
        out_shape=[shape, shape],
        compiler_params=_params("arbitrary"),
        name="xa_keys_values",
    )(mem, mem_gain.reshape(1, d), w_kv, k_gain.reshape(1, XA_DIM))


def _xattn_kernel(x_ref, g_ref, wq_ref, qg_ref, k_ref, v_ref, wo_ref, o_ref):
    x = x_ref[...]
    h = (_rms_rows(x) * g_ref[...]).astype(BF16)
    q = _dot(h, wq_ref[...])
    qg = qg_ref[...] * (XA_DIM ** -0.5)
    heads = range(XA_HEADS)
    qh = [(_rms_rows(q[:, hd * XA_DIM:(hd + 1) * XA_DIM]) * qg).astype(BF16) for hd in heads]
    sc = [_dot_nt(qh[hd], k_ref[hd]) for hd in heads]
    pr = [jnp.exp(s - jnp.max(s, axis=-1, keepdims=True)) for s in sc]
    pr = [p / jnp.sum(p, axis=-1, keepdims=True) for p in pr]
    outs = [_dot(pr[hd].astype(BF16), v_ref[hd]).astype(BF16) for hd in heads]
    o = jnp.concatenate(outs, axis=-1)
    o_ref[...] = x + _dot(o, wo_ref[...])


def cross_attention(x, gain, w_q, q_gain, k, v, w_o, layer, tm):
    s, d = x.shape
    hw = XA_HEADS * XA_DIM
    m = k.shape[1]
    return pl.pallas_call(
        _xattn_kernel,
        grid=(s // tm,),
        in_specs=[pl.BlockSpec((tm, d), lambda i: (i, 0)),
                  pl.BlockSpec((1, d), lambda i: (0, 0)),
                  pl.BlockSpec((None, d, hw), lambda i: (layer, 0, 0)),
                  pl.BlockSpec((1, XA_DIM), lambda i: (0, 0)),
                  pl.BlockSpec((XA_HEADS, m, XA_DIM), lambda i: (0, 0, 0)),
                  pl.BlockSpec((XA_HEADS, m, XA_DIM), lambda i: (0, 0, 0)),
                  pl.BlockSpec((None, hw, d), lambda i: (layer, 0, 0))],
        out_specs=pl.BlockSpec((tm, d), lambda i: (i, 0)),
        out_shape=jax.ShapeDtypeStruct((s, d), F32),
        compiler_params=_params("parallel"),
        name="cross_attention",
    )(x, gain.reshape(1, d), w_q, q_gain.reshape(1, XA_DIM), k, v, w_o)


def _pick_tile(total, target):
    t = min(total, target)
    while total % t:
        t //= 2
    return t


def _rotary_tables(s):
    inv = ROPE_BASE ** (-jnp.arange(0, RET_QK_DIM, 2, dtype=F32) / RET_QK_DIM)
    ang = jnp.arange(s).astype(F32)[:, None] * inv[None, :]
    return jnp.cos(ang), jnp.sin(ang)


def kernel(x, mem, norm_mix, norm_xa, norm_ffn, mem_norm, rel_bias, ar_w_in, ar_w_out, dil_q_gain, dil_k_gain, gdn_w_in, gdn_conv, gdn_a_log, gdn_dt_bias, gdn_norm, gdn_w_out, xa_w_q, xa_w_kv, xa_w_o, xa_q_gain, xa_k_gain, ffn_w1, ffn_w3, ffn_w2):
    b, s, d = x.shape
    assert b == 1
    xs = x.reshape(s, d)
    mem2 = mem.reshape(mem.shape[1], d)
    tm = _pick_tile(s, 1024)
    ar_w_in, ar_w_out, gdn_w_out, xa_w_q, xa_w_kv, xa_w_o, ffn_w1, ffn_w3, ffn_w2 = (
        w.astype(BF16) for w in (ar_w_in, ar_w_out, gdn_w_out, xa_w_q, xa_w_kv, xa_w_o, ffn_w1, ffn_w3, ffn_w2))

    def tail(xs, layer):
        k, v = xa_keys_values(mem2, mem_norm, xa_w_kv, layer, xa_k_gain[layer])
        xs = cross_attention(xs, norm_xa[layer], xa_w_q, xa_q_gain[layer], k, v, xa_w_o, layer,
                             _pick_tile(s, 512))
        hid = norm_swiglu(xs, norm_ffn[layer], ffn_w1, ffn_w3, layer, tm, _pick_tile(ffn_w1.shape[-1], 512))
        return matmul_residual(hid, ffn_w2, layer, xs, tm, _pick_tile(d, 512))

    proj = norm_matmul(xs, norm_mix[0], ar_w_in, 0, BF16, tm, 1024)
    cos, sin = _rotary_tables(s)
    ya = retention_mixer(proj, cos, sin, _pick_tile(s, 1024))
    qkv1, qkv4, qkv16 = dilated_prep(proj, dil_q_gain[0], dil_k_gain[0], _pick_tile(s, 2048))
    yb = dilated_attention(qkv1, qkv4, qkv16, rel_bias, _pick_tile(s, 2048), 8)
    xs = matmul2_residual(ya, yb, ar_w_out, 0, xs, tm, _pick_tile(d, 1024))
    xs = tail(xs, 0)

    n_main = GDN_QKV + GDN_V_W
    w_gate = gdn_w_in[0, :, n_main:].astype(BF16)
    proj, ab = norm_matmul(xs, norm_mix[1], gdn_w_in.astype(BF16), 0, BF16, tm, 1024, n_out=n_main,
                           w_side=w_gate)
    beta, gcum, gct = gdn_gates(ab, gdn_a_log[0], gdn_dt_bias[0], _pick_tile(s, 256))
    og = gdn_mixer(proj, beta, gcum, gct, gdn_conv[0], gdn_norm[0], 4 * GDN_CHUNK, 4)
    xs = matmul_residual(og, gdn_w_out, 0, xs, tm, _pick_tile(d, 512))
    xs = tail(xs, 1)
    return xs.reshape(b, s, d)
```

```python
import functools
import math

import jax
import jax.numpy as jnp
from jax import lax
from jax.experimental import pallas as pl
from jax.experimental.pallas import tpu as pltpu

F32 = jnp.float32
BF16 = jnp.bfloat16
EPS = 1e-6
NEG = -1e30

VMEM_LIMIT_BYTES = 56 * 1024 * 1024
LANES = 128

RET_HEADS = 4
RET_QK_DIM = 256
RET_V_DIM = 512
RET_CHUNK = 128
ROPE_BASE = 10000.0
DIL_HEADS = 8
DIL_DIM = 128
DIL_PATTERNS = ((128, 1), (512, 4), (2048, 16))
DIL_STEPS = 128
REL_BUCKETS = 32
REL_MAX_DIST = 2048
AR_QK = RET_HEADS * RET_QK_DIM
AR_V = RET_HEADS * RET_V_DIM
DIL_W = DIL_HEADS * DIL_DIM
GDN_K_HEADS = 16
GDN_V_HEADS = 32
GDN_DIM = 128
GDN_CONV = 4
GDN_CHUNK = 128
GDN_QK_W = GDN_K_HEADS * GDN_DIM
GDN_V_W = GDN_V_HEADS * GDN_DIM
GDN_QKV = 2 * GDN_QK_W + GDN_V_W
XA_HEADS = 4
XA_DIM = 128


def _params(*semantics):
    return pltpu.CompilerParams(dimension_semantics=semantics, vmem_limit_bytes=VMEM_LIMIT_BYTES)


def _dot(a, b):
    return jnp.dot(a, b, preferred_element_type=F32)


def _dot_nt(a, b):
    return lax.dot_general(a, b, (((1,), (1,)), ((), ())), preferred_element_type=F32)


def _dot_tn(a, b):
    return lax.dot_general(a, b, (((0,), (0,)), ((), ())), preferred_element_type=F32)


def _dot_f32(a, b):
    return jnp.dot(a, b, preferred_element_type=F32, precision=lax.Precision.HIGHEST)


def _dot_nt_f32(a, b):
    return lax.dot_general(a, b, (((1,), (1,)), ((), ())), preferred_element_type=F32,
                           precision=lax.Precision.HIGHEST)


def _rms_rows(x):
    return x * lax.rsqrt(jnp.mean(x * x, axis=-1, keepdims=True) + EPS)


def _silu(x):
    return x * jax.nn.sigmoid(x)


def _norm_swiglu_kernel(x_ref, g_ref, w1_ref, w3_ref, o_ref, h_ref):
    @pl.when(pl.program_id(1) == 0)
    def _():
        h_ref[...] = (_rms_rows(x_ref[...]) * g_ref[...]).astype(BF16)

    h = h_ref[...]
    a = _dot(h, w1_ref[...])
    b = _dot(h, w3_ref[...])
    o_ref[...] = (_silu(a) * b).astype(o_ref.dtype)


def norm_swiglu(x, gain, w1, w3, layer, tm, tn):
    m, k = x.shape
    n = w1.shape[2]
    return pl.pallas_call(
        _norm_swiglu_kernel,
        grid=(m // tm, n // tn),
        in_specs=[pl.BlockSpec((tm, k), lambda i, j: (i, 0)),
                  pl.BlockSpec((1, k), lambda i, j: (0, 0)),
                  pl.BlockSpec((None, k, tn), lambda i, j: (layer, 0, j)),
                  pl.BlockSpec((None, k, tn), lambda i, j: (layer, 0, j))],
        out_specs=pl.BlockSpec((tm, tn), lambda i, j: (i, j)),
        out_shape=jax.ShapeDtypeStruct((m, n), BF16),
        scratch_shapes=[pltpu.VMEM((tm, k), BF16)],
        compiler_params=_params("parallel", "arbitrary"),
        name="norm_swiglu",
    )(x, gain.reshape(1, k), w1, w3)


def _matmul_residual_kernel(y_ref, w_ref, x_ref, o_ref):
    o_ref[...] = x_ref[...] + _dot(y_ref[...], w_ref[...])


def matmul_residual(y, w, layer, x, tm, tn):
    m, k = y.shape
    n = w.shape[2]
    return pl.pallas_call(
        _matmul_residual_kernel,
        grid=(m // tm, n // tn),
        in_specs=[pl.BlockSpec((tm, k), lambda i, j: (i, 0)),
                  pl.BlockSpec((None, k, tn), lambda i, j: (layer, 0, j)),
                  pl.BlockSpec((tm, tn), lambda i, j: (i, j))],
        out_specs=pl.BlockSpec((tm, tn), lambda i, j: (i, j)),
        out_shape=jax.ShapeDtypeStruct((m, n), F32),
        compiler_params=_params("parallel", "parallel"),
        name="matmul_residual",
    )(y, w, x)


def _matmul2_residual_kernel(ya_ref, yb_ref, wa_ref, wb_ref, x_ref, o_ref):
    o_ref[...] = x_ref[...] + _dot(ya_ref[...], wa_ref[...]) + _dot(yb_ref[...], wb_ref[...])


def matmul2_residual(ya, yb, w, layer, x, tm, tn):
    m, ka = ya.shape
    kb = yb.shape[1]
    n = w.shape[2]
    assert ka % kb == 0
    return pl.pallas_call(
        _matmul2_residual_kernel,
        grid=(m // tm, n // tn),
        in_specs=[pl.BlockSpec((tm, ka), lambda i, j: (i, 0)),
                  pl.BlockSpec((tm, kb), lambda i, j: (i, 0)),
                  pl.BlockSpec((None, ka, tn), lambda i, j: (layer, 0, j)),
                  pl.BlockSpec((None, kb, tn), lambda i, j: (layer, ka // kb, j)),
                  pl.BlockSpec((tm, tn), lambda i, j: (i, j))],
        out_specs=pl.BlockSpec((tm, tn), lambda i, j: (i, j)),
        out_shape=jax.ShapeDtypeStruct((m, n), F32),
        compiler_params=_params("parallel", "parallel"),
        name="matmul2_residual",
    )(ya, yb, w, w, x)


def _retention_kernel(lg_ref, q_ref, k_ref, v_ref, g_ref, cos_ref, sin_ref, o_ref, state_ref,
                      *, chunk, n_chunks):
    @pl.when(pl.program_id(1) == 0)
    def _():
        state_ref[...] = jnp.zeros_like(state_ref)

    c = chunk
    half = RET_QK_DIM // 2
    lg = lg_ref[0]
    lg_l = lg[:, :LANES]
    ii = lax.broadcasted_iota(jnp.int32, (c, c), 0)
    jj = lax.broadcasted_iota(jnp.int32, (c, c), 1)
    rel = (ii - jj).astype(F32)
    inner = jnp.where(rel >= 0, jnp.exp(lg[:, :c] * jnp.maximum(rel, 0.0)), 0.0)
    idx = lax.broadcasted_iota(jnp.int32, (c, LANES), 0).astype(F32)
    q_dec = jnp.exp(lg_l * (idx + 1.0))
    k_dec = jnp.exp(lg_l * (c - 1.0 - idx))
    c_dec = jnp.exp(lg * float(c))
    k_scale = RET_QK_DIM ** -0.5

    def rot(t, cos, sin):
        t1, t2 = t[:, :half], t[:, half:]
        return t1 * cos - t2 * sin, t1 * sin + t2 * cos

    rows = [pl.ds(ci * c, c) for ci in range(n_chunks)]
    qb, kb, qd, kd = [], [], [], []
    for r in rows:
        cos = cos_ref[r, :]
        sin = sin_ref[r, :]
        q1, q2 = rot(q_ref[r, :].astype(F32), cos, sin)
        k1, k2 = rot(k_ref[r, :].astype(F32) * k_scale, cos, sin)
        qb.append(jnp.concatenate([q1, q2], axis=-1).astype(BF16))
        kb.append(jnp.concatenate([k1, k2], axis=-1).astype(BF16))
        qd.append(jnp.concatenate([q1 * q_dec, q2 * q_dec], axis=-1).astype(BF16))
        kd.append(jnp.concatenate([k1 * k_dec, k2 * k_dec], axis=-1).astype(BF16))
    scores = [(_dot_nt(a, b) * inner).astype(BF16) for a, b in zip(qb, kb)]
    kv = [_dot_tn(a, v_ref[r, :]) for a, r in zip(kd, rows)]
    states = [state_ref[...]]
    for inc in kv:
        states.append(states[-1] * c_dec + inc)
    state_ref[...] = states[-1]
    outs = [_dot(sc, v_ref[r, :]) + _dot(a, st.astype(BF16))
            for sc, r, a, st in zip(scores, rows, qd, states)]
    for o, r in zip(outs, rows):
        gate = _silu(g_ref[r, :].astype(F32))
        o_ref[r, :] = (_rms_rows(o) * gate).astype(o_ref.dtype)


def retention_mixer(proj, cos, sin, rows_per_step):
    s = proj.shape[0]
    t = rows_per_step
    log_gamma = jnp.log(1.0 - 2.0 ** (-5.0 - jnp.arange(RET_HEADS, dtype=F32)))
    lg = jnp.broadcast_to(log_gamma[:, None, None], (RET_HEADS, 1, RET_V_DIM))
    qk_blocks = AR_QK // RET_QK_DIM
    v_off = 2 * AR_QK // RET_V_DIM
    g_off = v_off + AR_V // RET_V_DIM
    kern = functools.partial(_retention_kernel, chunk=RET_CHUNK, n_chunks=t // RET_CHUNK)
    return pl.pallas_call(
        kern,
        grid=(RET_HEADS, s // t),
        in_specs=[pl.BlockSpec((1, 1, RET_V_DIM), lambda h, i: (h, 0, 0)),
                  pl.BlockSpec((t, RET_QK_DIM), lambda h, i: (i, h)),
                  pl.BlockSpec((t, RET_QK_DIM), lambda h, i: (i, qk_blocks + h)),
                  pl.BlockSpec((t, RET_V_DIM), lambda h, i: (i, v_off + h)),
                  pl.BlockSpec((t, RET_V_DIM), lambda h, i: (i, g_off + h)),
                  pl.BlockSpec((t, RET_QK_DIM // 2), lambda h, i: (i, 0)),
                  pl.BlockSpec((t, RET_QK_DIM // 2), lambda h, i: (i, 0))],
        out_specs=pl.BlockSpec((t, RET_V_DIM), lambda h, i: (i, h)),
        out_shape=jax.ShapeDtypeStruct((s, AR_V), BF16),
        scratch_shapes=[pltpu.VMEM((RET_QK_DIM, RET_V_DIM), F32)],
        compiler_params=_params("parallel", "arbitrary"),
        name="retention",
    )(lg, proj, proj, proj, proj, cos, sin)


def _ar_in_proj_kernel(x_ref, g_ref, w_ref, gn_ref, oa_ref, o1_ref, o4_ref, o16_ref,
                       h_ref, buf, nb, *, n_a):
    j = pl.program_id(1)
    tm, tn = buf.shape
    dd = DIL_DIM

    @pl.when(j == 0)
    def _():
        h_ref[...] = (_rms_rows(x_ref[...]) * g_ref[...]).astype(BF16)

    @pl.when(j < n_a)
    def _():
        oa_ref[...] = _dot(h_ref[...], w_ref[...]).astype(oa_ref.dtype)

    def head_major(normed):
        buf[...] = _dot(h_ref[...], w_ref[...])
        for hd in range(tn // dd):
            slot = hd % 2
            x = buf[:, hd * dd:(hd + 1) * dd]
            nb[slot] = _rms_rows(x) * gn_ref[...] if normed else x
            o1_ref[hd] = nb[slot].astype(o1_ref.dtype)
            for r in range(4):
                o4_ref[hd, :, r * dd:(r + 1) * dd] = nb[slot, pl.ds(r, tm // 4, stride=4), :].astype(o4_ref.dtype)
            for r in range(16):
                o16_ref[hd, :, r * dd:(r + 1) * dd] = nb[slot, pl.ds(r, tm // 16, stride=16), :].astype(o16_ref.dtype)

    @pl.when((j >= n_a) & (j < n_a + 2))
    def _():
        head_major(True)

    @pl.when(j == n_a + 2)
    def _():
        head_major(False)


def ar_in_proj(x, gain, w, q_gain, k_gain, tm):
    m, k = x.shape
    dd = DIL_DIM
    tn = DIL_W
    n_a = (2 * AR_QK + 2 * AR_V) // tn
    nh = 3 * DIL_HEADS
    gains = jnp.stack([q_gain * (dd ** -0.5), k_gain, jnp.ones_like(k_gain)]).reshape(3, 1, dd)

    def grp(j):
        return jnp.clip(j - n_a, 0, 2)

    return pl.pallas_call(
        functools.partial(_ar_in_proj_kernel, n_a=n_a),
        grid=(m // tm, n_a + 3),
        in_specs=[pl.BlockSpec((tm, k), lambda i, j: (i, 0)),
                  pl.BlockSpec((1, k), lambda i, j: (0, 0)),
                  pl.BlockSpec((None, k, tn), lambda i, j: (0, 0, j)),
                  pl.BlockSpec((None, 1, dd), lambda i, j: (grp(j), 0, 0))],
        out_specs=[pl.BlockSpec((tm, tn), lambda i, j: (i, jnp.minimum(j, n_a - 1)))]
        + [pl.BlockSpec((DIL_HEADS, tm // d, d * dd), lambda i, j: (grp(j), i, 0)) for d in (1, 4, 16)],
        out_shape=[jax.ShapeDtypeStruct((m, n_a * tn), BF16)]
        + [jax.ShapeDtypeStruct((nh, m // d, d * dd), BF16) for d in (1, 4, 16)],
        scratch_shapes=[pltpu.VMEM((tm, k), BF16),
                        pltpu.VMEM((tm, tn), F32),
                        pltpu.VMEM((2, tm, dd), F32)],
        compiler_params=_params("arbitrary", "arbitrary"),
        name="ar_in_proj",
    )(x, gain.reshape(1, k), w, gains)


def _dilated_kernel(bias_ref,
                    q1, k1, k1p, v1, v1p,
                    q4, k4, k4p, v4, v4p,
                    q16, k16, k16p, v16, v16p,
                    o_ref, acc_ref, m_ref, l_ref, *, tile, group):
    n = DIL_STEPS
    d_dim = DIL_DIM
    first = pl.program_id(1) == 0

    def run_group(bias_idx, blocks, mode):
        def ld(p):
            return p[0][p[1]]

        bias_p = bias_ref[bias_idx, :, :n]
        bias_c = bias_ref[bias_idx, :, n:]
        sp = [_dot_nt(ld(b[0]), ld(b[1])) + bias_p for b in blocks]
        sc = [_dot_nt(ld(b[0]), ld(b[2])) + bias_c for b in blocks]
        sp = [jnp.where(first, NEG, x) if b[5] else x for x, b in zip(sp, blocks)]
        m = [jnp.maximum(jnp.max(x, axis=-1, keepdims=True), jnp.max(y, axis=-1, keepdims=True))
             for x, y in zip(sp, sc)]
        pp = [jnp.exp(x - mm) for x, mm in zip(sp, m)]
        pc = [jnp.exp(x - mm) for x, mm in zip(sc, m)]
        l = [jnp.sum(x, axis=-1, keepdims=True) + jnp.sum(y, axis=-1, keepdims=True)
             for x, y in zip(pp, pc)]
        o = [_dot(x.astype(BF16), ld(b[3])) + _dot(y.astype(BF16), ld(b[4]))
             for x, y, b in zip(pp, pc, blocks)]
        for b, mm, ll, oo in zip(blocks, m, l, o):
            rows = b[6]
            if mode == "init":
                m_ref[rows, :] = jnp.broadcast_to(mm, (n, d_dim))
                l_ref[rows, :] = jnp.broadcast_to(ll, (n, d_dim))
                acc_ref[rows, :] = oo
                continue
            m_old = m_ref[rows, :]
            m_new = jnp.maximum(m_old, mm)
            a_old = jnp.exp(m_old - m_new)
            a_cur = jnp.exp(mm - m_new)
            l_new = l_ref[rows, :] * a_old + ll * a_cur
            acc_new = acc_ref[rows, :] * a_old + oo * a_cur
            if mode == "merge":
                m_ref[rows, :] = m_new
                l_ref[rows, :] = l_new
                acc_ref[rows, :] = acc_new
            else:
                o_ref[rows, :] = (acc_new / l_new).astype(o_ref.dtype)

    every = slice(None)

    blocks = []
    for r in range(16):
        ls = slice(r * d_dim, (r + 1) * d_dim)
        blocks.append(((q16, (every, ls)), (k16p, (every, ls)), (k16, (every, ls)),
                       (v16p, (every, ls)), (v16, (every, ls)), True, pl.ds(r, n, stride=16)))
    for g0 in range(0, len(blocks), group):
        run_group(2, blocks[g0:g0 + group], "init")

    blocks = []
    for r in range(4):
        ls = slice(r * d_dim, (r + 1) * d_dim)
        for b in range(tile // (4 * n)):
            cur = pl.ds(b * n, n)
            prev = pl.ds((b - 1) * n, n)
            kp, vp = (((k4p, (every, ls)), (v4p, (every, ls))) if b == 0
                      else ((k4, (prev, ls)), (v4, (prev, ls))))
            blocks.append(((q4, (cur, ls)), kp, (k4, (cur, ls)), vp, (v4, (cur, ls)), b == 0,
                           pl.ds(b * 4 * n + r, n, stride=4)))
    for g0 in range(0, len(blocks), group):
        run_group(1, blocks[g0:g0 + group], "merge")

    blocks = []
    for b in range(tile // n):
        cur = pl.ds(b * n, n)
        prev = pl.ds((b - 1) * n, n)
        kp, vp = (((k1p, (every, every)), (v1p, (every, every))) if b == 0
                  else ((k1, (prev, every)), (v1, (prev, every))))
        blocks.append(((q1, (cur, every)), kp, (k1, (cur, every)), vp, (v1, (cur, every)), b == 0, cur))
    for g0 in range(0, len(blocks), group):
        run_group(0, blocks[g0:g0 + group], "final")


def _t5_bucket(dist):
    exact = REL_BUCKETS // 2
    large = exact + (jnp.log(jnp.maximum(dist, exact).astype(F32) / exact)
                     / math.log(REL_MAX_DIST / exact) * (REL_BUCKETS - exact)).astype(jnp.int32)
    large = jnp.minimum(large, REL_BUCKETS - 1)
    return jnp.where(dist < exact, dist, large)


def _dilated_bias(rel_bias):
    n = DIL_STEPS
    steps = jnp.arange(n + 1)
    vec = jnp.stack([rel_bias.astype(F32)[_t5_bucket(steps * dil)] for (_, dil) in DIL_PATTERNS])
    vec = vec.transpose(0, 2, 1)
    width = 3 * n
    p = jnp.full(vec.shape[:2] + (width,), NEG, F32)
    p = p.at[:, :, n - 1:2 * n].set(vec[:, :, ::-1])
    tiled = jnp.broadcast_to(p[:, :, None, :], vec.shape[:2] + (n, width))
    skew = tiled.reshape(vec.shape[:2] + (n * width,))[:, :, :n * (width - 1)]
    skew = skew.reshape(vec.shape[:2] + (n, width - 1))
    return skew[:, :, :, n - 1:3 * n - 1]


def dilated_attention(qkv1, qkv4, qkv16, rel_bias, tile, group):
    _, s, dd = qkv1.shape
    n = DIL_STEPS
    h_n = DIL_HEADS
    bias = _dilated_bias(rel_bias)
    args, specs = [bias], [pl.BlockSpec((len(DIL_PATTERNS), None, n, 2 * n), lambda h, i: (0, h, 0, 0))]
    for dil, view in ((1, qkv1), (4, qkv4), (16, qkv16)):
        rows = tile // dil
        per = rows // n

        def cur_spec(off, rows=rows, dil=dil):
            return pl.BlockSpec((None, rows, dil * dd), lambda h, i: (off + h, i, 0))

        def prev_spec(off, per=per, dil=dil):
            return pl.BlockSpec((None, n, dil * dd), lambda h, i: (off + h, jnp.maximum(i * per - 1, 0), 0))

        args += [view, view, view, view, view]
        specs += [cur_spec(0), cur_spec(h_n), prev_spec(h_n), cur_spec(2 * h_n), prev_spec(2 * h_n)]
    return pl.pallas_call(
        functools.partial(_dilated_kernel, tile=tile, group=group),
        grid=(h_n, s // tile),
        in_specs=specs,
        out_specs=pl.BlockSpec((tile, dd), lambda h, i: (i, h)),
        out_shape=jax.ShapeDtypeStruct((s, h_n * dd), BF16),
        scratch_shapes=[pltpu.VMEM((tile, dd), F32)] * 3,
        compiler_params=_params("parallel", "arbitrary"),
        name="dilated_attention",
    )(*args)


def _gdn_in_proj_kernel(x_ref, g_ref, w_ref, ws_ref, cw_ref, oc_ref, oz_ref, os_ref,
                        h_ref, buf, tails, *, n_norm, n_conv, q_tiles):
    i = pl.program_id(0)
    j = pl.program_id(1)
    tm = buf.shape[0] - 8
    tn = buf.shape[1]
    halo = 8
    sub = 128
    dd = GDN_DIM

    @pl.when((i == 0) & (j == 0))
    def _():
        tails[...] = jnp.zeros_like(tails)

    @pl.when(j == 0)
    def _():
        h = (_rms_rows(x_ref[...]) * g_ref[...]).astype(BF16)
        h_ref[...] = h
        os_ref[...] = _dot(h, ws_ref[...])

    def conv_tile(l2norm):
        buf[halo:halo + tm, :] = _dot(h_ref[...], w_ref[...])
        buf[0:halo, :] = tails[j]
        tails[j] = buf[tm:tm + halo, :]
        scale = jnp.where(j < q_tiles, dd ** -0.5, 1.0)
        for r0 in range(0, tm, sub):
            y = buf[halo + r0:halo + r0 + sub, :] * cw_ref[GDN_CONV - 1:GDN_CONV, :]
            for t in range(GDN_CONV - 1):
                off = halo + r0 - (GDN_CONV - 1) + t
                y = y + buf[off:off + sub, :] * cw_ref[t:t + 1, :]
            y = _silu(y)
            if l2norm:
                for h0 in range(0, tn, dd):
                    yh = y[:, h0:h0 + dd]
                    yh = yh * (lax.rsqrt(jnp.sum(yh * yh, axis=-1, keepdims=True) + EPS) * scale)
                    oc_ref[r0:r0 + sub, h0:h0 + dd] = yh.astype(oc_ref.dtype)
            else:
                oc_ref[r0:r0 + sub, :] = y.astype(oc_ref.dtype)

    @pl.when(j < n_norm)
    def _():
        conv_tile(True)

    @pl.when((j >= n_norm) & (j < n_conv))
    def _():
        conv_tile(False)

    @pl.when(j >= n_conv)
    def _():
        oz_ref[...] = _dot(h_ref[...], w_ref[...]).astype(oz_ref.dtype)


def gdn_in_proj(x, gain, w, w_side, conv_w, tm, tn):
    m, k = x.shape
    ns = w_side.shape[1]
    n_conv = GDN_QKV // tn
    n_norm = 2 * GDN_QK_W // tn
    n_tiles = (GDN_QKV + GDN_V_W) // tn
    kern = functools.partial(_gdn_in_proj_kernel, n_norm=n_norm, n_conv=n_conv, q_tiles=GDN_QK_W // tn)
    return pl.pallas_call(
        kern,
        grid=(m // tm, n_tiles),
        in_specs=[pl.BlockSpec((tm, k), lambda i, j: (i, 0)),
                  pl.BlockSpec((1, k), lambda i, j: (0, 0)),
                  pl.BlockSpec((None, k, tn), lambda i, j: (0, 0, j)),
                  pl.BlockSpec((k, ns), lambda i, j: (0, 0)),
                  pl.BlockSpec((GDN_CONV, tn), lambda i, j: (0, jnp.minimum(j, n_conv - 1)))],
        out_specs=[pl.BlockSpec((tm, tn), lambda i, j: (i, jnp.minimum(j, n_conv - 1))),
                   pl.BlockSpec((tm, tn), lambda i, j: (i, jnp.maximum(j - n_conv, 0))),
                   pl.BlockSpec((tm, ns), lambda i, j: (i, 0))],
        out_shape=[jax.ShapeDtypeStruct((m, GDN_QKV), BF16),
                   jax.ShapeDtypeStruct((m, GDN_V_W), BF16),
                   jax.ShapeDtypeStruct((m, ns), F32)],
        scratch_shapes=[pltpu.VMEM((tm, k), BF16),
                        pltpu.VMEM((8 + tm, tn), F32),
                        pltpu.VMEM((n_conv, 8, tn), F32)],
        compiler_params=_params("arbitrary", "arbitrary"),
        name="gdn_in_proj",
    )(x, gain.reshape(1, k), w, w_side, conv_w)


def _gdn_gates_kernel(ab_ref, alog_ref, dt_ref, beta_ref, gcum_ref, gct_ref, *, chunk):
    t = ab_ref.shape[0]
    hv = GDN_V_HEADS
    ab = ab_ref[...]
    beta_ref[...] = jax.nn.sigmoid(ab[:, :hv])
    z = ab[:, hv:] + dt_ref[...]
    softplus = jnp.maximum(z, 0.0) + jnp.log1p(jnp.exp(-jnp.abs(z)))
    g = -jnp.exp(alog_ref[...]) * softplus
    ii = lax.broadcasted_iota(jnp.int32, (t, t), 0)
    jj = lax.broadcasted_iota(jnp.int32, (t, t), 1)
    tri = ((ii >= jj) & (ii // chunk == jj // chunk)).astype(F32)
    gcum = _dot_f32(tri, g)
    gcum_ref[...] = gcum
    eh = lax.broadcasted_iota(jnp.int32, (hv, hv), 0) == lax.broadcasted_iota(jnp.int32, (hv, hv), 1)
    gct_ref[...] = _dot_nt_f32(eh.astype(F32), gcum)


def gdn_gates(ab, a_log, dt_bias, tm):
    s = ab.shape[0]
    hv = GDN_V_HEADS
    return pl.pallas_call(
        functools.partial(_gdn_gates_kernel, chunk=GDN_CHUNK),
        grid=(s // tm,),
        in_specs=[pl.BlockSpec((tm, 2 * hv), lambda i: (i, 0)),
                  pl.BlockSpec((1, hv), lambda i: (0, 0)),
                  pl.BlockSpec((1, hv), lambda i: (0, 0))],
        out_specs=[pl.BlockSpec((tm, hv), lambda i: (i, 0)),
                   pl.BlockSpec((tm, hv), lambda i: (i, 0)),
                   pl.BlockSpec((hv, tm), lambda i: (0, i))],
        out_shape=[jax.ShapeDtypeStruct((s, hv), F32), jax.ShapeDtypeStruct((s, hv), F32),
                   jax.ShapeDtypeStruct((hv, s), F32)],
        compiler_params=_params("parallel"),
        name="gdn_gates",
    )(ab, a_log.reshape(1, hv), dt_bias.reshape(1, hv))


def _unit_lower_inverses(a_list):
    c = a_list[0].shape[0]
    ii = lax.broadcasted_iota(jnp.int32, (c, c), 0)
    jj = lax.broadcasted_iota(jnp.int32, (c, c), 1)
    eye = jnp.where(ii == jj, 1.0, 0.0)
    p = [-a for a in a_list]
    t = [eye + x for x in p]
    pb = [x.astype(BF16) for x in p]
    p = [_dot(x, x) for x in pb]
    span = 4
    while span < c:
        pb = [x.astype(BF16) for x in p]
        st = [_dot(jnp.concatenate([x, y.astype(BF16)], axis=0), x) for x, y in zip(pb, t)]
        p = [x[:c] for x in st]
        t = [y + x[c:] for x, y in zip(st, t)]
        span *= 2
    pb = [x.astype(BF16) for x in p]
    return [y + _dot(y.astype(BF16), x) for x, y in zip(pb, t)]


def _gdn_kernel(q_ref, k_ref, v_ref, z_ref, beta_ref, gcum_ref, gct_ref, ng_ref, o_ref, state_ref,
                *, tb, chunk, kheads):
    grp = pl.program_id(0)
    c = chunk
    dd = GDN_DIM
    nh = 2 * kheads
    nc = tb // c

    @pl.when(pl.program_id(1) == 0)
    def _():
        state_ref[...] = jnp.zeros_like(state_ref)

    lane_h = lax.broadcasted_iota(jnp.int32, (1, GDN_V_HEADS), 1)
    ii = lax.broadcasted_iota(jnp.int32, (c, c), 0)
    jj = lax.broadcasted_iota(jnp.int32, (c, c), 1)
    causal = ii >= jj
    strict = ii > jj
    ng = ng_ref[...]

    def head_cols(h):
        return slice(h * dd, (h + 1) * dd)

    for ci in range(nc):
        r = slice(ci * c, (ci + 1) * c)
        beta_t = beta_ref[r, :]
        gcum_t = gcum_ref[r, :]
        beta_col, gc_col, gc_row = [], [], []
        for h in range(nh):
            hv = grp * nh + h
            sel = lane_h == hv
            beta_col.append(jnp.sum(jnp.where(sel, beta_t, 0.0), axis=-1, keepdims=True))
            gc_col.append(jnp.sum(jnp.where(sel, gcum_t, 0.0), axis=-1, keepdims=True))
            gc_row.append(gct_ref[pl.ds(hv, 1), :][:, r])
        eg_col = [jnp.exp(x) for x in gc_col]

        k16 = [k_ref[r, head_cols(kh)] for kh in range(kheads)]
        q16 = [q_ref[r, head_cols(kh)] for kh in range(kheads)]
        kf = [x.astype(F32) for x in k16]
        qkk = [_dot_nt(jnp.concatenate([q, k], axis=0), k) for q, k in zip(q16, k16)]

        decay, a_list = [], []
        for h in range(nh):
            dlog = gc_col[h] - gc_row[h]
            dec = jnp.where(causal, jnp.exp(jnp.where(causal, dlog, 0.0)), 0.0)
            decay.append(dec)
            a_list.append(jnp.where(strict, qkk[h // 2][c:] * beta_col[h] * dec, 0.0))
        t_list = _unit_lower_inverses(a_list)

        uw = []
        for h in range(nh):
            rhs = jnp.concatenate([v_ref[r, head_cols(h)].astype(F32) * beta_col[h],
                                   kf[h // 2] * (beta_col[h] * eg_col[h])], axis=1)
            uw.append(_dot(t_list[h].astype(BF16), rhs.astype(BF16)))

        states = [state_ref[h] for h in range(nh)]
        s16 = [s.astype(BF16) for s in states]
        wq = []
        for h in range(nh):
            lhs = jnp.concatenate([uw[h][:, dd:], q16[h // 2].astype(F32) * eg_col[h]], axis=0)
            wq.append(_dot(lhs.astype(BF16), s16[h]))
        vn16 = [(uw[h][:, :dd] - wq[h][:c]).astype(BF16) for h in range(nh)]
        outs = []
        for h in range(nh):
            intra = jnp.where(causal, qkk[h // 2][:c] * decay[h], 0.0)
            outs.append(wq[h][c:] + _dot(intra.astype(BF16), vn16[h]))
        for h in range(nh):
            g_last = gc_col[h][c - 1:c, :]
            kd = kf[h // 2] * jnp.exp(g_last - gc_col[h])
            state_ref[h] = states[h] * jnp.exp(g_last) + _dot_tn(kd.astype(BF16), vn16[h])
        for h in range(nh):
            zg = _silu(z_ref[r, head_cols(h)].astype(F32))
            o_ref[r, head_cols(h)] = (_rms_rows(outs[h]) * ng * zg).astype(o_ref.dtype)


def gdn_mixer(qkv, z, beta, gcum, gct, norm_gain, tb, kheads):
    s = qkv.shape[0]
    dd = GDN_DIM
    wk = kheads * dd
    wv = 2 * wk
    k0 = GDN_QK_W // wk
    v0 = 2 * GDN_QK_W // wv
    return pl.pallas_call(
        functools.partial(_gdn_kernel, tb=tb, chunk=GDN_CHUNK, kheads=kheads),
        grid=(GDN_K_HEADS // kheads, s // tb),
        in_specs=[pl.BlockSpec((tb, wk), lambda g, i: (i, g)),
                  pl.BlockSpec((tb, wk), lambda g, i: (i, k0 + g)),
                  pl.BlockSpec((tb, wv), lambda g, i: (i, v0 + g)),
                  pl.BlockSpec((tb, wv), lambda g, i: (i, g)),
                  pl.BlockSpec((tb, GDN_V_HEADS), lambda g, i: (i, 0)),
                  pl.BlockSpec((tb, GDN_V_HEADS), lambda g, i: (i, 0)),
                  pl.BlockSpec((GDN_V_HEADS, tb), lambda g, i: (0, i)),
                  pl.BlockSpec((1, dd), lambda g, i: (0, 0))],
        out_specs=pl.BlockSpec((tb, wv), lambda g, i: (i, g)),
        out_shape=jax.ShapeDtypeStruct((s, GDN_V_W), BF16),
        scratch_shapes=[pltpu.VMEM((2 * kheads, dd, dd), F32)],
        compiler_params=_params("parallel", "arbitrary"),
        name="gated_deltanet",
    )(qkv, qkv, qkv, z, beta, gcum, gct, norm_gain.reshape(1, dd))


def _xa_kv_kernel(mem_ref, mg_ref, w_ref, kg_ref, k_ref, v_ref):
    mem_n = (_rms_rows(mem_ref[...]) * mg_ref[...]).astype(BF16)
    kv = _dot(mem_n, w_ref[...])
    for h in range(XA_HEADS):
        k = kv[:, h * XA_DIM:(h + 1) * XA_DIM]
        k_ref[h] = (_rms_rows(k) * kg_ref[...]).astype(k_ref.dtype)
        v_ref[h] = kv[:, (XA_HEADS + h) * XA_DIM:(XA_HEADS + h + 1) * XA_DIM].astype(v_ref.dtype)


def xa_keys_values(mem, mem_gain, w_kv, layer, k_gain):
    m, d = mem.shape
    shape = jax.ShapeDtypeStruct((XA_HEADS, m, XA_DIM), BF16)
    nkv = w_kv.shape[2]
    return pl.pallas_call(
        _xa_kv_kernel,
        grid=(1,),
        in_specs=[pl.BlockSpec((m, d), lambda i: (0, 0)),
                  pl.BlockSpec((1, d), lambda i: (0, 0)),
                  pl.BlockSpec((None, d, nkv), lambda i: (layer, 0, 0)),
                  pl.BlockSpec((1, XA_DIM), lambda i: (0, 0))],
        out_specs=[pl.BlockSpec((XA_HEADS, m, XA_DIM), lambda i: (0, 0, 0))] * 2,
        out_shape=[shape, shape],
        compiler_params=_params("arbitrary"),
        name="xa_keys_values",
    )(mem, mem_gain.reshape(1, d), w_kv, k_gain.reshape(1, XA_DIM))


def _xattn_kernel(x_ref, g_ref, wq_ref, qg_ref, k_ref, v_ref, wo_ref, o_ref):
    x = x_ref[...]
    h = (_rms_rows(x) * g_ref[...]).astype(BF16)
    q = _dot(h, wq_ref[...])
    qg = qg_ref[...] * (XA_DIM ** -0.5)
    heads = range(XA_HEADS)
    qh = [(_rms_rows(q[:, hd * XA_DIM:(hd + 1) * XA_DIM]) * qg).astype(BF16) for hd in heads]
    sc = [_dot_nt(qh[hd], k_ref[hd]) for hd in heads]
    pr = [jnp.exp(s - jnp.max(s, axis=-1, keepdims=True)) for s in sc]
    pr = [p / jnp.sum(p, axis=-1, keepdims=True) for p in pr]
    outs = [_dot(pr[hd].astype(BF16), v_ref[hd]).astype(BF16) for hd in heads]
    o = jnp.concatenate(outs, axis=-1)
    o_ref[...] = x + _dot(o, wo_ref[...])


def cross_attention(x, gain, w_q, q_gain, k, v, w_o, layer, tm):
    s, d = x.shape
    hw = XA_HEADS * XA_DIM
    m = k.shape[1]
    return pl.pallas_call(
        _xattn_kernel,
        grid=(s // tm,),
        in_specs=[pl.BlockSpec((tm, d), lambda i: (i, 0)),
                  pl.BlockSpec((1, d), lambda i: (0, 0)),
                  pl.BlockSpec((None, d, hw), lambda i: (layer, 0, 0)),
                  pl.BlockSpec((1, XA_DIM), lambda i: (0, 0)),
                  pl.BlockSpec((XA_HEADS, m, XA_DIM), lambda i: (0, 0, 0)),
                  pl.BlockSpec((XA_HEADS, m, XA_DIM), lambda i: (0, 0, 0)),
                  pl.BlockSpec((None, hw, d), lambda i: (layer, 0, 0))],
        out_specs=pl.BlockSpec((tm, d), lambda i: (i, 0)),
        out_shape=jax.ShapeDtypeStruct((s, d), F32),
        compiler_params=_params("parallel"),
        name="cross_attention",
    )(x, gain.reshape(1, d), w_q, q_gain.reshape(1, XA_DIM), k, v, w_o)


def _pick_tile(total, target):
    t = min(total, target)
    while total % t:
        t //= 2
    return t


def _rotary_tables(s):
    inv = ROPE_BASE ** (-jnp.arange(0, RET_QK_DIM, 2, dtype=F32) / RET_QK_DIM)
    ang = jnp.arange(s).astype(F32)[:, None] * inv[None, :]
    return jnp.cos(ang), jnp.sin(ang)


def kernel(x, mem, norm_mix, norm_xa, norm_ffn, mem_norm, rel_bias, ar_w_in, ar_w_out, dil_q_gain, dil_k_gain, gdn_w_in, gdn_conv, gdn_a_log, gdn_dt_bias, gdn_norm, gdn_w_out, xa_w_q, xa_w_kv, xa_w_o, xa_q_gain, xa_k_gain, ffn_w1, ffn_w3, ffn_w2):
    b, s, d = x.shape
    assert b == 1
    xs = x.reshape(s, d)
    mem2 = mem.reshape(mem.shape[1], d)
    tm = _pick_tile(s, 1024)
    ar_w_in, ar_w_out, gdn_w_out, xa_w_q, xa_w_kv, xa_w_o, ffn_w1, ffn_w3, ffn_w2 = (
        w.astype(BF16) for w in (ar_w_in, ar_w_out, gdn_w_out, xa_w_q, xa_w_kv, xa_w_o, ffn_w1, ffn_w3, ffn_w2))

    def tail(xs, layer):
        k, v = xa_keys_values(mem2, mem_norm, xa_w_kv, layer, xa_k_gain[layer])
        xs = cross_attention(xs, norm_xa[layer], xa_w_q, xa_q_gain[layer], k, v, xa_w_o, layer,
                             _pick_tile(s, 512))
        hid = norm_swiglu(xs, norm_ffn[layer], ffn_w1, ffn_w3, layer, tm, _pick_tile(ffn_w1.shape[-1], 512))
        return matmul_residual(hid, ffn_w2, layer, xs, tm, _pick_tile(d, 512))

    proj, qkv1, qkv4, qkv16 = ar_in_proj(xs, norm_mix[0], ar_w_in, dil_q_gain[0], dil_k_gain[0], tm)
    cos, sin = _rotary_tables(s)
    ya = retention_mixer(proj, cos, sin, _pick_tile(s, 1024))
    yb = dilated_attention(qkv1, qkv4, qkv16, rel_bias, _pick_tile(s, 2048), 8)
    xs = matmul2_residual(ya, yb, ar_w_out, 0, xs, tm, _pick_tile(d, 1024))
    xs = tail(xs, 0)

    n_main = GDN_QKV + GDN_V_W
    w_gate = gdn_w_in[0, :, n_main:].astype(BF16)
    qkv, z, ab = gdn_in_proj(xs, norm_mix[1], gdn_w_in.astype(BF16), w_gate, gdn_conv[0], tm, 1024)
    beta, gcum, gct = gdn_gates(ab, gdn_a_log[0], gdn_dt_bias[0], _pick_tile(s, 256))
    og = gdn_mixer(qkv, z, beta, gcum, gct, gdn_norm[0], 2 * GDN_CHUNK, 4)
    xs = matmul_residual(og, gdn_w_out, 0, xs, tm, _pick_tile(d, 512))
    xs = tail(xs, 1)
    return xs.reshape(b, s, d)
```

```python
import functools
import math

import jax
import jax.numpy as jnp
from jax import lax
from jax.experimental import pallas as pl
from jax.experimental.pallas import tpu as pltpu

F32 = jnp.float32
BF16 = jnp.bfloat16
EPS = 1e-6
NEG = -1e30

VMEM_LIMIT_BYTES = 56 * 1024 * 1024
LANES = 128

RET_HEADS = 4
RET_QK_DIM = 256
RET_V_DIM = 512
RET_CHUNK = 128
ROPE_BASE = 10000.0
DIL_HEADS = 8
DIL_DIM = 128
DIL_PATTERNS = ((128, 1), (512, 4), (2048, 16))
DIL_STEPS = 128
REL_BUCKETS = 32
REL_MAX_DIST = 2048
AR_QK = RET_HEADS * RET_QK_DIM
AR_V = RET_HEADS * RET_V_DIM
DIL_W = DIL_HEADS * DIL_DIM
GDN_K_HEADS = 16
GDN_V_HEADS = 32
GDN_DIM = 128
GDN_CONV = 4
GDN_CHUNK = 128
GDN_QK_W = GDN_K_HEADS * GDN_DIM
GDN_V_W = GDN_V_HEADS * GDN_DIM
GDN_QKV = 2 * GDN_QK_W + GDN_V_W
XA_HEADS = 4
XA_DIM = 128


def _params(*semantics):
    return pltpu.CompilerParams(dimension_semantics=semantics, vmem_limit_bytes=VMEM_LIMIT_BYTES)


def _dot(a, b):
    return jnp.dot(a, b, preferred_element_type=F32)


def _dot_nt(a, b):
    return lax.dot_general(a, b, (((1,), (1,)), ((), ())), preferred_element_type=F32)


def _dot_tn(a, b):
    return lax.dot_general(a, b, (((0,), (0,)), ((), ())), preferred_element_type=F32)


def _dot_f32(a, b):
    return jnp.dot(a, b, preferred_element_type=F32, precision=lax.Precision.HIGHEST)


def _dot_nt_f32(a, b):
    return lax.dot_general(a, b, (((1,), (1,)), ((), ())), preferred_element_type=F32,
                           precision=lax.Precision.HIGHEST)


def _rms_rows(x):
    return x * lax.rsqrt(jnp.mean(x * x, axis=-1, keepdims=True) + EPS)


def _silu(x):
    return x * jax.nn.sigmoid(x)


def _norm_swiglu_kernel(x_ref, g_ref, w1_ref, w3_ref, o_ref, h_ref):
    @pl.when(pl.program_id(1) == 0)
    def _():
        h_ref[...] = (_rms_rows(x_ref[...]) * g_ref[...]).astype(BF16)

    h = h_ref[...]
    a = _dot(h, w1_ref[...])
    b = _dot(h, w3_ref[...])
    o_ref[...] = (_silu(a) * b).astype(o_ref.dtype)


def norm_swiglu(x, gain, w1, w3, layer, tm, tn):
    m, k = x.shape
    n = w1.shape[2]
    return pl.pallas_call(
        _norm_swiglu_kernel,
        grid=(m // tm, n // tn),
        in_specs=[pl.BlockSpec((tm, k), lambda i, j: (i, 0)),
                  pl.BlockSpec((1, k), lambda i, j: (0, 0)),
                  pl.BlockSpec((None, k, tn), lambda i, j: (layer, 0, j)),
                  pl.BlockSpec((None, k, tn), lambda i, j: (layer, 0, j))],
        out_specs=pl.BlockSpec((tm, tn), lambda i, j: (i, j)),
        out_shape=jax.ShapeDtypeStruct((m, n), BF16),
        scratch_shapes=[pltpu.VMEM((tm, k), BF16)],
        compiler_params=_params("parallel", "arbitrary"),
        name="norm_swiglu",
    )(x, gain.reshape(1, k), w1, w3)


def _matmul_residual_kernel(y_ref, w_ref, x_ref, o_ref):
    o_ref[...] = x_ref[...] + _dot(y_ref[...], w_ref[...])


def matmul_residual(y, w, layer, x, tm, tn):
    m, k = y.shape
    n = w.shape[2]
    return pl.pallas_call(
        _matmul_residual_kernel,
        grid=(m // tm, n // tn),
        in_specs=[pl.BlockSpec((tm, k), lambda i, j: (i, 0)),
                  pl.BlockSpec((None, k, tn), lambda i, j: (layer, 0, j)),
                  pl.BlockSpec((tm, tn), lambda i, j: (i, j))],
        out_specs=pl.BlockSpec((tm, tn), lambda i, j: (i, j)),
        out_shape=jax.ShapeDtypeStruct((m, n), F32),
        compiler_params=_params("parallel", "parallel"),
        name="matmul_residual",
    )(y, w, x)


def _retention_kernel(lg_ref, q_ref, k_ref, v_ref, g_ref, cos_ref, sin_ref, o_ref, state_ref,
                      *, chunk, n_chunks):
    @pl.when(pl.program_id(1) == 0)
    def _():
        state_ref[...] = jnp.zeros_like(state_ref)

    c = chunk
    half = RET_QK_DIM // 2
    lg = lg_ref[0]
    lg_l = lg[:, :LANES]
    ii = lax.broadcasted_iota(jnp.int32, (c, c), 0)
    jj = lax.broadcasted_iota(jnp.int32, (c, c), 1)
    rel = (ii - jj).astype(F32)
    inner = jnp.where(rel >= 0, jnp.exp(lg[:, :c] * jnp.maximum(rel, 0.0)), 0.0)
    idx = lax.broadcasted_iota(jnp.int32, (c, LANES), 0).astype(F32)
    q_dec = jnp.exp(lg_l * (idx + 1.0))
    k_dec = jnp.exp(lg_l * (c - 1.0 - idx))
    c_dec = jnp.exp(lg * float(c))
    k_scale = RET_QK_DIM ** -0.5

    def rot(t, cos, sin):
        t1, t2 = t[:, :half], t[:, half:]
        return t1 * cos - t2 * sin, t1 * sin + t2 * cos

    rows = [pl.ds(ci * c, c) for ci in range(n_chunks)]
    qb, kb, qd, kd = [], [], [], []
    for r in rows:
        cos = cos_ref[r, :]
        sin = sin_ref[r, :]
        q1, q2 = rot(q_ref[r, :].astype(F32), cos, sin)
        k1, k2 = rot(k_ref[r, :].astype(F32) * k_scale, cos, sin)
        qb.append(jnp.concatenate([q1, q2], axis=-1).astype(BF16))
        kb.append(jnp.concatenate([k1, k2], axis=-1).astype(BF16))
        qd.append(jnp.concatenate([q1 * q_dec, q2 * q_dec], axis=-1).astype(BF16))
        kd.append(jnp.concatenate([k1 * k_dec, k2 * k_dec], axis=-1).astype(BF16))
    scores = [(_dot_nt(a, b) * inner).astype(BF16) for a, b in zip(qb, kb)]
    kv = [_dot_tn(a, v_ref[r, :]) for a, r in zip(kd, rows)]
    states = [state_ref[...]]
    for inc in kv:
        states.append(states[-1] * c_dec + inc)
    state_ref[...] = states[-1]
    outs = [_dot(sc, v_ref[r, :]) + _dot(a, st.astype(BF16))
            for sc, r, a, st in zip(scores, rows, qd, states)]
    for o, r in zip(outs, rows):
        gate = _silu(g_ref[r, :].astype(F32))
        o_ref[r, :] = (_rms_rows(o) * gate).astype(o_ref.dtype)


def retention_mixer(proj, cos, sin, rows_per_step):
    s = proj.shape[0]
    t = rows_per_step
    log_gamma = jnp.log(1.0 - 2.0 ** (-5.0 - jnp.arange(RET_HEADS, dtype=F32)))
    lg = jnp.broadcast_to(log_gamma[:, None, None], (RET_HEADS, 1, RET_V_DIM))
    qk_blocks = AR_QK // RET_QK_DIM
    v_off = 2 * AR_QK // RET_V_DIM
    g_off = v_off + AR_V // RET_V_DIM
    kern = functools.partial(_retention_kernel, chunk=RET_CHUNK, n_chunks=t // RET_CHUNK)
    return pl.pallas_call(
        kern,
        grid=(RET_HEADS, s // t),
        in_specs=[pl.BlockSpec((1, 1, RET_V_DIM), lambda h, i: (h, 0, 0)),
                  pl.BlockSpec((t, RET_QK_DIM), lambda h, i: (i, h)),
                  pl.BlockSpec((t, RET_QK_DIM), lambda h, i: (i, qk_blocks + h)),
                  pl.BlockSpec((t, RET_V_DIM), lambda h, i: (i, v_off + h)),
                  pl.BlockSpec((t, RET_V_DIM), lambda h, i: (i, g_off + h)),
                  pl.BlockSpec((t, RET_QK_DIM // 2), lambda h, i: (i, 0)),
                  pl.BlockSpec((t, RET_QK_DIM // 2), lambda h, i: (i, 0))],
        out_specs=pl.BlockSpec((t, RET_V_DIM), lambda h, i: (i, h)),
        out_shape=jax.ShapeDtypeStruct((s, AR_V), BF16),
        scratch_shapes=[pltpu.VMEM((RET_QK_DIM, RET_V_DIM), F32)],
        compiler_params=_params("parallel", "arbitrary"),
        name="retention",
    )(lg, proj, proj, proj, proj, cos, sin)


def _ar_in_proj_kernel(x_ref, g_ref, w_ref, gn_ref, oa_ref, o1_ref, o4_ref, o16_ref,
                       h_ref, buf, nb, *, n_a):
    j = pl.program_id(1)
    tm, tn = buf.shape
    dd = DIL_DIM

    @pl.when(j == 0)
    def _():
        h_ref[...] = (_rms_rows(x_ref[...]) * g_ref[...]).astype(BF16)

    @pl.when(j < n_a)
    def _():
        oa_ref[...] = _dot(h_ref[...], w_ref[...]).astype(oa_ref.dtype)

    def head_major(normed):
        buf[...] = _dot(h_ref[...], w_ref[...])
        for hd in range(tn // dd):
            slot = hd % 2
            x = buf[:, hd * dd:(hd + 1) * dd]
            nb[slot] = _rms_rows(x) * gn_ref[...] if normed else x
            o1_ref[hd] = nb[slot].astype(o1_ref.dtype)
            for r in range(4):
                o4_ref[hd, :, r * dd:(r + 1) * dd] = nb[slot, pl.ds(r, tm // 4, stride=4), :].astype(o4_ref.dtype)
            for r in range(16):
                o16_ref[hd, :, r * dd:(r + 1) * dd] = nb[slot, pl.ds(r, tm // 16, stride=16), :].astype(o16_ref.dtype)

    @pl.when((j >= n_a) & (j < n_a + 2))
    def _():
        head_major(True)

    @pl.when(j == n_a + 2)
    def _():
        head_major(False)


def ar_in_proj(x, gain, w, q_gain, k_gain, tm):
    m, k = x.shape
    dd = DIL_DIM
    tn = DIL_W
    n_a = (2 * AR_QK + 2 * AR_V) // tn
    nh = 3 * DIL_HEADS
    gains = jnp.stack([q_gain * (dd ** -0.5), k_gain, jnp.ones_like(k_gain)]).reshape(3, 1, dd)

    def grp(j):
        return jnp.clip(j - n_a, 0, 2)

    return pl.pallas_call(
        functools.partial(_ar_in_proj_kernel, n_a=n_a),
        grid=(m // tm, n_a + 3),
        in_specs=[pl.BlockSpec((tm, k), lambda i, j: (i, 0)),
                  pl.BlockSpec((1, k), lambda i, j: (0, 0)),
                  pl.BlockSpec((None, k, tn), lambda i, j: (0, 0, j)),
                  pl.BlockSpec((None, 1, dd), lambda i, j: (grp(j), 0, 0))],
        out_specs=[pl.BlockSpec((tm, tn), lambda i, j: (i, jnp.minimum(j, n_a - 1)))]
        + [pl.BlockSpec((DIL_HEADS, tm // d, d * dd), lambda i, j: (grp(j), i, 0)) for d in (1, 4, 16)],
        out_shape=[jax.ShapeDtypeStruct((m, n_a * tn), BF16)]
        + [jax.ShapeDtypeStruct((nh, m // d, d * dd), BF16) for d in (1, 4, 16)],
        scratch_shapes=[pltpu.VMEM((tm, k), BF16),
                        pltpu.VMEM((tm, tn), F32),
                        pltpu.VMEM((2, tm, dd), F32)],
        compiler_params=_params("arbitrary", "arbitrary"),
        name="ar_in_proj",
    )(x, gain.reshape(1, k), w, gains)


def _dilated_kernel(bias_ref,
                    q1, k1, k1p, v1, v1p,
                    q4, k4, k4p, v4, v4p,
                    q16, k16, k16p, v16, v16p,
                    o_ref, acc_ref, m_ref, l_ref, *, tile, group):
    n = DIL_STEPS
    d_dim = DIL_DIM
    first = pl.program_id(1) == 0

    def run_group(bias_idx, blocks, mode):
        def ld(p):
            return p[0][p[1]]

        bias_p = bias_ref[bias_idx, :, :n]
        bias_c = bias_ref[bias_idx, :, n:]
        sp = [_dot_nt(ld(b[0]), ld(b[1])) + bias_p for b in blocks]
        sc = [_dot_nt(ld(b[0]), ld(b[2])) + bias_c for b in blocks]
        sp = [jnp.where(first, NEG, x) if b[5] else x for x, b in zip(sp, blocks)]
        m = [jnp.maximum(jnp.max(x, axis=-1, keepdims=True), jnp.max(y, axis=-1, keepdims=True))
             for x, y in zip(sp, sc)]
        pp = [jnp.exp(x - mm) for x, mm in zip(sp, m)]
        pc = [jnp.exp(x - mm) for x, mm in zip(sc, m)]
        l = [jnp.sum(x, axis=-1, keepdims=True) + jnp.sum(y, axis=-1, keepdims=True)
             for x, y in zip(pp, pc)]
        o = [_dot(x.astype(BF16), ld(b[3])) + _dot(y.astype(BF16), ld(b[4]))
             for x, y, b in zip(pp, pc, blocks)]
        for b, mm, ll, oo in zip(blocks, m, l, o):
            rows = b[6]
            if mode == "init":
                m_ref[rows, :] = jnp.broadcast_to(mm, (n, d_dim))
                l_ref[rows, :] = jnp.broadcast_to(ll, (n, d_dim))
                acc_ref[rows, :] = oo
                continue
            m_old = m_ref[rows, :]
            m_new = jnp.maximum(m_old, mm)
            a_old = jnp.exp(m_old - m_new)
            a_cur = jnp.exp(mm - m_new)
            l_new = l_ref[rows, :] * a_old + ll * a_cur
            acc_new = acc_ref[rows, :] * a_old + oo * a_cur
            if mode == "merge":
                m_ref[rows, :] = m_new
                l_ref[rows, :] = l_new
                acc_ref[rows, :] = acc_new
            else:
                o_ref[rows, :] = (acc_new / l_new).astype(o_ref.dtype)

    every = slice(None)

    blocks = []
    for r in range(16):
        ls = slice(r * d_dim, (r + 1) * d_dim)
        blocks.append(((q16, (every, ls)), (k16p, (every, ls)), (k16, (every, ls)),
                       (v16p, (every, ls)), (v16, (every, ls)), True, pl.ds(r, n, stride=16)))
    for g0 in range(0, len(blocks), group):
        run_group(2, blocks[g0:g0 + group], "init")

    blocks = []
    for r in range(4):
        ls = slice(r * d_dim, (r + 1) * d_dim)
        for b in range(tile // (4 * n)):
            cur = pl.ds(b * n, n)
            prev = pl.ds((b - 1) * n, n)
            kp, vp = (((k4p, (every, ls)), (v4p, (every, ls))) if b == 0
                      else ((k4, (prev, ls)), (v4, (prev, ls))))
            blocks.append(((q4, (cur, ls)), kp, (k4, (cur, ls)), vp, (v4, (cur, ls)), b == 0,
                           pl.ds(b * 4 * n + r, n, stride=4)))
    for g0 in range(0, len(blocks), group):
        run_group(1, blocks[g0:g0 + group], "merge")

    blocks = []
    for b in range(tile // n):
        cur = pl.ds(b * n, n)
        prev = pl.ds((b - 1) * n, n)
        kp, vp = (((k1p, (every, every)), (v1p, (every, every))) if b == 0
                  else ((k1, (prev, every)), (v1, (prev, every))))
        blocks.append(((q1, (cur, every)), kp, (k1, (cur, every)), vp, (v1, (cur, every)), b == 0, cur))
    for g0 in range(0, len(blocks), group):
        run_group(0, blocks[g0:g0 + group], "final")


def _t5_bucket(dist):
    exact = REL_BUCKETS // 2
    large = exact + (jnp.log(jnp.maximum(dist, exact).astype(F32) / exact)
                     / math.log(REL_MAX_DIST / exact) * (REL_BUCKETS - exact)).astype(jnp.int32)
    large = jnp.minimum(large, REL_BUCKETS - 1)
    return jnp.where(dist < exact, dist, large)


def _dilated_bias(rel_bias):
    n = DIL_STEPS
    steps = jnp.arange(n + 1)
    vec = jnp.stack([rel_bias.astype(F32)[_t5_bucket(steps * dil)] for (_, dil) in DIL_PATTERNS])
    vec = vec.transpose(0, 2, 1)
    width = 3 * n
    p = jnp.full(vec.shape[:2] + (width,), NEG, F32)
    p = p.at[:, :, n - 1:2 * n].set(vec[:, :, ::-1])
    tiled = jnp.broadcast_to(p[:, :, None, :], vec.shape[:2] + (n, width))
    skew = tiled.reshape(vec.shape[:2] + (n * width,))[:, :, :n * (width - 1)]
    skew = skew.reshape(vec.shape[:2] + (n, width - 1))
    return skew[:, :, :, n - 1:3 * n - 1]


def dilated_attention(qkv1, qkv4, qkv16, rel_bias, tile, group):
    _, s, dd = qkv1.shape
    n = DIL_STEPS
    h_n = DIL_HEADS
    bias = _dilated_bias(rel_bias)
    args, specs = [bias], [pl.BlockSpec((len(DIL_PATTERNS), None, n, 2 * n), lambda h, i: (0, h, 0, 0))]
    for dil, view in ((1, qkv1), (4, qkv4), (16, qkv16)):
        rows = tile // dil
        per = rows // n

        def cur_spec(off, rows=rows, dil=dil):
            return pl.BlockSpec((None, rows, dil * dd), lambda h, i: (off + h, i, 0))

        def prev_spec(off, per=per, dil=dil):
            return pl.BlockSpec((None, n, dil * dd), lambda h, i: (off + h, jnp.maximum(i * per - 1, 0), 0))

        args += [view, view, view, view, view]
        specs += [cur_spec(0), cur_spec(h_n), prev_spec(h_n), cur_spec(2 * h_n), prev_spec(2 * h_n)]
    return pl.pallas_call(
        functools.partial(_dilated_kernel, tile=tile, group=group),
        grid=(h_n, s // tile),
        in_specs=specs,
        out_specs=pl.BlockSpec((tile, dd), lambda h, i: (i, h)),
        out_shape=jax.ShapeDtypeStruct((s, h_n * dd), BF16),
        scratch_shapes=[pltpu.VMEM((tile, dd), F32)] * 3,
        compiler_params=_params("parallel", "arbitrary"),
        name="dilated_attention",
    )(*args)


def _gdn_in_proj_kernel(x_ref, g_ref, w_ref, ws_ref, cw_ref, oc_ref, oz_ref, os_ref,
                        h_ref, buf, tails, *, n_norm, n_conv, q_tiles):
    i = pl.program_id(0)
    j = pl.program_id(1)
    tm = buf.shape[0] - 8
    tn = buf.shape[1]
    halo = 8
    sub = 128
    dd = GDN_DIM

    @pl.when((i == 0) & (j == 0))
    def _():
        tails[...] = jnp.zeros_like(tails)

    @pl.when(j == 0)
    def _():
        h = (_rms_rows(x_ref[...]) * g_ref[...]).astype(BF16)
        h_ref[...] = h
        os_ref[...] = _dot(h, ws_ref[...])

    def conv_tile(l2norm):
        buf[halo:halo + tm, :] = _dot(h_ref[...], w_ref[...])
        buf[0:halo, :] = tails[j]
        tails[j] = buf[tm:tm + halo, :]
        scale = jnp.where(j < q_tiles, dd ** -0.5, 1.0)
        for r0 in range(0, tm, sub):
            y = buf[halo + r0:halo + r0 + sub, :] * cw_ref[GDN_CONV - 1:GDN_CONV, :]
            for t in range(GDN_CONV - 1):
                off = halo + r0 - (GDN_CONV - 1) + t
                y = y + buf[off:off + sub, :] * cw_ref[t:t + 1, :]
            y = _silu(y)
            if l2norm:
                for h0 in range(0, tn, dd):
                    yh = y[:, h0:h0 + dd]
                    yh = yh * (lax.rsqrt(jnp.sum(yh * yh, axis=-1, keepdims=True) + EPS) * scale)
                    oc_ref[r0:r0 + sub, h0:h0 + dd] = yh.astype(oc_ref.dtype)
            else:
                oc_ref[r0:r0 + sub, :] = y.astype(oc_ref.dtype)

    @pl.when(j < n_norm)
    def _():
        conv_tile(True)

    @pl.when((j >= n_norm) & (j < n_conv))
    def _():
        conv_tile(False)

    @pl.when(j >= n_conv)
    def _():
        oz_ref[...] = _dot(h_ref[...], w_ref[...]).astype(oz_ref.dtype)


def gdn_in_proj(x, gain, w, w_side, conv_w, tm, tn):
    m, k = x.shape
    ns = w_side.shape[1]
    n_conv = GDN_QKV // tn
    n_norm = 2 * GDN_QK_W // tn
    n_tiles = (GDN_QKV + GDN_V_W) // tn
    kern = functools.partial(_gdn_in_proj_kernel, n_norm=n_norm, n_conv=n_conv, q_tiles=GDN_QK_W // tn)
    return pl.pallas_call(
        kern,
        grid=(m // tm, n_tiles),
        in_specs=[pl.BlockSpec((tm, k), lambda i, j: (i, 0)),
                  pl.BlockSpec((1, k), lambda i, j: (0, 0)),
                  pl.BlockSpec((None, k, tn), lambda i, j: (0, 0, j)),
                  pl.BlockSpec((k, ns), lambda i, j: (0, 0)),
                  pl.BlockSpec((GDN_CONV, tn), lambda i, j: (0, jnp.minimum(j, n_conv - 1)))],
        out_specs=[pl.BlockSpec((tm, tn), lambda i, j: (i, jnp.minimum(j, n_conv - 1))),
                   pl.BlockSpec((tm, tn), lambda i, j: (i, jnp.maximum(j - n_conv, 0))),
                   pl.BlockSpec((tm, ns), lambda i, j: (i, 0))],
        out_shape=[jax.ShapeDtypeStruct((m, GDN_QKV), BF16),
                   jax.ShapeDtypeStruct((m, GDN_V_W), BF16),
                   jax.ShapeDtypeStruct((m, ns), F32)],
        scratch_shapes=[pltpu.VMEM((tm, k), BF16),
                        pltpu.VMEM((8 + tm, tn), F32),
                        pltpu.VMEM((n_conv, 8, tn), F32)],
        compiler_params=_params("arbitrary", "arbitrary"),
        name="gdn_in_proj",
    )(x, gain.reshape(1, k), w, w_side, conv_w)


def _gdn_gates_kernel(ab_ref, alog_ref, dt_ref, beta_ref, gcum_ref, gct_ref, *, chunk):
    t = ab_ref.shape[0]
    hv = GDN_V_HEADS
    ab = ab_ref[...]
    beta_ref[...] = jax.nn.sigmoid(ab[:, :hv])
    z = ab[:, hv:] + dt_ref[...]
    softplus = jnp.maximum(z, 0.0) + jnp.log1p(jnp.exp(-jnp.abs(z)))
    g = -jnp.exp(alog_ref[...]) * softplus
    ii = lax.broadcasted_iota(jnp.int32, (t, t), 0)
    jj = lax.broadcasted_iota(jnp.int32, (t, t), 1)
    tri = ((ii >= jj) & (ii // chunk == jj // chunk)).astype(F32)
    gcum = _dot_f32(tri, g)
    gcum_ref[...] = gcum
    eh = lax.broadcasted_iota(jnp.int32, (hv, hv), 0) == lax.broadcasted_iota(jnp.int32, (hv, hv), 1)
    gct_ref[...] = _dot_nt_f32(eh.astype(F32), gcum)


def gdn_gates(ab, a_log, dt_bias, tm):
    s = ab.shape[0]
    hv = GDN_V_HEADS
    return pl.pallas_call(
        functools.partial(_gdn_gates_kernel, chunk=GDN_CHUNK),
        grid=(s // tm,),
        in_specs=[pl.BlockSpec((tm, 2 * hv), lambda i: (i, 0)),
                  pl.BlockSpec((1, hv), lambda i: (0, 0)),
                  pl.BlockSpec((1, hv), lambda i: (0, 0))],
        out_specs=[pl.BlockSpec((tm, hv), lambda i: (i, 0)),
                   pl.BlockSpec((tm, hv), lambda i: (i, 0)),
                   pl.BlockSpec((hv, tm), lambda i: (0, i))],
        out_shape=[jax.ShapeDtypeStruct((s, hv), F32), jax.ShapeDtypeStruct((s, hv), F32),
                   jax.ShapeDtypeStruct((hv, s), F32)],
        compiler_params=_params("parallel"),
        name="gdn_gates",
    )(ab, a_log.reshape(1, hv), dt_bias.reshape(1, hv))


def _unit_lower_inverses(a_list):
    c = a_list[0].shape[0]
    ii = lax.broadcasted_iota(jnp.int32, (c, c), 0)
    jj = lax.broadcasted_iota(jnp.int32, (c, c), 1)
    eye = jnp.where(ii == jj, 1.0, 0.0)
    p = [-a for a in a_list]
    t = [eye + x for x in p]
    pb = [x.astype(BF16) for x in p]
    p = [_dot(x, x) for x in pb]
    span = 4
    while span < c:
        pb = [x.astype(BF16) for x in p]
        st = [_dot(jnp.concatenate([x, y.astype(BF16)], axis=0), x) for x, y in zip(pb, t)]
        p = [x[:c] for x in st]
        t = [y + x[c:] for x, y in zip(st, t)]
        span *= 2
    pb = [x.astype(BF16) for x in p]
    return [y + _dot(y.astype(BF16), x) for x, y in zip(pb, t)]


def _gdn_kernel(q_ref, k_ref, v_ref, z_ref, beta_ref, gcum_ref, gct_ref, ng_ref, o_ref, state_ref,
                *, tb, chunk, kheads):
    grp = pl.program_id(0)
    c = chunk
    dd = GDN_DIM
    nh = 2 * kheads
    nc = tb // c

    @pl.when(pl.program_id(1) == 0)
    def _():
        state_ref[...] = jnp.zeros_like(state_ref)

    lane_h = lax.broadcasted_iota(jnp.int32, (1, GDN_V_HEADS), 1)
    ii = lax.broadcasted_iota(jnp.int32, (c, c), 0)
    jj = lax.broadcasted_iota(jnp.int32, (c, c), 1)
    causal = ii >= jj
    strict = ii > jj
    ng = ng_ref[...]

    def head_cols(h):
        return slice(h * dd, (h + 1) * dd)

    for ci in range(nc):
        r = slice(ci * c, (ci + 1) * c)
        beta_t = beta_ref[r, :]
        gcum_t = gcum_ref[r, :]
        beta_col, gc_col, gc_row = [], [], []
        for h in range(nh):
            hv = grp * nh + h
            sel = lane_h == hv
            beta_col.append(jnp.sum(jnp.where(sel, beta_t, 0.0), axis=-1, keepdims=True))
            gc_col.append(jnp.sum(jnp.where(sel, gcum_t, 0.0), axis=-1, keepdims=True))
            gc_row.append(gct_ref[pl.ds(hv, 1), :][:, r])
        eg_col = [jnp.exp(x) for x in gc_col]

        k16 = [k_ref[r, head_cols(kh)] for kh in range(kheads)]
        q16 = [q_ref[r, head_cols(kh)] for kh in range(kheads)]
        kf = [x.astype(F32) for x in k16]
        qkk = [_dot_nt(jnp.concatenate([q, k], axis=0), k) for q, k in zip(q16, k16)]

        decay, a_list = [], []
        for h in range(nh):
            dlog = gc_col[h] - gc_row[h]
            dec = jnp.where(causal, jnp.exp(jnp.where(causal, dlog, 0.0)), 0.0)
            decay.append(dec)
            a_list.append(jnp.where(strict, qkk[h // 2][c:] * beta_col[h] * dec, 0.0))
        t_list = _unit_lower_inverses(a_list)

        uw = []
        for h in range(nh):
            rhs = jnp.concatenate([v_ref[r, head_cols(h)].astype(F32) * beta_col[h],
                                   kf[h // 2] * (beta_col[h] * eg_col[h])], axis=1)
            uw.append(_dot(t_list[h].astype(BF16), rhs.astype(BF16)))

        states = [state_ref[h] for h in range(nh)]
        s16 = [s.astype(BF16) for s in states]
        wq = []
        for h in range(nh):
            lhs = jnp.concatenate([uw[h][:, dd:], q16[h // 2].astype(F32) * eg_col[h]], axis=0)
            wq.append(_dot(lhs.astype(BF16), s16[h]))
        vn16 = [(uw[h][:, :dd] - wq[h][:c]).astype(BF16) for h in range(nh)]
        outs = []
        for h in range(nh):
            intra = jnp.where(causal, qkk[h // 2][:c] * decay[h], 0.0)
            outs.append(wq[h][c:] + _dot(intra.astype(BF16), vn16[h]))
        for h in range(nh):
            g_last = gc_col[h][c - 1:c, :]
            kd = kf[h // 2] * jnp.exp(g_last - gc_col[h])
            state_ref[h] = states[h] * jnp.exp(g_last) + _dot_tn(kd.astype(BF16), vn16[h])
        for h in range(nh):
            zg = _silu(z_ref[r, head_cols(h)].astype(F32))
            o_ref[r, head_cols(h)] = (_rms_rows(outs[h]) * ng * zg).astype(o_ref.dtype)


def gdn_mixer(qkv, z, beta, gcum, gct, norm_gain, tb, kheads):
    s = qkv.shape[0]
    dd = GDN_DIM
    wk = kheads * dd
    wv = 2 * wk
    k0 = GDN_QK_W // wk
    v0 = 2 * GDN_QK_W // wv
    return pl.pallas_call(
        functools.partial(_gdn_kernel, tb=tb, chunk=GDN_CHUNK, kheads=kheads),
        grid=(GDN_K_HEADS // kheads, s // tb),
        in_specs=[pl.BlockSpec((tb, wk), lambda g, i: (i, g)),
                  pl.BlockSpec((tb, wk), lambda g, i: (i, k0 + g)),
                  pl.BlockSpec((tb, wv), lambda g, i: (i, v0 + g)),
                  pl.BlockSpec((tb, wv), lambda g, i: (i, g)),
                  pl.BlockSpec((tb, GDN_V_HEADS), lambda g, i: (i, 0)),
                  pl.BlockSpec((tb, GDN_V_HEADS), lambda g, i: (i, 0)),
                  pl.BlockSpec((GDN_V_HEADS, tb), lambda g, i: (0, i)),
                  pl.BlockSpec((1, dd), lambda g, i: (0, 0))],
        out_specs=pl.BlockSpec((tb, wv), lambda g, i: (i, g)),
        out_shape=jax.ShapeDtypeStruct((s, GDN_V_W), BF16),
        scratch_shapes=[pltpu.VMEM((2 * kheads, dd, dd), F32)],
        compiler_params=_params("parallel", "arbitrary"),
        name="gated_deltanet",
    )(qkv, qkv, qkv, z, beta, gcum, gct, norm_gain.reshape(1, dd))


def _xa_kv_kernel(mem_ref, mg_ref, w_ref, kg_ref, k_ref, v_ref):
    mem_n = (_rms_rows(mem_ref[...]) * mg_ref[...]).astype(BF16)
    kv = _dot(mem_n, w_ref[...])
    for h in range(XA_HEADS):
        k = kv[:, h * XA_DIM:(h + 1) * XA_DIM]
        k_ref[h] = (_rms_rows(k) * kg_ref[...]).astype(k_ref.dtype)
        v_ref[h] = kv[:, (XA_HEADS + h) * XA_DIM:(XA_HEADS + h + 1) * XA_DIM].astype(v_ref.dtype)


def xa_keys_values(mem, mem_gain, w_kv, layer, k_gain):
    m, d = mem.shape
    shape = jax.ShapeDtypeStruct((XA_HEADS, m, XA_DIM), BF16)
    nkv = w_kv.shape[2]
    return pl.pallas_call(
        _xa_kv_kernel,
        grid=(1,),
        in_specs=[pl.BlockSpec((m, d), lambda i: (0, 0)),
                  pl.BlockSpec((1, d), lambda i: (0, 0)),
                  pl.BlockSpec((None, d, nkv), lambda i: (layer, 0, 0)),
                  pl.BlockSpec((1, XA_DIM), lambda i: (0, 0))],
        out_specs=[pl.BlockSpec((XA_HEADS, m, XA_DIM), lambda i: (0, 0, 0))] * 2,
        out_shape=[shape, shape],
        compiler_params=_params("arbitrary"),
        name="xa_keys_values",
    )(mem, mem_gain.reshape(1, d), w_kv, k_gain.reshape(1, XA_DIM))


def _xattn_rows(x, g_ref, wq_ref, qg_ref, k_ref, v_ref, wo_ref):
    h = (_rms_rows(x) * g_ref[...]).astype(BF16)
    q = _dot(h, wq_ref[...])
    qg = qg_ref[...] * (XA_DIM ** -0.5)
    heads = range(XA_HEADS)
    qh = [(_rms_rows(q[:, hd * XA_DIM:(hd + 1) * XA_DIM]) * qg).astype(BF16) for hd in heads]
    sc = [_dot_nt(qh[hd], k_ref[hd]) for hd in heads]
    pr = [jnp.exp(s - jnp.max(s, axis=-1, keepdims=True)) for s in sc]
    pr = [p / jnp.sum(p, axis=-1, keepdims=True) for p in pr]
    outs = [_dot(pr[hd].astype(BF16), v_ref[hd]).astype(BF16) for hd in heads]
    return x + _dot(jnp.concatenate(outs, axis=-1), wo_ref[...])


def _outproj_xattn_kernel(*refs, n_y, n_col):
    y_refs, w_refs = refs[:n_y], refs[n_y:2 * n_y]
    x_ref, g_ref, wq_ref, qg_ref, k_ref, v_ref, wo_ref, o_ref, x1_ref = refs[2 * n_y:]
    j = pl.program_id(1)
    acc = x_ref[...]
    for y_ref, w_ref in zip(y_refs, w_refs):
        acc = acc + _dot(y_ref[...], w_ref[...])
    x1_ref[j] = acc

    @pl.when(j == n_col - 1)
    def _():
        x1 = jnp.concatenate([x1_ref[c] for c in range(n_col)], axis=1)
        o_ref[...] = _xattn_rows(x1, g_ref, wq_ref, qg_ref, k_ref, v_ref, wo_ref)


def outproj_cross_attention(ys, w_out, w_layer, x, gain, w_q, q_gain, k, v, w_o, layer, tm, tn):
    s, d = x.shape
    hw = XA_HEADS * XA_DIM
    m = k.shape[1]
    n_col = d // tn
    kb = ys[-1].shape[1]
    y_specs, w_specs, row0 = [], [], 0
    for y in ys:
        ky = y.shape[1]
        assert ky % kb == 0 and row0 % ky == 0
        y_specs.append(pl.BlockSpec((tm, ky), lambda i, j: (i, 0)))
        w_specs.append(pl.BlockSpec((None, ky, tn), lambda i, j, blk=row0 // ky: (w_layer, blk, j)))
        row0 += ky
    return pl.pallas_call(
        functools.partial(_outproj_xattn_kernel, n_y=len(ys), n_col=n_col),
        grid=(s // tm, n_col),
        in_specs=y_specs + w_specs + [
            pl.BlockSpec((tm, tn), lambda i, j: (i, j)),
            pl.BlockSpec((1, d), lambda i, j: (0, 0)),
            pl.BlockSpec((None, d, hw), lambda i, j: (layer, 0, 0)),
            pl.BlockSpec((1, XA_DIM), lambda i, j: (0, 0)),
            pl.BlockSpec((XA_HEADS, m, XA_DIM), lambda i, j: (0, 0, 0)),
            pl.BlockSpec((XA_HEADS, m, XA_DIM), lambda i, j: (0, 0, 0)),
            pl.BlockSpec((None, hw, d), lambda i, j: (layer, 0, 0))],
        out_specs=pl.BlockSpec((tm, d), lambda i, j: (i, 0)),
        out_shape=jax.ShapeDtypeStruct((s, d), F32),
        scratch_shapes=[pltpu.VMEM((n_col, tm, tn), F32)],
        compiler_params=_params("parallel", "arbitrary"),
        name="outproj_cross_attention",
    )(*ys, *([w_out] * len(ys)), x, gain.reshape(1, d), w_q, q_gain.reshape(1, XA_DIM), k, v, w_o)


def _pick_tile(total, target):
    t = min(total, target)
    while total % t:
        t //= 2
    return t


def _rotary_tables(s):
    inv = ROPE_BASE ** (-jnp.arange(0, RET_QK_DIM, 2, dtype=F32) / RET_QK_DIM)
    ang = jnp.arange(s).astype(F32)[:, None] * inv[None, :]
    return jnp.cos(ang), jnp.sin(ang)


def kernel(x, mem, norm_mix, norm_xa, norm_ffn, mem_norm, rel_bias, ar_w_in, ar_w_out, dil_q_gain, dil_k_gain, gdn_w_in, gdn_conv, gdn_a_log, gdn_dt_bias, gdn_norm, gdn_w_out, xa_w_q, xa_w_kv, xa_w_o, xa_q_gain, xa_k_gain, ffn_w1, ffn_w3, ffn_w2):
    b, s, d = x.shape
    assert b == 1
    xs = x.reshape(s, d)
    mem2 = mem.reshape(mem.shape[1], d)
    tm = _pick_tile(s, 1024)
    ar_w_in, ar_w_out, gdn_w_out, xa_w_q, xa_w_kv, xa_w_o, ffn_w1, ffn_w3, ffn_w2 = (
        w.astype(BF16) for w in (ar_w_in, ar_w_out, gdn_w_out, xa_w_q, xa_w_kv, xa_w_o, ffn_w1, ffn_w3, ffn_w2))

    def xa_kv(layer):
        return xa_keys_values(mem2, mem_norm, xa_w_kv, layer, xa_k_gain[layer])

    def ffn(xs, layer):
        hid = norm_swiglu(xs, norm_ffn[layer], ffn_w1, ffn_w3, layer, tm, _pick_tile(ffn_w1.shape[-1], 512))
        return matmul_residual(hid, ffn_w2, layer, xs, tm, _pick_tile(d, 512))

    tx = _pick_tile(s, 512)

    proj, qkv1, qkv4, qkv16 = ar_in_proj(xs, norm_mix[0], ar_w_in, dil_q_gain[0], dil_k_gain[0], tm)
    cos, sin = _rotary_tables(s)
    ya = retention_mixer(proj, cos, sin, _pick_tile(s, 1024))
    yb = dilated_attention(qkv1, qkv4, qkv16, rel_bias, _pick_tile(s, 2048), 8)
    k, v = xa_kv(0)
    xs = outproj_cross_attention((ya, yb), ar_w_out, 0, xs, norm_xa[0], xa_w_q, xa_q_gain[0], k, v, xa_w_o,
                                 0, tx, _pick_tile(d, 1024))
    xs = ffn(xs, 0)

    n_main = GDN_QKV + GDN_V_W
    w_gate = gdn_w_in[0, :, n_main:].astype(BF16)
    qkv, z, ab = gdn_in_proj(xs, norm_mix[1], gdn_w_in.astype(BF16), w_gate, gdn_conv[0], tm, 1024)
    beta, gcum, gct = gdn_gates(ab, gdn_a_log[0], gdn_dt_bias[0], _pick_tile(s, 256))
    og = gdn_mixer(qkv, z, beta, gcum, gct, gdn_norm[0], 2 * GDN_CHUNK, 4)
    k, v = xa_kv(1)
    xs = outproj_cross_attention((og,), gdn_w_out, 0, xs, norm_xa[1], xa_w_q, xa_q_gain[1], k, v, xa_w_o,
                                 1, tx, _pick_tile(d, 1024))
    xs = ffn(xs, 1)
    return xs.reshape(b, s, d)
```

```python
import functools
import math

import jax
import jax.numpy as jnp
from jax import lax
from jax.experimental import pallas as pl
from jax.experimental.pallas import tpu as pltpu

F32 = jnp.float32
BF16 = jnp.bfloat16
EPS = 1e-6
NEG = -1e30

VMEM_LIMIT_BYTES = 56 * 1024 * 1024
LANES = 128

RET_HEADS = 4
RET_QK_DIM = 256
RET_V_DIM = 512
RET_CHUNK = 128
ROPE_BASE = 10000.0
DIL_HEADS = 8
DIL_DIM = 128
DIL_PATTERNS = ((128, 1), (512, 4), (2048, 16))
DIL_STEPS = 128
REL_BUCKETS = 32
REL_MAX_DIST = 2048
AR_QK = RET_HEADS * RET_QK_DIM
AR_V = RET_HEADS * RET_V_DIM
DIL_W = DIL_HEADS * DIL_DIM
GDN_K_HEADS = 16
GDN_V_HEADS = 32
GDN_DIM = 128
GDN_CONV = 4
GDN_CHUNK = 128
GDN_QK_W = GDN_K_HEADS * GDN_DIM
GDN_V_W = GDN_V_HEADS * GDN_DIM
GDN_QKV = 2 * GDN_QK_W + GDN_V_W
XA_HEADS = 4
XA_DIM = 128


def _params(*semantics):
    return pltpu.CompilerParams(dimension_semantics=semantics, vmem_limit_bytes=VMEM_LIMIT_BYTES)


def _dot(a, b):
    return jnp.dot(a, b, preferred_element_type=F32)


def _dot_nt(a, b):
    return lax.dot_general(a, b, (((1,), (1,)), ((), ())), preferred_element_type=F32)


def _dot_tn(a, b):
    return lax.dot_general(a, b, (((0,), (0,)), ((), ())), preferred_element_type=F32)


def _dot_f32(a, b):
    return jnp.dot(a, b, preferred_element_type=F32, precision=lax.Precision.HIGHEST)


def _dot_nt_f32(a, b):
    return lax.dot_general(a, b, (((1,), (1,)), ((), ())), preferred_element_type=F32,
                           precision=lax.Precision.HIGHEST)


def _rms_rows(x):
    return x * lax.rsqrt(jnp.mean(x * x, axis=-1, keepdims=True) + EPS)


def _silu(x):
    return x * jax.nn.sigmoid(x)


def _norm_swiglu_kernel(x_ref, g_ref, w1_ref, w3_ref, o_ref, h_ref):
    @pl.when(pl.program_id(1) == 0)
    def _():
        h_ref[...] = (_rms_rows(x_ref[...]) * g_ref[...]).astype(BF16)

    h = h_ref[...]
    a = _dot(h, w1_ref[...].astype(BF16))
    b = _dot(h, w3_ref[...].astype(BF16))
    o_ref[...] = (_silu(a) * b).astype(o_ref.dtype)


def norm_swiglu(x, gain, w1, w3, layer, tm, tn):
    m, k = x.shape
    n = w1.shape[2]
    return pl.pallas_call(
        _norm_swiglu_kernel,
        grid=(m // tm, n // tn),
        in_specs=[pl.BlockSpec((tm, k), lambda i, j: (i, 0)),
                  pl.BlockSpec((1, k), lambda i, j: (0, 0)),
                  pl.BlockSpec((None, k, tn), lambda i, j: (layer, 0, j)),
                  pl.BlockSpec((None, k, tn), lambda i, j: (layer, 0, j))],
        out_specs=pl.BlockSpec((tm, tn), lambda i, j: (i, j)),
        out_shape=jax.ShapeDtypeStruct((m, n), BF16),
        scratch_shapes=[pltpu.VMEM((tm, k), BF16)],
        compiler_params=_params("parallel", "arbitrary"),
        name="norm_swiglu",
    )(x, gain.reshape(1, k), w1, w3)


def _matmul_residual_kernel(y_ref, w_ref, x_ref, o_ref):
    o_ref[...] = x_ref[...] + _dot(y_ref[...], w_ref[...])


def matmul_residual(y, w, layer, x, tm, tn):
    m, k = y.shape
    n = w.shape[2]
    return pl.pallas_call(
        _matmul_residual_kernel,
        grid=(m // tm, n // tn),
        in_specs=[pl.BlockSpec((tm, k), lambda i, j: (i, 0)),
                  pl.BlockSpec((None, k, tn), lambda i, j: (layer, 0, j)),
                  pl.BlockSpec((tm, tn), lambda i, j: (i, j))],
        out_specs=pl.BlockSpec((tm, tn), lambda i, j: (i, j)),
        out_shape=jax.ShapeDtypeStruct((m, n), F32),
        compiler_params=_params("parallel", "parallel"),
        name="matmul_residual",
    )(y, w, x)


def _matmul2_residual_kernel(ya_ref, yb_ref, wa_ref, wb_ref, x_ref, o_ref):
    o_ref[...] = x_ref[...] + _dot(ya_ref[...], wa_ref[...]) + _dot(yb_ref[...], wb_ref[...])


def matmul2_residual(ya, yb, w, layer, x, tm, tn):
    m, ka = ya.shape
    kb = yb.shape[1]
    n = w.shape[2]
    assert ka % kb == 0
    return pl.pallas_call(
        _matmul2_residual_kernel,
        grid=(m // tm, n // tn),
        in_specs=[pl.BlockSpec((tm, ka), lambda i, j: (i, 0)),
                  pl.BlockSpec((tm, kb), lambda i, j: (i, 0)),
                  pl.BlockSpec((None, ka, tn), lambda i, j: (layer, 0, j)),
                  pl.BlockSpec((None, kb, tn), lambda i, j: (layer, ka // kb, j)),
                  pl.BlockSpec((tm, tn), lambda i, j: (i, j))],
        out_specs=pl.BlockSpec((tm, tn), lambda i, j: (i, j)),
        out_shape=jax.ShapeDtypeStruct((m, n), F32),
        compiler_params=_params("parallel", "parallel"),
        name="matmul2_residual",
    )(ya, yb, w, w, x)


def _retention_kernel(lg_ref, q_ref, k_ref, v_ref, g_ref, cos_ref, sin_ref, o_ref, state_ref,
                      *, chunk, n_chunks):
    @pl.when(pl.program_id(1) == 0)
    def _():
        state_ref[...] = jnp.zeros_like(state_ref)

    c = chunk
    half = RET_QK_DIM // 2
    lg = lg_ref[0]
    lg_l = lg[:, :LANES]
    ii = lax.broadcasted_iota(jnp.int32, (c, c), 0)
    jj = lax.broadcasted_iota(jnp.int32, (c, c), 1)
    rel = (ii - jj).astype(F32)
    inner = jnp.where(rel >= 0, jnp.exp(lg[:, :c] * jnp.maximum(rel, 0.0)), 0.0)
    idx = lax.broadcasted_iota(jnp.int32, (c, LANES), 0).astype(F32)
    q_dec = jnp.exp(lg_l * (idx + 1.0))
    k_dec = jnp.exp(lg_l * (c - 1.0 - idx))
    c_dec = jnp.exp(lg * float(c))
    k_scale = RET_QK_DIM ** -0.5

    def rot(t, cos, sin):
        t1, t2 = t[:, :half], t[:, half:]
        return t1 * cos - t2 * sin, t1 * sin + t2 * cos

    rows = [pl.ds(ci * c, c) for ci in range(n_chunks)]
    qb, kb, qd, kd = [], [], [], []
    for r in rows:
        cos = cos_ref[r, :]
        sin = sin_ref[r, :]
        q1, q2 = rot(q_ref[r, :].astype(F32), cos, sin)
        k1, k2 = rot(k_ref[r, :].astype(F32) * k_scale, cos, sin)
        qb.append(jnp.concatenate([q1, q2], axis=-1).astype(BF16))
        kb.append(jnp.concatenate([k1, k2], axis=-1).astype(BF16))
        qd.append(jnp.concatenate([q1 * q_dec, q2 * q_dec], axis=-1).astype(BF16))
        kd.append(jnp.concatenate([k1 * k_dec, k2 * k_dec], axis=-1).astype(BF16))
    scores = [(_dot_nt(a, b) * inner).astype(BF16) for a, b in zip(qb, kb)]
    kv = [_dot_tn(a, v_ref[r, :]) for a, r in zip(kd, rows)]
    states = [state_ref[...]]
    for inc in kv:
        states.append(states[-1] * c_dec + inc)
    state_ref[...] = states[-1]
    outs = [_dot(sc, v_ref[r, :]) + _dot(a, st.astype(BF16))
            for sc, r, a, st in zip(scores, rows, qd, states)]
    for o, r in zip(outs, rows):
        gate = _silu(g_ref[r, :].astype(F32))
        o_ref[r, :] = (_rms_rows(o) * gate).astype(o_ref.dtype)


def retention_mixer(proj, cos, sin, rows_per_step):
    s = proj.shape[0]
    t = rows_per_step
    log_gamma = jnp.log(1.0 - 2.0 ** (-5.0 - jnp.arange(RET_HEADS, dtype=F32)))
    lg = jnp.broadcast_to(log_gamma[:, None, None], (RET_HEADS, 1, RET_V_DIM))
    qk_blocks = AR_QK // RET_QK_DIM
    v_off = 2 * AR_QK // RET_V_DIM
    g_off = v_off + AR_V // RET_V_DIM
    kern = functools.partial(_retention_kernel, chunk=RET_CHUNK, n_chunks=t // RET_CHUNK)
    return pl.pallas_call(
        kern,
        grid=(RET_HEADS, s // t),
        in_specs=[pl.BlockSpec((1, 1, RET_V_DIM), lambda h, i: (h, 0, 0)),
                  pl.BlockSpec((t, RET_QK_DIM), lambda h, i: (i, h)),
                  pl.BlockSpec((t, RET_QK_DIM), lambda h, i: (i, qk_blocks + h)),
                  pl.BlockSpec((t, RET_V_DIM), lambda h, i: (i, v_off + h)),
                  pl.BlockSpec((t, RET_V_DIM), lambda h, i: (i, g_off + h)),
                  pl.BlockSpec((t, RET_QK_DIM // 2), lambda h, i: (i, 0)),
                  pl.BlockSpec((t, RET_QK_DIM // 2), lambda h, i: (i, 0))],
        out_specs=pl.BlockSpec((t, RET_V_DIM), lambda h, i: (i, h)),
        out_shape=jax.ShapeDtypeStruct((s, AR_V), BF16),
        scratch_shapes=[pltpu.VMEM((RET_QK_DIM, RET_V_DIM), F32)],
        compiler_params=_params("parallel", "arbitrary"),
        name="retention",
    )(lg, proj, proj, proj, proj, cos, sin)


def _ar_in_proj_kernel(x_ref, g_ref, w_ref, gn_ref, oa_ref, o1_ref, o4_ref, o16_ref,
                       h_ref, buf, nb, *, n_a):
    j = pl.program_id(1)
    tm, tn = buf.shape
    dd = DIL_DIM

    @pl.when(j == 0)
    def _():
        h_ref[...] = (_rms_rows(x_ref[...]) * g_ref[...]).astype(BF16)

    @pl.when(j < n_a)
    def _():
        oa_ref[...] = _dot(h_ref[...], w_ref[...]).astype(oa_ref.dtype)

    def head_major(normed):
        buf[...] = _dot(h_ref[...], w_ref[...])
        for hd in range(tn // dd):
            slot = hd % 2
            x = buf[:, hd * dd:(hd + 1) * dd]
            nb[slot] = _rms_rows(x) * gn_ref[...] if normed else x
            o1_ref[hd] = nb[slot].astype(o1_ref.dtype)
            for r in range(4):
                o4_ref[hd, :, r * dd:(r + 1) * dd] = nb[slot, pl.ds(r, tm // 4, stride=4), :].astype(o4_ref.dtype)
            for r in range(16):
                o16_ref[hd, :, r * dd:(r + 1) * dd] = nb[slot, pl.ds(r, tm // 16, stride=16), :].astype(o16_ref.dtype)

    @pl.when((j >= n_a) & (j < n_a + 2))
    def _():
        head_major(True)

    @pl.when(j == n_a + 2)
    def _():
        head_major(False)


def ar_in_proj(x, gain, w, q_gain, k_gain, tm):
    m, k = x.shape
    dd = DIL_DIM
    tn = DIL_W
    n_a = (2 * AR_QK + 2 * AR_V) // tn
    nh = 3 * DIL_HEADS
    gains = jnp.stack([q_gain * (dd ** -0.5), k_gain, jnp.ones_like(k_gain)]).reshape(3, 1, dd)

    def grp(j):
        return jnp.clip(j - n_a, 0, 2)

    return pl.pallas_call(
        functools.partial(_ar_in_proj_kernel, n_a=n_a),
        grid=(m // tm, n_a + 3),
        in_specs=[pl.BlockSpec((tm, k), lambda i, j: (i, 0)),
                  pl.BlockSpec((1, k), lambda i, j: (0, 0)),
                  pl.BlockSpec((None, k, tn), lambda i, j: (0, 0, j)),
                  pl.BlockSpec((None, 1, dd), lambda i, j: (grp(j), 0, 0))],
        out_specs=[pl.BlockSpec((tm, tn), lambda i, j: (i, jnp.minimum(j, n_a - 1)))]
        + [pl.BlockSpec((DIL_HEADS, tm // d, d * dd), lambda i, j: (grp(j), i, 0)) for d in (1, 4, 16)],
        out_shape=[jax.ShapeDtypeStruct((m, n_a * tn), BF16)]
        + [jax.ShapeDtypeStruct((nh, m // d, d * dd), BF16) for d in (1, 4, 16)],
        scratch_shapes=[pltpu.VMEM((tm, k), BF16),
                        pltpu.VMEM((tm, tn), F32),
                        pltpu.VMEM((2, tm, dd), F32)],
        compiler_params=_params("arbitrary", "arbitrary"),
        name="ar_in_proj",
    )(x, gain.reshape(1, k), w, gains)


def _dilated_kernel(bias_ref,
                    q1, k1, k1p, v1, v1p,
                    q4, k4, k4p, v4, v4p,
                    q16, k16, k16p, v16, v16p,
                    o_ref, acc_ref, m_ref, l_ref, *, tile, group):
    n = DIL_STEPS
    d_dim = DIL_DIM
    first = pl.program_id(1) == 0

    def run_group(bias_idx, blocks, mode):
        def ld(p):
            return p[0][p[1]]

        bias_p = bias_ref[bias_idx, :, :n]
        bias_c = bias_ref[bias_idx, :, n:]
        sp = [_dot_nt(ld(b[0]), ld(b[1])) + bias_p for b in blocks]
        sc = [_dot_nt(ld(b[0]), ld(b[2])) + bias_c for b in blocks]
        sp = [jnp.where(first, NEG, x) if b[5] else x for x, b in zip(sp, blocks)]
        m = [jnp.max(jnp.maximum(x, y), axis=-1, keepdims=True) for x, y in zip(sp, sc)]
        pp = [jnp.exp(x - mm) for x, mm in zip(sp, m)]
        pc = [jnp.exp(x - mm) for x, mm in zip(sc, m)]
        l = [jnp.sum(x + y, axis=-1, keepdims=True) for x, y in zip(pp, pc)]
        o = [_dot(x.astype(BF16), ld(b[3])) + _dot(y.astype(BF16), ld(b[4]))
             for x, y, b in zip(pp, pc, blocks)]
        for b, mm, ll, oo in zip(blocks, m, l, o):
            rows = b[6]
            if mode == "init":
                m_ref[rows, :] = jnp.broadcast_to(mm, (n, d_dim))
                l_ref[rows, :] = jnp.broadcast_to(ll, (n, d_dim))
                acc_ref[rows, :] = oo
                continue
            m_old = m_ref[rows, :]
            m_new = jnp.maximum(m_old, mm)
            a_old = jnp.exp(m_old - m_new)
            a_cur = jnp.exp(mm - m_new)
            l_new = l_ref[rows, :] * a_old + ll * a_cur
            acc_new = acc_ref[rows, :] * a_old + oo * a_cur
            if mode == "merge":
                m_ref[rows, :] = m_new
                l_ref[rows, :] = l_new
                acc_ref[rows, :] = acc_new
            else:
                o_ref[rows, :] = (acc_new / l_new).astype(o_ref.dtype)

    every = slice(None)

    blocks = []
    for r in range(16):
        ls = slice(r * d_dim, (r + 1) * d_dim)
        blocks.append(((q16, (every, ls)), (k16p, (every, ls)), (k16, (every, ls)),
                       (v16p, (every, ls)), (v16, (every, ls)), True, pl.ds(r, n, stride=16)))
    for g0 in range(0, len(blocks), group):
        run_group(2, blocks[g0:g0 + group], "init")

    blocks = []
    for r in range(4):
        ls = slice(r * d_dim, (r + 1) * d_dim)
        for b in range(tile // (4 * n)):
            cur = pl.ds(b * n, n)
            prev = pl.ds((b - 1) * n, n)
            kp, vp = (((k4p, (every, ls)), (v4p, (every, ls))) if b == 0
                      else ((k4, (prev, ls)), (v4, (prev, ls))))
            blocks.append(((q4, (cur, ls)), kp, (k4, (cur, ls)), vp, (v4, (cur, ls)), b == 0,
                           pl.ds(b * 4 * n + r, n, stride=4)))
    for g0 in range(0, len(blocks), group):
        run_group(1, blocks[g0:g0 + group], "merge")

    blocks = []
    for b in range(tile // n):
        cur = pl.ds(b * n, n)
        prev = pl.ds((b - 1) * n, n)
        kp, vp = (((k1p, (every, every)), (v1p, (every, every))) if b == 0
                  else ((k1, (prev, every)), (v1, (prev, every))))
        blocks.append(((q1, (cur, every)), kp, (k1, (cur, every)), vp, (v1, (cur, every)), b == 0, cur))
    for g0 in range(0, len(blocks), group):
        run_group(0, blocks[g0:g0 + group], "final")


def _t5_bucket(dist):
    exact = REL_BUCKETS // 2
    large = exact + (jnp.log(jnp.maximum(dist, exact).astype(F32) / exact)
                     / math.log(REL_MAX_DIST / exact) * (REL_BUCKETS - exact)).astype(jnp.int32)
    large = jnp.minimum(large, REL_BUCKETS - 1)
    return jnp.where(dist < exact, dist, large)


def _dilated_bias(rel_bias):
    n = DIL_STEPS
    steps = jnp.arange(n + 1)
    vec = jnp.stack([rel_bias.astype(F32)[_t5_bucket(steps * dil)] for (_, dil) in DIL_PATTERNS])
    vec = vec.transpose(0, 2, 1)
    width = 3 * n
    p = jnp.full(vec.shape[:2] + (width,), NEG, F32)
    p = p.at[:, :, n - 1:2 * n].set(vec[:, :, ::-1])
    tiled = jnp.broadcast_to(p[:, :, None, :], vec.shape[:2] + (n, width))
    skew = tiled.reshape(vec.shape[:2] + (n * width,))[:, :, :n * (width - 1)]
    skew = skew.reshape(vec.shape[:2] + (n, width - 1))
    return skew[:, :, :, n - 1:3 * n - 1]


def dilated_attention(qkv1, qkv4, qkv16, rel_bias, tile, group):
    _, s, dd = qkv1.shape
    n = DIL_STEPS
    h_n = DIL_HEADS
    bias = _dilated_bias(rel_bias)
    args, specs = [bias], [pl.BlockSpec((len(DIL_PATTERNS), None, n, 2 * n), lambda h, i: (0, h, 0, 0))]
    for dil, view in ((1, qkv1), (4, qkv4), (16, qkv16)):
        rows = tile // dil
        per = rows // n

        def cur_spec(off, rows=rows, dil=dil):
            return pl.BlockSpec((None, rows, dil * dd), lambda h, i: (off + h, i, 0))

        def prev_spec(off, per=per, dil=dil):
            return pl.BlockSpec((None, n, dil * dd), lambda h, i: (off + h, jnp.maximum(i * per - 1, 0), 0))

        args += [view, view, view, view, view]
        specs += [cur_spec(0), cur_spec(h_n), prev_spec(h_n), cur_spec(2 * h_n), prev_spec(2 * h_n)]
    return pl.pallas_call(
        functools.partial(_dilated_kernel, tile=tile, group=group),
        grid=(h_n, s // tile),
        in_specs=specs,
        out_specs=pl.BlockSpec((tile, dd), lambda h, i: (i, h)),
        out_shape=jax.ShapeDtypeStruct((s, h_n * dd), BF16),
        scratch_shapes=[pltpu.VMEM((tile, dd), F32)] * 3,
        compiler_params=_params("parallel", "arbitrary"),
        name="dilated_attention",
    )(*args)


def _gdn_in_proj_kernel(x_ref, g_ref, w_ref, ws_ref, cw_ref, oc_ref, oz_ref, os_ref,
                        h_ref, buf, tails, *, n_norm, n_conv, q_tiles):
    i = pl.program_id(0)
    j = pl.program_id(1)
    tm = buf.shape[0] - 8
    tn = buf.shape[1]
    halo = 8
    sub = 128
    dd = GDN_DIM

    @pl.when((i == 0) & (j == 0))
    def _():
        tails[...] = jnp.zeros_like(tails)

    @pl.when(j == 0)
    def _():
        h = (_rms_rows(x_ref[...]) * g_ref[...]).astype(BF16)
        h_ref[...] = h
        os_ref[...] = _dot(h, ws_ref[...])

    def conv_tile(l2norm):
        buf[halo:halo + tm, :] = _dot(h_ref[...], w_ref[...].astype(BF16))
        buf[0:halo, :] = tails[j]
        tails[j] = buf[tm:tm + halo, :]
        scale = jnp.where(j < q_tiles, dd ** -0.5, 1.0)
        for r0 in range(0, tm, sub):
            y = buf[halo + r0:halo + r0 + sub, :] * cw_ref[GDN_CONV - 1:GDN_CONV, :]
            for t in range(GDN_CONV - 1):
                off = halo + r0 - (GDN_CONV - 1) + t
                y = y + buf[off:off + sub, :] * cw_ref[t:t + 1, :]
            y = _silu(y)
            if l2norm:
                for h0 in range(0, tn, dd):
                    yh = y[:, h0:h0 + dd]
                    yh = yh * (lax.rsqrt(jnp.sum(yh * yh, axis=-1, keepdims=True) + EPS) * scale)
                    oc_ref[r0:r0 + sub, h0:h0 + dd] = yh.astype(oc_ref.dtype)
            else:
                oc_ref[r0:r0 + sub, :] = y.astype(oc_ref.dtype)

    @pl.when(j < n_norm)
    def _():
        conv_tile(True)

    @pl.when((j >= n_norm) & (j < n_conv))
    def _():
        conv_tile(False)

    @pl.when(j >= n_conv)
    def _():
        oz_ref[...] = _dot(h_ref[...], w_ref[...].astype(BF16)).astype(oz_ref.dtype)


def gdn_in_proj(x, gain, w, w_side, conv_w, tm, tn):
    m, k = x.shape
    ns = w_side.shape[1]
    n_conv = GDN_QKV // tn
    n_norm = 2 * GDN_QK_W // tn
    n_tiles = (GDN_QKV + GDN_V_W) // tn
    kern = functools.partial(_gdn_in_proj_kernel, n_norm=n_norm, n_conv=n_conv, q_tiles=GDN_QK_W // tn)
    return pl.pallas_call(
        kern,
        grid=(m // tm, n_tiles),
        in_specs=[pl.BlockSpec((tm, k), lambda i, j: (i, 0)),
                  pl.BlockSpec((1, k), lambda i, j: (0, 0)),
                  pl.BlockSpec((None, k, tn), lambda i, j: (0, 0, j)),
                  pl.BlockSpec((k, ns), lambda i, j: (0, 0)),
                  pl.BlockSpec((GDN_CONV, tn), lambda i, j: (0, jnp.minimum(j, n_conv - 1)))],
        out_specs=[pl.BlockSpec((tm, tn), lambda i, j: (i, jnp.minimum(j, n_conv - 1))),
                   pl.BlockSpec((tm, tn), lambda i, j: (i, jnp.maximum(j - n_conv, 0))),
                   pl.BlockSpec((tm, ns), lambda i, j: (i, 0))],
        out_shape=[jax.ShapeDtypeStruct((m, GDN_QKV), BF16),
                   jax.ShapeDtypeStruct((m, GDN_V_W), BF16),
                   jax.ShapeDtypeStruct((m, ns), F32)],
        scratch_shapes=[pltpu.VMEM((tm, k), BF16),
                        pltpu.VMEM((8 + tm, tn), F32),
                        pltpu.VMEM((n_conv, 8, tn), F32)],
        compiler_params=_params("arbitrary", "arbitrary"),
        name="gdn_in_proj",
    )(x, gain.reshape(1, k), w, w_side, conv_w)


def _gdn_gates_kernel(ab_ref, alog_ref, dt_ref, beta_ref, gcum_ref, gct_ref, *, chunk):
    t = ab_ref.shape[0]
    hv = GDN_V_HEADS
    ab = ab_ref[...]
    beta_ref[...] = jax.nn.sigmoid(ab[:, :hv])
    z = ab[:, hv:] + dt_ref[...]
    softplus = jnp.maximum(z, 0.0) + jnp.log1p(jnp.exp(-jnp.abs(z)))
    g = -jnp.exp(alog_ref[...]) * softplus
    ii = lax.broadcasted_iota(jnp.int32, (chunk, chunk), 0)
    jj = lax.broadcasted_iota(jnp.int32, (chunk, chunk), 1)
    tri = (ii >= jj).astype(F32)
    eh = (lax.broadcasted_iota(jnp.int32, (hv, hv), 0) == lax.broadcasted_iota(jnp.int32, (hv, hv), 1)).astype(F32)
    gcs = [_dot_f32(tri, g[r0:r0 + chunk]) for r0 in range(0, t, chunk)]
    gts = [_dot_nt_f32(eh, gc) for gc in gcs]
    for n, (gc, gt) in enumerate(zip(gcs, gts)):
        gcum_ref[n * chunk:(n + 1) * chunk, :] = gc
        gct_ref[:, n * chunk:(n + 1) * chunk] = gt


def gdn_gates(ab, a_log, dt_bias, tm):
    s = ab.shape[0]
    hv = GDN_V_HEADS
    return pl.pallas_call(
        functools.partial(_gdn_gates_kernel, chunk=GDN_CHUNK),
        grid=(s // tm,),
        in_specs=[pl.BlockSpec((tm, 2 * hv), lambda i: (i, 0)),
                  pl.BlockSpec((1, hv), lambda i: (0, 0)),
                  pl.BlockSpec((1, hv), lambda i: (0, 0))],
        out_specs=[pl.BlockSpec((tm, hv), lambda i: (i, 0)),
                   pl.BlockSpec((tm, hv), lambda i: (i, 0)),
                   pl.BlockSpec((hv, tm), lambda i: (0, i))],
        out_shape=[jax.ShapeDtypeStruct((s, hv), F32), jax.ShapeDtypeStruct((s, hv), F32),
                   jax.ShapeDtypeStruct((hv, s), F32)],
        compiler_params=_params("parallel"),
        name="gdn_gates",
    )(ab, a_log.reshape(1, hv), dt_bias.reshape(1, hv))


def _unit_lower_inverses(a_list):
    c = a_list[0].shape[0]
    ii = lax.broadcasted_iota(jnp.int32, (c, c), 0)
    jj = lax.broadcasted_iota(jnp.int32, (c, c), 1)
    eye = jnp.where(ii == jj, 1.0, 0.0)
    p = [-a for a in a_list]
    t = [eye + x for x in p]
    pb = [x.astype(BF16) for x in p]
    p = [_dot(x, x) for x in pb]
    span = 4
    while span < c:
        pb = [x.astype(BF16) for x in p]
        st = [_dot(jnp.concatenate([x, y.astype(BF16)], axis=0), x) for x, y in zip(pb, t)]
        p = [x[:c] for x in st]
        t = [y + x[c:] for x, y in zip(st, t)]
        span *= 2
    pb = [x.astype(BF16) for x in p]
    return [y + _dot(y.astype(BF16), x) for x, y in zip(pb, t)]


def _gdn_kernel(q_ref, k_ref, v_ref, z_ref, beta_ref, gcum_ref, gct_ref, ng_ref, o_ref, state_ref,
                *, tb, chunk, kheads):
    grp = pl.program_id(0)
    c = chunk
    dd = GDN_DIM
    nh = 2 * kheads
    nc = tb // c

    @pl.when(pl.program_id(1) == 0)
    def _():
        state_ref[...] = jnp.zeros_like(state_ref)

    lane_h = lax.broadcasted_iota(jnp.int32, (1, GDN_V_HEADS), 1)
    ii = lax.broadcasted_iota(jnp.int32, (c, c), 0)
    jj = lax.broadcasted_iota(jnp.int32, (c, c), 1)
    causal = ii >= jj
    strict = ii > jj
    ng = ng_ref[...]

    def head_cols(h):
        return slice(h * dd, (h + 1) * dd)

    for ci in range(nc):
        r = slice(ci * c, (ci + 1) * c)
        beta_t = beta_ref[r, :]
        gcum_t = gcum_ref[r, :]
        beta_col, gc_col, gc_row = [], [], []
        for h in range(nh):
            hv = grp * nh + h
            sel = lane_h == hv
            beta_col.append(jnp.sum(jnp.where(sel, beta_t, 0.0), axis=-1, keepdims=True))
            gc_col.append(jnp.sum(jnp.where(sel, gcum_t, 0.0), axis=-1, keepdims=True))
            gc_row.append(gct_ref[pl.ds(hv, 1), :][:, r])
        eg_col = [jnp.exp(x) for x in gc_col]

        k16 = [k_ref[r, head_cols(kh)] for kh in range(kheads)]
        q16 = [q_ref[r, head_cols(kh)] for kh in range(kheads)]
        kf = [x.astype(F32) for x in k16]
        qkk = [_dot_nt(jnp.concatenate([q, k], axis=0), k) for q, k in zip(q16, k16)]

        decay, a_list = [], []
        for h in range(nh):
            dlog = gc_col[h] - gc_row[h]
            dec = jnp.where(causal, jnp.exp(jnp.where(causal, dlog, 0.0)), 0.0)
            decay.append(dec)
            a_list.append(jnp.where(strict, qkk[h // 2][c:] * beta_col[h] * dec, 0.0))
        t_list = _unit_lower_inverses(a_list)

        uw = []
        for h in range(nh):
            rhs = jnp.concatenate([v_ref[r, head_cols(h)].astype(F32) * beta_col[h],
                                   kf[h // 2] * (beta_col[h] * eg_col[h])], axis=1)
            uw.append(_dot(t_list[h].astype(BF16), rhs.astype(BF16)))

        states = [state_ref[h] for h in range(nh)]
        s16 = [s.astype(BF16) for s in states]
        wq = []
        for h in range(nh):
            lhs = jnp.concatenate([uw[h][:, dd:], q16[h // 2].astype(F32) * eg_col[h]], axis=0)
            wq.append(_dot(lhs.astype(BF16), s16[h]))
        vn16 = [(uw[h][:, :dd] - wq[h][:c]).astype(BF16) for h in range(nh)]
        outs = []
        for h in range(nh):
            intra = jnp.where(causal, qkk[h // 2][:c] * decay[h], 0.0)
            outs.append(wq[h][c:] + _dot(intra.astype(BF16), vn16[h]))
        for h in range(nh):
            g_last = gc_col[h][c - 1:c, :]
            kd = kf[h // 2] * jnp.exp(g_last - gc_col[h])
            state_ref[h] = states[h] * jnp.exp(g_last) + _dot_tn(kd.astype(BF16), vn16[h])
        for h in range(nh):
            zg = _silu(z_ref[r, head_cols(h)].astype(F32))
            o_ref[r, head_cols(h)] = (_rms_rows(outs[h]) * ng * zg).astype(o_ref.dtype)


def gdn_mixer(qkv, z, beta, gcum, gct, norm_gain, tb, kheads):
    s = qkv.shape[0]
    dd = GDN_DIM
    wk = kheads * dd
    wv = 2 * wk
    k0 = GDN_QK_W // wk
    v0 = 2 * GDN_QK_W // wv
    return pl.pallas_call(
        functools.partial(_gdn_kernel, tb=tb, chunk=GDN_CHUNK, kheads=kheads),
        grid=(GDN_K_HEADS // kheads, s // tb),
        in_specs=[pl.BlockSpec((tb, wk), lambda g, i: (i, g)),
                  pl.BlockSpec((tb, wk), lambda g, i: (i, k0 + g)),
                  pl.BlockSpec((tb, wv), lambda g, i: (i, v0 + g)),
                  pl.BlockSpec((tb, wv), lambda g, i: (i, g)),
                  pl.BlockSpec((tb, GDN_V_HEADS), lambda g, i: (i, 0)),
                  pl.BlockSpec((tb, GDN_V_HEADS), lambda g, i: (i, 0)),
                  pl.BlockSpec((GDN_V_HEADS, tb), lambda g, i: (0, i)),
                  pl.BlockSpec((1, dd), lambda g, i: (0, 0))],
        out_specs=pl.BlockSpec((tb, wv), lambda g, i: (i, g)),
        out_shape=jax.ShapeDtypeStruct((s, GDN_V_W), BF16),
        scratch_shapes=[pltpu.VMEM((2 * kheads, dd, dd), F32)],
        compiler_params=_params("parallel", "arbitrary"),
        name="gated_deltanet",
    )(qkv, qkv, qkv, z, beta, gcum, gct, norm_gain.reshape(1, dd))


def _xa_kv_kernel(mem_ref, mg_ref, w_ref, kg_ref, k_ref, v_ref):
    mem_n = (_rms_rows(mem_ref[...]) * mg_ref[...]).astype(BF16)
    kv = _dot(mem_n, w_ref[...])
    for h in range(XA_HEADS):
        k = kv[:, h * XA_DIM:(h + 1) * XA_DIM]
        k_ref[h] = (_rms_rows(k) * kg_ref[...]).astype(k_ref.dtype)
        v_ref[h] = kv[:, (XA_HEADS + h) * XA_DIM:(XA_HEADS + h + 1) * XA_DIM].astype(v_ref.dtype)


def xa_keys_values(mem, mem_gain, w_kv, layer, k_gain):
    m, d = mem.shape
    shape = jax.ShapeDtypeStruct((XA_HEADS, m, XA_DIM), BF16)
    nkv = w_kv.shape[2]
    return pl.pallas_call(
        _xa_kv_kernel,
        grid=(1,),
        in_specs=[pl.BlockSpec((m, d), lambda i: (0, 0)),
                  pl.BlockSpec((1, d), lambda i: (0, 0)),
                  pl.BlockSpec((None, d, nkv), lambda i: (layer, 0, 0)),
                  pl.BlockSpec((1, XA_DIM), lambda i: (0, 0))],
        out_specs=[pl.BlockSpec((XA_HEADS, m, XA_DIM), lambda i: (0, 0, 0))] * 2,
        out_shape=[shape, shape],
        compiler_params=_params("arbitrary"),
        name="xa_keys_values",
    )(mem, mem_gain.reshape(1, d), w_kv, k_gain.reshape(1, XA_DIM))


def _xattn_kernel(x_ref, g_ref, wq_ref, qg_ref, k_ref, v_ref, wo_ref, o_ref):
    x = x_ref[...]
    h = (_rms_rows(x) * g_ref[...]).astype(BF16)
    q = _dot(h, wq_ref[...])
    qg = qg_ref[...] * (XA_DIM ** -0.5)
    heads = range(XA_HEADS)
    qh = [(_rms_rows(q[:, hd * XA_DIM:(hd + 1) * XA_DIM]) * qg).astype(BF16) for hd in heads]
    sc = [_dot_nt(qh[hd], k_ref[hd]) for hd in heads]
    pr = [jnp.exp(s - jnp.max(s, axis=-1, keepdims=True)) for s in sc]
    pr = [p / jnp.sum(p, axis=-1, keepdims=True) for p in pr]
    outs = [_dot(pr[hd].astype(BF16), v_ref[hd]).astype(BF16) for hd in heads]
    o = jnp.concatenate(outs, axis=-1)
    o_ref[...] = x + _dot(o, wo_ref[...])


def cross_attention(x, gain, w_q, q_gain, k, v, w_o, layer, tm):
    s, d = x.shape
    hw = XA_HEADS * XA_DIM
    m = k.shape[1]
    return pl.pallas_call(
        _xattn_kernel,
        grid=(s // tm,),
        in_specs=[pl.BlockSpec((tm, d), lambda i: (i, 0)),
                  pl.BlockSpec((1, d), lambda i: (0, 0)),
                  pl.BlockSpec((None, d, hw), lambda i: (layer, 0, 0)),
                  pl.BlockSpec((1, XA_DIM), lambda i: (0, 0)),
                  pl.BlockSpec((XA_HEADS, m, XA_DIM), lambda i: (0, 0, 0)),
                  pl.BlockSpec((XA_HEADS, m, XA_DIM), lambda i: (0, 0, 0)),
                  pl.BlockSpec((None, hw, d), lambda i: (layer, 0, 0))],
        out_specs=pl.BlockSpec((tm, d), lambda i: (i, 0)),
        out_shape=jax.ShapeDtypeStruct((s, d), F32),
        compiler_params=_params("parallel"),
        name="cross_attention",
    )(x, gain.reshape(1, d), w_q, q_gain.reshape(1, XA_DIM), k, v, w_o)


def _pick_tile(total, target):
    t = min(total, target)
    while total % t:
        t //= 2
    return t


def _rotary_tables(s):
    inv = ROPE_BASE ** (-jnp.arange(0, RET_QK_DIM, 2, dtype=F32) / RET_QK_DIM)
    ang = jnp.arange(s).astype(F32)[:, None] * inv[None, :]
    return jnp.cos(ang), jnp.sin(ang)


def kernel(x, mem, norm_mix, norm_xa, norm_ffn, mem_norm, rel_bias, ar_w_in, ar_w_out, dil_q_gain, dil_k_gain, gdn_w_in, gdn_conv, gdn_a_log, gdn_dt_bias, gdn_norm, gdn_w_out, xa_w_q, xa_w_kv, xa_w_o, xa_q_gain, xa_k_gain, ffn_w1, ffn_w3, ffn_w2):
    b, s, d = x.shape
    assert b == 1
    xs = x.reshape(s, d)
    mem2 = mem.reshape(mem.shape[1], d)
    tm = _pick_tile(s, 1024)
    ar_w_in, ar_w_out, gdn_w_out, xa_w_q, xa_w_kv, xa_w_o, ffn_w2 = (
        w.astype(BF16) for w in (ar_w_in, ar_w_out, gdn_w_out, xa_w_q, xa_w_kv, xa_w_o, ffn_w2))

    def tail(xs, layer):
        k, v = xa_keys_values(mem2, mem_norm, xa_w_kv, layer, xa_k_gain[layer])
        xs = cross_attention(xs, norm_xa[layer], xa_w_q, xa_q_gain[layer], k, v, xa_w_o, layer,
                             _pick_tile(s, 512))
        hid = norm_swiglu(xs, norm_ffn[layer], ffn_w1, ffn_w3, layer, tm, _pick_tile(ffn_w1.shape[-1], 512))
        return matmul_residual(hid, ffn_w2, layer, xs, tm, _pick_tile(d, 512))

    proj, qkv1, qkv4, qkv16 = ar_in_proj(xs, norm_mix[0], ar_w_in, dil_q_gain[0], dil_k_gain[0], tm)
    cos, sin = _rotary_tables(s)
    ya = retention_mixer(proj, cos, sin, _pick_tile(s, 1024))
    yb = dilated_attention(qkv1, qkv4, qkv16, rel_bias, _pick_tile(s, 2048), 8)
    xs = matmul2_residual(ya, yb, ar_w_out, 0, xs, tm, _pick_tile(d, 1024))
    xs = tail(xs, 0)

    n_main = GDN_QKV + GDN_V_W
    w_gate = gdn_w_in[0, :, n_main:].astype(BF16)
    qkv, z, ab = gdn_in_proj(xs, norm_mix[1], gdn_w_in, w_gate, gdn_conv[0], tm, 1024)
    beta, gcum, gct = gdn_gates(ab, gdn_a_log[0], gdn_dt_bias[0], _pick_tile(s, 1024))
    og = gdn_mixer(qkv, z, beta, gcum, gct, gdn_norm[0], 2 * GDN_CHUNK, 4)
    xs = matmul_residual(og, gdn_w_out, 0, xs, tm, _pick_tile(d, 512))
    xs = tail(xs, 1)
    return xs.reshape(b, s, d)
```

```python
import functools
import math

import jax
import jax.numpy as jnp
from jax import lax
from jax.experimental import pallas as pl
from jax.experimental.pallas import tpu as pltpu

F32 = jnp.float32
BF16 = jnp.bfloat16
EPS = 1e-6
NEG = -1e30

VMEM_LIMIT_BYTES = 56 * 1024 * 1024
LANES = 128

RET_HEADS = 4
RET_QK_DIM = 256
RET_V_DIM = 512
RET_CHUNK = 128
ROPE_BASE = 10000.0
DIL_HEADS = 8
DIL_DIM = 128
DIL_PATTERNS = ((128, 1), (512, 4), (2048, 16))
DIL_STEPS = 128
REL_BUCKETS = 32
REL_MAX_DIST = 2048
AR_QK = RET_HEADS * RET_QK_DIM
AR_V = RET_HEADS * RET_V_DIM
DIL_W = DIL_HEADS * DIL_DIM
GDN_K_HEADS = 16
GDN_V_HEADS = 32
GDN_DIM = 128
GDN_CONV = 4
GDN_CHUNK = 128
GDN_QK_W = GDN_K_HEADS * GDN_DIM
GDN_V_W = GDN_V_HEADS * GDN_DIM
GDN_QKV = 2 * GDN_QK_W + GDN_V_W
XA_HEADS = 4
XA_DIM = 128


def _params(*semantics):
    return pltpu.CompilerParams(dimension_semantics=semantics, vmem_limit_bytes=VMEM_LIMIT_BYTES)


def _dot(a, b):
    return jnp.dot(a, b, preferred_element_type=F32)


def _dot_nt(a, b):
    return lax.dot_general(a, b, (((1,), (1,)), ((), ())), preferred_element_type=F32)


def _dot_tn(a, b):
    return lax.dot_general(a, b, (((0,), (0,)), ((), ())), preferred_element_type=F32)


def _dot_f32(a, b):
    return jnp.dot(a, b, preferred_element_type=F32, precision=lax.Precision.HIGHEST)


def _dot_nt_f32(a, b):
    return lax.dot_general(a, b, (((1,), (1,)), ((), ())), preferred_element_type=F32,
                           precision=lax.Precision.HIGHEST)


def _rms_rows(x):
    return x * lax.rsqrt(jnp.mean(x * x, axis=-1, keepdims=True) + EPS)


def _silu(x):
    return x * jax.nn.sigmoid(x)


def _norm_swiglu_kernel(x_ref, g_ref, w1_ref, w3_ref, o_ref, h_ref):
    @pl.when(pl.program_id(1) == 0)
    def _():
        h_ref[...] = (_rms_rows(x_ref[...]) * g_ref[...]).astype(BF16)

    h = h_ref[...]
    a = _dot(h, w1_ref[...])
    b = _dot(h, w3_ref[...])
    o_ref[...] = (_silu(a) * b).astype(o_ref.dtype)


def norm_swiglu(x, gain, w1, w3, layer, tm, tn):
    m, k = x.shape
    n = w1.shape[2]
    return pl.pallas_call(
        _norm_swiglu_kernel,
        grid=(m // tm, n // tn),
        in_specs=[pl.BlockSpec((tm, k), lambda i, j: (i, 0)),
                  pl.BlockSpec((1, k), lambda i, j: (0, 0)),
                  pl.BlockSpec((None, k, tn), lambda i, j: (layer, 0, j)),
                  pl.BlockSpec((None, k, tn), lambda i, j: (layer, 0, j))],
        out_specs=pl.BlockSpec((tm, tn), lambda i, j: (i, j)),
        out_shape=jax.ShapeDtypeStruct((m, n), BF16),
        scratch_shapes=[pltpu.VMEM((tm, k), BF16)],
        compiler_params=_params("parallel", "arbitrary"),
        name="norm_swiglu",
    )(x, gain.reshape(1, k), w1, w3)


def _matmul_residual_kernel(y_ref, w_ref, x_ref, o_ref):
    o_ref[...] = x_ref[...] + _dot(y_ref[...], w_ref[...])


def matmul_residual(y, w, layer, x, tm, tn):
    m, k = y.shape
    n = w.shape[2]
    return pl.pallas_call(
        _matmul_residual_kernel,
        grid=(m // tm, n // tn),
        in_specs=[pl.BlockSpec((tm, k), lambda i, j: (i, 0)),
                  pl.BlockSpec((None, k, tn), lambda i, j: (layer, 0, j)),
                  pl.BlockSpec((tm, tn), lambda i, j: (i, j))],
        out_specs=pl.BlockSpec((tm, tn), lambda i, j: (i, j)),
        out_shape=jax.ShapeDtypeStruct((m, n), F32),
        compiler_params=_params("parallel", "parallel"),
        name="matmul_residual",
    )(y, w, x)


def _matmul2_residual_kernel(ya_ref, yb_ref, wa_ref, wb_ref, x_ref, o_ref):
    o_ref[...] = x_ref[...] + _dot(ya_ref[...], wa_ref[...]) + _dot(yb_ref[...], wb_ref[...])


def matmul2_residual(ya, yb, w, layer, x, tm, tn):
    m, ka = ya.shape
    kb = yb.shape[1]
    n = w.shape[2]
    assert ka % kb == 0
    return pl.pallas_call(
        _matmul2_residual_kernel,
        grid=(m // tm, n // tn),
        in_specs=[pl.BlockSpec((tm, ka), lambda i, j: (i, 0)),
                  pl.BlockSpec((tm, kb), lambda i, j: (i, 0)),
                  pl.BlockSpec((None, ka, tn), lambda i, j: (layer, 0, j)),
                  pl.BlockSpec((None, kb, tn), lambda i, j: (layer, ka // kb, j)),
                  pl.BlockSpec((tm, tn), lambda i, j: (i, j))],
        out_specs=pl.BlockSpec((tm, tn), lambda i, j: (i, j)),
        out_shape=jax.ShapeDtypeStruct((m, n), F32),
        compiler_params=_params("parallel", "parallel"),
        name="matmul2_residual",
    )(ya, yb, w, w, x)


def _retention_kernel(lg_ref, q_ref, k_ref, v_ref, g_ref, cos_ref, sin_ref, o_ref, state_ref,
                      *, chunk, n_chunks):
    @pl.when(pl.program_id(1) == 0)
    def _():
        state_ref[...] = jnp.zeros_like(state_ref)

    c = chunk
    half = RET_QK_DIM // 2
    lg = lg_ref[0]
    lg_l = lg[:, :LANES]
    ii = lax.broadcasted_iota(jnp.int32, (c, c), 0)
    jj = lax.broadcasted_iota(jnp.int32, (c, c), 1)
    rel = (ii - jj).astype(F32)
    inner = jnp.where(rel >= 0, jnp.exp(lg[:, :c] * jnp.maximum(rel, 0.0)), 0.0)
    idx = lax.broadcasted_iota(jnp.int32, (c, LANES), 0).astype(F32)
    q_dec = jnp.exp(lg_l * (idx + 1.0))
    k_dec = jnp.exp(lg_l * (c - 1.0 - idx))
    c_dec = jnp.exp(lg * float(c))
    k_scale = RET_QK_DIM ** -0.5

    def rot(t, cos, sin):
        t1, t2 = t[:, :half], t[:, half:]
        return t1 * cos - t2 * sin, t1 * sin + t2 * cos

    rows = [pl.ds(ci * c, c) for ci in range(n_chunks)]
    qb, kb, qd, kd = [], [], [], []
    for r in rows:
        cos = cos_ref[r, :]
        sin = sin_ref[r, :]
        q1, q2 = rot(q_ref[r, :].astype(F32), cos, sin)
        k1, k2 = rot(k_ref[r, :].astype(F32) * k_scale, cos, sin)
        qb.append(jnp.concatenate([q1, q2], axis=-1).astype(BF16))
        kb.append(jnp.concatenate([k1, k2], axis=-1).astype(BF16))
        qd.append(jnp.concatenate([q1 * q_dec, q2 * q_dec], axis=-1).astype(BF16))
        kd.append(jnp.concatenate([k1 * k_dec, k2 * k_dec], axis=-1).astype(BF16))
    scores = [(_dot_nt(a, b) * inner).astype(BF16) for a, b in zip(qb, kb)]
    kv = [_dot_tn(a, v_ref[r, :]) for a, r in zip(kd, rows)]
    states = [state_ref[...]]
    for inc in kv:
        states.append(states[-1] * c_dec + inc)
    state_ref[...] = states[-1]
    outs = [_dot(sc, v_ref[r, :]) + _dot(a, st.astype(BF16))
            for sc, r, a, st in zip(scores, rows, qd, states)]
    for o, r in zip(outs, rows):
        gate = _silu(g_ref[r, :].astype(F32))
        o_ref[r, :] = (_rms_rows(o) * gate).astype(o_ref.dtype)


def retention_mixer(proj, cos, sin, rows_per_step):
    s = proj.shape[0]
    t = rows_per_step
    log_gamma = jnp.log(1.0 - 2.0 ** (-5.0 - jnp.arange(RET_HEADS, dtype=F32)))
    lg = jnp.broadcast_to(log_gamma[:, None, None], (RET_HEADS, 1, RET_V_DIM))
    qk_blocks = AR_QK // RET_QK_DIM
    v_off = 2 * AR_QK // RET_V_DIM
    g_off = v_off + AR_V // RET_V_DIM
    kern = functools.partial(_retention_kernel, chunk=RET_CHUNK, n_chunks=t // RET_CHUNK)
    return pl.pallas_call(
        kern,
        grid=(RET_HEADS, s // t),
        in_specs=[pl.BlockSpec((1, 1, RET_V_DIM), lambda h, i: (h, 0, 0)),
                  pl.BlockSpec((t, RET_QK_DIM), lambda h, i: (i, h)),
                  pl.BlockSpec((t, RET_QK_DIM), lambda h, i: (i, qk_blocks + h)),
                  pl.BlockSpec((t, RET_V_DIM), lambda h, i: (i, v_off + h)),
                  pl.BlockSpec((t, RET_V_DIM), lambda h, i: (i, g_off + h)),
                  pl.BlockSpec((t, RET_QK_DIM // 2), lambda h, i: (i, 0)),
                  pl.BlockSpec((t, RET_QK_DIM // 2), lambda h, i: (i, 0))],
        out_specs=pl.BlockSpec((t, RET_V_DIM), lambda h, i: (i, h)),
        out_shape=jax.ShapeDtypeStruct((s, AR_V), BF16),
        scratch_shapes=[pltpu.VMEM((RET_QK_DIM, RET_V_DIM), F32)],
        compiler_params=_params("parallel", "arbitrary"),
        name="retention",
    )(lg, proj, proj, proj, proj, cos, sin)


def _ar_in_proj_kernel(x_ref, g_ref, w_ref, gn_ref, oa_ref, o1_ref, o4_ref, o16_ref,
                       h_ref, buf, nb, nb4, *, n_a):
    j = pl.program_id(1)
    tm, tn = buf.shape
    dd = DIL_DIM

    @pl.when(j == 0)
    def _():
        h_ref[...] = (_rms_rows(x_ref[...]) * g_ref[...]).astype(BF16)

    @pl.when(j < n_a)
    def _():
        oa_ref[...] = _dot(h_ref[...], w_ref[...]).astype(oa_ref.dtype)

    def head_major(normed):
        buf[...] = _dot(h_ref[...], w_ref[...])
        for hd in range(tn // dd):
            slot = hd % 2
            x = buf[:, hd * dd:(hd + 1) * dd]
            nb[slot] = _rms_rows(x) * gn_ref[...] if normed else x
            o1_ref[hd] = nb[slot].astype(o1_ref.dtype)
            for r4 in range(4):
                stream = nb[slot, pl.ds(r4, tm // 4, stride=4), :]
                nb4[slot, r4] = stream
                o4_ref[hd, :, r4 * dd:(r4 + 1) * dd] = stream.astype(o4_ref.dtype)
            for r in range(16):
                sub = nb4[slot, r % 4, pl.ds(r // 4, tm // 16, stride=4), :]
                o16_ref[hd, :, r * dd:(r + 1) * dd] = sub.astype(o16_ref.dtype)

    @pl.when((j >= n_a) & (j < n_a + 2))
    def _():
        head_major(True)

    @pl.when(j == n_a + 2)
    def _():
        head_major(False)


def ar_in_proj(x, gain, w, q_gain, k_gain, tm):
    m, k = x.shape
    dd = DIL_DIM
    tn = DIL_W
    n_a = (2 * AR_QK + 2 * AR_V) // tn
    nh = 3 * DIL_HEADS
    gains = jnp.stack([q_gain * (dd ** -0.5), k_gain, jnp.ones_like(k_gain)]).reshape(3, 1, dd)

    def grp(j):
        return jnp.clip(j - n_a, 0, 2)

    return pl.pallas_call(
        functools.partial(_ar_in_proj_kernel, n_a=n_a),
        grid=(m // tm, n_a + 3),
        in_specs=[pl.BlockSpec((tm, k), lambda i, j: (i, 0)),
                  pl.BlockSpec((1, k), lambda i, j: (0, 0)),
                  pl.BlockSpec((None, k, tn), lambda i, j: (0, 0, j)),
                  pl.BlockSpec((None, 1, dd), lambda i, j: (grp(j), 0, 0))],
        out_specs=[pl.BlockSpec((tm, tn), lambda i, j: (i, jnp.minimum(j, n_a - 1)))]
        + [pl.BlockSpec((DIL_HEADS, tm // d, d * dd), lambda i, j: (grp(j), i, 0)) for d in (1, 4, 16)],
        out_shape=[jax.ShapeDtypeStruct((m, n_a * tn), BF16)]
        + [jax.ShapeDtypeStruct((nh, m // d, d * dd), BF16) for d in (1, 4, 16)],
        scratch_shapes=[pltpu.VMEM((tm, k), BF16),
                        pltpu.VMEM((tm, tn), F32),
                        pltpu.VMEM((2, tm, dd), F32),
                        pltpu.VMEM((2, 4, tm // 4, dd), F32)],
        compiler_params=_params("arbitrary", "arbitrary"),
        name="ar_in_proj",
    )(x, gain.reshape(1, k), w, gains)


def _dilated_kernel(bias_ref,
                    q1, k1, k1p, v1, v1p,
                    q4, k4, k4p, v4, v4p,
                    q16, k16, k16p, v16, v16p,
                    o_ref, acc_ref, m_ref, l_ref, *, tile, group):
    n = DIL_STEPS
    d_dim = DIL_DIM
    first = pl.program_id(1) == 0

    def run_group(bias_idx, blocks, mode):
        def ld(p):
            return p[0][p[1]]

        bias_p = bias_ref[bias_idx, :, :n]
        bias_c = bias_ref[bias_idx, :, n:]
        sp = [_dot_nt(ld(b[0]), ld(b[1])) + bias_p for b in blocks]
        sc = [_dot_nt(ld(b[0]), ld(b[2])) + bias_c for b in blocks]
        sp = [jnp.where(first, NEG, x) if b[5] else x for x, b in zip(sp, blocks)]
        m = [jnp.max(jnp.maximum(x, y), axis=-1, keepdims=True) for x, y in zip(sp, sc)]
        pp = [jnp.exp(x - mm) for x, mm in zip(sp, m)]
        pc = [jnp.exp(x - mm) for x, mm in zip(sc, m)]
        l = [jnp.sum(x + y, axis=-1, keepdims=True) for x, y in zip(pp, pc)]
        o = [_dot(x.astype(BF16), ld(b[3])) + _dot(y.astype(BF16), ld(b[4]))
             for x, y, b in zip(pp, pc, blocks)]
        for b, mm, ll, oo in zip(blocks, m, l, o):
            rows = b[6]
            if mode == "init":
                m_ref[rows, :] = jnp.broadcast_to(mm, (n, d_dim))
                l_ref[rows, :] = jnp.broadcast_to(ll, (n, d_dim))
                acc_ref[rows, :] = oo
                continue
            m_old = m_ref[rows, :]
            m_new = jnp.maximum(m_old, mm)
            a_old = jnp.exp(m_old - m_new)
            a_cur = jnp.exp(mm - m_new)
            l_new = l_ref[rows, :] * a_old + ll * a_cur
            acc_new = acc_ref[rows, :] * a_old + oo * a_cur
            if mode == "merge":
                m_ref[rows, :] = m_new
                l_ref[rows, :] = l_new
                acc_ref[rows, :] = acc_new
            else:
                o_ref[rows, :] = (acc_new / l_new).astype(o_ref.dtype)

    every = slice(None)

    blocks = []
    for r in range(16):
        ls = slice(r * d_dim, (r + 1) * d_dim)
        blocks.append(((q16, (every, ls)), (k16p, (every, ls)), (k16, (every, ls)),
                       (v16p, (every, ls)), (v16, (every, ls)), True, pl.ds(r, n, stride=16)))
    for g0 in range(0, len(blocks), group):
        run_group(2, blocks[g0:g0 + group], "init")

    blocks = []
    for r in range(4):
        ls = slice(r * d_dim, (r + 1) * d_dim)
        for b in range(tile // (4 * n)):
            cur = pl.ds(b * n, n)
            prev = pl.ds((b - 1) * n, n)
            kp, vp = (((k4p, (every, ls)), (v4p, (every, ls))) if b == 0
                      else ((k4, (prev, ls)), (v4, (prev, ls))))
            blocks.append(((q4, (cur, ls)), kp, (k4, (cur, ls)), vp, (v4, (cur, ls)), b == 0,
                           pl.ds(b * 4 * n + r, n, stride=4)))
    for g0 in range(0, len(blocks), group):
        run_group(1, blocks[g0:g0 + group], "merge")

    blocks = []
    for b in range(tile // n):
        cur = pl.ds(b * n, n)
        prev = pl.ds((b - 1) * n, n)
        kp, vp = (((k1p, (every, every)), (v1p, (every, every))) if b == 0
                  else ((k1, (prev, every)), (v1, (prev, every))))
        blocks.append(((q1, (cur, every)), kp, (k1, (cur, every)), vp, (v1, (cur, every)), b == 0, cur))
    for g0 in range(0, len(blocks), group):
        run_group(0, blocks[g0:g0 + group], "final")


def _t5_bucket(dist):
    exact = REL_BUCKETS // 2
    large = exact + (jnp.log(jnp.maximum(dist, exact).astype(F32) / exact)
                     / math.log(REL_MAX_DIST / exact) * (REL_BUCKETS - exact)).astype(jnp.int32)
    large = jnp.minimum(large, REL_BUCKETS - 1)
    return jnp.where(dist < exact, dist, large)


def _dilated_bias(rel_bias):
    n = DIL_STEPS
    steps = jnp.arange(n + 1)
    vec = jnp.stack([rel_bias.astype(F32)[_t5_bucket(steps * dil)] for (_, dil) in DIL_PATTERNS])
    vec = vec.transpose(0, 2, 1)
    width = 3 * n
    p = jnp.full(vec.shape[:2] + (width,), NEG, F32)
    p = p.at[:, :, n - 1:2 * n].set(vec[:, :, ::-1])
    tiled = jnp.broadcast_to(p[:, :, None, :], vec.shape[:2] + (n, width))
    skew = tiled.reshape(vec.shape[:2] + (n * width,))[:, :, :n * (width - 1)]
    skew = skew.reshape(vec.shape[:2] + (n, width - 1))
    return skew[:, :, :, n - 1:3 * n - 1]


def dilated_attention(qkv1, qkv4, qkv16, rel_bias, tile, group):
    _, s, dd = qkv1.shape
    n = DIL_STEPS
    h_n = DIL_HEADS
    bias = _dilated_bias(rel_bias)
    args, specs = [bias], [pl.BlockSpec((len(DIL_PATTERNS), None, n, 2 * n), lambda h, i: (0, h, 0, 0))]
    for dil, view in ((1, qkv1), (4, qkv4), (16, qkv16)):
        rows = tile // dil
        per = rows // n

        def cur_spec(off, rows=rows, dil=dil):
            return pl.BlockSpec((None, rows, dil * dd), lambda h, i: (off + h, i, 0))

        def prev_spec(off, per=per, dil=dil):
            return pl.BlockSpec((None, n, dil * dd), lambda h, i: (off + h, jnp.maximum(i * per - 1, 0), 0))

        args += [view, view, view, view, view]
        specs += [cur_spec(0), cur_spec(h_n), prev_spec(h_n), cur_spec(2 * h_n), prev_spec(2 * h_n)]
    return pl.pallas_call(
        functools.partial(_dilated_kernel, tile=tile, group=group),
        grid=(h_n, s // tile),
        in_specs=specs,
        out_specs=pl.BlockSpec((tile, dd), lambda h, i: (i, h)),
        out_shape=jax.ShapeDtypeStruct((s, h_n * dd), BF16),
        scratch_shapes=[pltpu.VMEM((tile, dd), F32)] * 3,
        compiler_params=_params("parallel", "arbitrary"),
        name="dilated_attention",
    )(*args)


def _gdn_in_proj_kernel(x_ref, g_ref, w_ref, ws_ref, cw_ref, oc_ref, oz_ref, os_ref,
                        h_ref, buf, tails, *, n_norm, n_conv, q_tiles):
    i = pl.program_id(0)
    j = pl.program_id(1)
    tm = buf.shape[0] - 8
    tn = buf.shape[1]
    halo = 8
    sub = 128
    dd = GDN_DIM

    @pl.when((i == 0) & (j == 0))
    def _():
        tails[...] = jnp.zeros_like(tails)

    @pl.when(j == 0)
    def _():
        h = (_rms_rows(x_ref[...]) * g_ref[...]).astype(BF16)
        h_ref[...] = h
        os_ref[...] = _dot(h, ws_ref[...])

    def conv_tile(l2norm):
        buf[halo:halo + tm, :] = _dot(h_ref[...], w_ref[...])
        buf[0:halo, :] = tails[j]
        tails[j] = buf[tm:tm + halo, :]
        scale = jnp.where(j < q_tiles, dd ** -0.5, 1.0)
        for r0 in range(0, tm, sub):
            y = buf[halo + r0:halo + r0 + sub, :] * cw_ref[GDN_CONV - 1:GDN_CONV, :]
            for t in range(GDN_CONV - 1):
                off = halo + r0 - (GDN_CONV - 1) + t
                y = y + buf[off:off + sub, :] * cw_ref[t:t + 1, :]
            y = _silu(y)
            if l2norm:
                for h0 in range(0, tn, dd):
                    yh = y[:, h0:h0 + dd]
                    yh = yh * (lax.rsqrt(jnp.sum(yh * yh, axis=-1, keepdims=True) + EPS) * scale)
                    oc_ref[r0:r0 + sub, h0:h0 + dd] = yh.astype(oc_ref.dtype)
            else:
                oc_ref[r0:r0 + sub, :] = y.astype(oc_ref.dtype)

    @pl.when(j < n_norm)
    def _():
        conv_tile(True)

    @pl.when((j >= n_norm) & (j < n_conv))
    def _():
        conv_tile(False)

    @pl.when(j >= n_conv)
    def _():
        oz_ref[...] = _dot(h_ref[...], w_ref[...]).astype(oz_ref.dtype)


def gdn_in_proj(x, gain, w, w_side, conv_w, tm, tn):
    m, k = x.shape
    ns = w_side.shape[1]
    n_conv = GDN_QKV // tn
    n_norm = 2 * GDN_QK_W // tn
    n_tiles = (GDN_QKV + GDN_V_W) // tn
    kern = functools.partial(_gdn_in_proj_kernel, n_norm=n_norm, n_conv=n_conv, q_tiles=GDN_QK_W // tn)
    return pl.pallas_call(
        kern,
        grid=(m // tm, n_tiles),
        in_specs=[pl.BlockSpec((tm, k), lambda i, j: (i, 0)),
                  pl.BlockSpec((1, k), lambda i, j: (0, 0)),
                  pl.BlockSpec((None, k, tn), lambda i, j: (0, 0, j)),
                  pl.BlockSpec((k, ns), lambda i, j: (0, 0)),
                  pl.BlockSpec((GDN_CONV, tn), lambda i, j: (0, jnp.minimum(j, n_conv - 1)))],
        out_specs=[pl.BlockSpec((tm, tn), lambda i, j: (i, jnp.minimum(j, n_conv - 1))),
                   pl.BlockSpec((tm, tn), lambda i, j: (i, jnp.maximum(j - n_conv, 0))),
                   pl.BlockSpec((tm, ns), lambda i, j: (i, 0))],
        out_shape=[jax.ShapeDtypeStruct((m, GDN_QKV), BF16),
                   jax.ShapeDtypeStruct((m, GDN_V_W), BF16),
                   jax.ShapeDtypeStruct((m, ns), F32)],
        scratch_shapes=[pltpu.VMEM((tm, k), BF16),
                        pltpu.VMEM((8 + tm, tn), F32),
                        pltpu.VMEM((n_conv, 8, tn), F32)],
        compiler_params=_params("arbitrary", "arbitrary"),
        name="gdn_in_proj",
    )(x, gain.reshape(1, k), w, w_side, conv_w)


def _gdn_gates_kernel(ab_ref, alog_ref, dt_ref, beta_ref, gcum_ref, gct_ref, *, chunk):
    t = ab_ref.shape[0]
    hv = GDN_V_HEADS
    ab = ab_ref[...]
    beta_ref[...] = jax.nn.sigmoid(ab[:, :hv])
    z = ab[:, hv:] + dt_ref[...]
    softplus = jnp.maximum(z, 0.0) + jnp.log1p(jnp.exp(-jnp.abs(z)))
    g = -jnp.exp(alog_ref[...]) * softplus
    ii = lax.broadcasted_iota(jnp.int32, (chunk, chunk), 0)
    jj = lax.broadcasted_iota(jnp.int32, (chunk, chunk), 1)
    tri = (ii >= jj).astype(F32)
    eh = (lax.broadcasted_iota(jnp.int32, (hv, hv), 0) == lax.broadcasted_iota(jnp.int32, (hv, hv), 1)).astype(F32)
    gcs = [_dot_f32(tri, g[r0:r0 + chunk]) for r0 in range(0, t, chunk)]
    gts = [_dot_nt_f32(eh, gc) for gc in gcs]
    for n, (gc, gt) in enumerate(zip(gcs, gts)):
        gcum_ref[n * chunk:(n + 1) * chunk, :] = gc
        gct_ref[:, n * chunk:(n + 1) * chunk] = gt


def gdn_gates(ab, a_log, dt_bias, tm):
    s = ab.shape[0]
    hv = GDN_V_HEADS
    return pl.pallas_call(
        functools.partial(_gdn_gates_kernel, chunk=GDN_CHUNK),
        grid=(s // tm,),
        in_specs=[pl.BlockSpec((tm, 2 * hv), lambda i: (i, 0)),
                  pl.BlockSpec((1, hv), lambda i: (0, 0)),
                  pl.BlockSpec((1, hv), lambda i: (0, 0))],
        out_specs=[pl.BlockSpec((tm, hv), lambda i: (i, 0)),
                   pl.BlockSpec((tm, hv), lambda i: (i, 0)),
                   pl.BlockSpec((hv, tm), lambda i: (0, i))],
        out_shape=[jax.ShapeDtypeStruct((s, hv), F32), jax.ShapeDtypeStruct((s, hv), F32),
                   jax.ShapeDtypeStruct((hv, s), F32)],
        compiler_params=_params("parallel"),
        name="gdn_gates",
    )(ab, a_log.reshape(1, hv), dt_bias.reshape(1, hv))


def _unit_lower_inverses(a_list):
    c = a_list[0].shape[0]
    ii = lax.broadcasted_iota(jnp.int32, (c, c), 0)
    jj = lax.broadcasted_iota(jnp.int32, (c, c), 1)
    eye = jnp.where(ii == jj, 1.0, 0.0)
    p = [-a for a in a_list]
    t = [eye + x for x in p]
    pb = [x.astype(BF16) for x in p]
    p = [_dot(x, x) for x in pb]
    span = 4
    while span < c:
        pb = [x.astype(BF16) for x in p]
        st = [_dot(jnp.concatenate([x, y.astype(BF16)], axis=0), x) for x, y in zip(pb, t)]
        p = [x[:c] for x in st]
        t = [y + x[c:] for x, y in zip(st, t)]
        span *= 2
    pb = [x.astype(BF16) for x in p]
    return [y + _dot(y.astype(BF16), x) for x, y in zip(pb, t)]


def _gdn_kernel(q_ref, k_ref, v_ref, z_ref, beta_ref, gcum_ref, gct_ref, ng_ref, o_ref, state_ref,
                *, tb, chunk, kheads):
    grp = pl.program_id(0)
    c = chunk
    dd = GDN_DIM
    nh = 2 * kheads
    nc = tb // c

    @pl.when(pl.program_id(1) == 0)
    def _():
        state_ref[...] = jnp.zeros_like(state_ref)

    lane_h = lax.broadcasted_iota(jnp.int32, (1, GDN_V_HEADS), 1)
    ii = lax.broadcasted_iota(jnp.int32, (c, c), 0)
    jj = lax.broadcasted_iota(jnp.int32, (c, c), 1)
    causal = ii >= jj
    strict = ii > jj
    ng = ng_ref[...]

    def head_cols(h):
        return slice(h * dd, (h + 1) * dd)

    for ci in range(nc):
        r = slice(ci * c, (ci + 1) * c)
        beta_t = beta_ref[r, :]
        gcum_t = gcum_ref[r, :]
        beta_col, gc_col, gc_row = [], [], []
        for h in range(nh):
            hv = grp * nh + h
            sel = lane_h == hv
            beta_col.append(jnp.sum(jnp.where(sel, beta_t, 0.0), axis=-1, keepdims=True))
            gc_col.append(jnp.sum(jnp.where(sel, gcum_t, 0.0), axis=-1, keepdims=True))
            gc_row.append(gct_ref[pl.ds(hv, 1), :][:, r])
        eg_col = [jnp.exp(x) for x in gc_col]

        k16 = [k_ref[r, head_cols(kh)] for kh in range(kheads)]
        q16 = [q_ref[r, head_cols(kh)] for kh in range(kheads)]
        kf = [x.astype(F32) for x in k16]
        qkk = [_dot_nt(jnp.concatenate([q, k], axis=0), k) for q, k in zip(q16, k16)]

        decay, a_list = [], []
        for h in range(nh):
            dlog = gc_col[h] - gc_row[h]
            dec = jnp.where(causal, jnp.exp(jnp.where(causal, dlog, 0.0)), 0.0)
            decay.append(dec)
            a_list.append(jnp.where(strict, qkk[h // 2][c:] * beta_col[h] * dec, 0.0))
        t_list = _unit_lower_inverses(a_list)

        uw = []
        for h in range(nh):
            rhs = jnp.concatenate([v_ref[r, head_cols(h)].astype(F32) * beta_col[h],
                                   kf[h // 2] * (beta_col[h] * eg_col[h])], axis=1)
            uw.append(_dot(t_list[h].astype(BF16), rhs.astype(BF16)))

        states = [state_ref[h] for h in range(nh)]
        s16 = [s.astype(BF16) for s in states]
        wq = []
        for h in range(nh):
            lhs = jnp.concatenate([uw[h][:, dd:], q16[h // 2].astype(F32) * eg_col[h]], axis=0)
            wq.append(_dot(lhs.astype(BF16), s16[h]))
        vn16 = [(uw[h][:, :dd] - wq[h][:c]).astype(BF16) for h in range(nh)]
        outs = []
        for h in range(nh):
            intra = jnp.where(causal, qkk[h // 2][:c] * decay[h], 0.0)
            outs.append(wq[h][c:] + _dot(intra.astype(BF16), vn16[h]))
        for h in range(nh):
            g_last = gc_col[h][c - 1:c, :]
            kd = kf[h // 2] * jnp.exp(g_last - gc_col[h])
            state_ref[h] = states[h] * jnp.exp(g_last) + _dot_tn(kd.astype(BF16), vn16[h])
        for h in range(nh):
            zg = _silu(z_ref[r, head_cols(h)].astype(F32))
            o_ref[r, head_cols(h)] = (_rms_rows(outs[h]) * ng * zg).astype(o_ref.dtype)


def gdn_mixer(qkv, z, beta, gcum, gct, norm_gain, tb, kheads):
    s = qkv.shape[0]
    dd = GDN_DIM
    wk = kheads * dd
    wv = 2 * wk
    k0 = GDN_QK_W // wk
    v0 = 2 * GDN_QK_W // wv
    return pl.pallas_call(
        functools.partial(_gdn_kernel, tb=tb, chunk=GDN_CHUNK, kheads=kheads),
        grid=(GDN_K_HEADS // kheads, s // tb),
        in_specs=[pl.BlockSpec((tb, wk), lambda g, i: (i, g)),
                  pl.BlockSpec((tb, wk), lambda g, i: (i, k0 + g)),
                  pl.BlockSpec((tb, wv), lambda g, i: (i, v0 + g)),
                  pl.BlockSpec((tb, wv), lambda g, i: (i, g)),
                  pl.BlockSpec((tb, GDN_V_HEADS), lambda g, i: (i, 0)),
                  pl.BlockSpec((tb, GDN_V_HEADS), lambda g, i: (i, 0)),
                  pl.BlockSpec((GDN_V_HEADS, tb), lambda g, i: (0, i)),
                  pl.BlockSpec((1, dd), lambda g, i: (0, 0))],
        out_specs=pl.BlockSpec((tb, wv), lambda g, i: (i, g)),
        out_shape=jax.ShapeDtypeStruct((s, GDN_V_W), BF16),
        scratch_shapes=[pltpu.VMEM((2 * kheads, dd, dd), F32)],
        compiler_params=_params("parallel", "arbitrary"),
        name="gated_deltanet",
    )(qkv, qkv, qkv, z, beta, gcum, gct, norm_gain.reshape(1, dd))


def _xa_kv_kernel(mem_ref, mg_ref, w_ref, kg_ref, k_ref, v_ref):
    mem_n = (_rms_rows(mem_ref[...]) * mg_ref[...]).astype(BF16)
    kv = _dot(mem_n, w_ref[...])
    for h in range(XA_HEADS):
        k = kv[:, h * XA_DIM:(h + 1) * XA_DIM]
        k_ref[h] = (_rms_rows(k) * kg_ref[...]).astype(k_ref.dtype)
        v_ref[h] = kv[:, (XA_HEADS + h) * XA_DIM:(XA_HEADS + h + 1) * XA_DIM].astype(v_ref.dtype)


def xa_keys_values(mem, mem_gain, w_kv, layer, k_gain):
    m, d = mem.shape
    shape = jax.ShapeDtypeStruct((XA_HEADS, m, XA_DIM), BF16)
    nkv = w_kv.shape[2]
    return pl.pallas_call(
        _xa_kv_kernel,
        grid=(1,),
        in_specs=[pl.BlockSpec((m, d), lambda i: (0, 0)),
                  pl.BlockSpec((1, d), lambda i: (0, 0)),
                  pl.BlockSpec((None, d, nkv), lambda i: (layer, 0, 0)),
                  pl.BlockSpec((1, XA_DIM), lambda i: (0, 0))],
        out_specs=[pl.BlockSpec((XA_HEADS, m, XA_DIM), lambda i: (0, 0, 0))] * 2,
        out_shape=[shape, shape],
        compiler_params=_params("arbitrary"),
        name="xa_keys_values",
    )(mem, mem_gain.reshape(1, d), w_kv, k_gain.reshape(1, XA_DIM))


def _xattn_kernel(x_ref, g_ref, wq_ref, qg_ref, k_ref, v_ref, wo_ref, o_ref):
    x = x_ref[...]
    h = (_rms_rows(x) * g_ref[...]).astype(BF16)
    q = _dot(h, wq_ref[...])
    qg = qg_ref[...] * (XA_DIM ** -0.5)
    heads = range(XA_HEADS)
    qh = [(_rms_rows(q[:, hd * XA_DIM:(hd + 1) * XA_DIM]) * qg).astype(BF16) for hd in heads]
    sc = [_dot_nt(qh[hd], k_ref[hd]) for hd in heads]
    pr = [jnp.exp(s - jnp.max(s, axis=-1, keepdims=True)) for s in sc]
    pr = [p / jnp.sum(p, axis=-1, keepdims=True) for p in pr]
    outs = [_dot(pr[hd].astype(BF16), v_ref[hd]).astype(BF16) for hd in heads]
    o = jnp.concatenate(outs, axis=-1)
    o_ref[...] = x + _dot(o, wo_ref[...])


def cross_attention(x, gain, w_q, q_gain, k, v, w_o, layer, tm):
    s, d = x.shape
    hw = XA_HEADS * XA_DIM
    m = k.shape[1]
    return pl.pallas_call(
        _xattn_kernel,
        grid=(s // tm,),
        in_specs=[pl.BlockSpec((tm, d), lambda i: (i, 0)),
                  pl.BlockSpec((1, d), lambda i: (0, 0)),
                  pl.BlockSpec((None, d, hw), lambda i: (layer, 0, 0)),
                  pl.BlockSpec((1, XA_DIM), lambda i: (0, 0)),
                  pl.BlockSpec((XA_HEADS, m, XA_DIM), lambda i: (0, 0, 0)),
                  pl.BlockSpec((XA_HEADS, m, XA_DIM), lambda i: (0, 0, 0)),
                  pl.BlockSpec((None, hw, d), lambda i: (layer, 0, 0))],
        out_specs=pl.BlockSpec((tm, d), lambda i: (i, 0)),
        out_shape=jax.ShapeDtypeStruct((s, d), F32),
        compiler_params=_params("parallel"),
        name="cross_attention",
    )(x, gain.reshape(1, d), w_q, q_gain.reshape(1, XA_DIM), k, v, w_o)


def _pick_tile(total, target):
    t = min(total, target)
    while total % t:
        t //= 2
    return t


def _rotary_tables(s):
    inv = ROPE_BASE ** (-jnp.arange(0, RET_QK_DIM, 2, dtype=F32) / RET_QK_DIM)
    ang = jnp.arange(s).astype(F32)[:, None] * inv[None, :]
    return jnp.cos(ang), jnp.sin(ang)


def kernel(x, mem, norm_mix, norm_xa, norm_ffn, mem_norm, rel_bias, ar_w_in, ar_w_out, dil_q_gain, dil_k_gain, gdn_w_in, gdn_conv, gdn_a_log, gdn_dt_bias, gdn_norm, gdn_w_out, xa_w_q, xa_w_kv, xa_w_o, xa_q_gain, xa_k_gain, ffn_w1, ffn_w3, ffn_w2):
    b, s, d = x.shape
    assert b == 1
    xs = x.reshape(s, d)
    mem2 = mem.reshape(mem.shape[1], d)
    tm = _pick_tile(s, 1024)
    ar_w_in, ar_w_out, gdn_w_out, xa_w_q, xa_w_kv, xa_w_o, ffn_w1, ffn_w3, ffn_w2 = (
        w.astype(BF16) for w in (ar_w_in, ar_w_out, gdn_w_out, xa_w_q, xa_w_kv, xa_w_o, ffn_w1, ffn_w3, ffn_w2))

    def tail(xs, layer):
        k, v = xa_keys_values(mem2, mem_norm, xa_w_kv, layer, xa_k_gain[layer])
        xs = cross_attention(xs, norm_xa[layer], xa_w_q, xa_q_gain[layer], k, v, xa_w_o, layer,
                             _pick_tile(s, 512))
        hid = norm_swiglu(xs, norm_ffn[layer], ffn_w1, ffn_w3, layer, tm, _pick_tile(ffn_w1.shape[-1], 512))
        return matmul_residual(hid, ffn_w2, layer, xs, tm, _pick_tile(d, 512))

    proj, qkv1, qkv4, qkv16 = ar_in_proj(xs, norm_mix[0], ar_w_in, dil_q_gain[0], dil_k_gain[0], tm)
    cos, sin = _rotary_tables(s)
    ya = retention_mixer(proj, cos, sin, _pick_tile(s, 1024))
    yb = dilated_attention(qkv1, qkv4, qkv16, rel_bias, _pick_tile(s, 2048), 8)
    xs = matmul2_residual(ya, yb, ar_w_out, 0, xs, tm, _pick_tile(d, 1024))
    xs = tail(xs, 0)

    n_main = GDN_QKV + GDN_V_W
    w_gate = gdn_w_in[0, :, n_main:].astype(BF16)
    qkv, z, ab = gdn_in_proj(xs, norm_mix[1], gdn_w_in.astype(BF16), w_gate, gdn_conv[0], tm, 1024)
    beta, gcum, gct = gdn_gates(ab, gdn_a_log[0], gdn_dt_bias[0], _pick_tile(s, 1024))
    og = gdn_mixer(qkv, z, beta, gcum, gct, gdn_norm[0], 4 * GDN_CHUNK, 4)
    xs = matmul_residual(og, gdn_w_out, 0, xs, tm, _pick_tile(d, 512))
    xs = tail(xs, 1)
    return xs.reshape(b, s, d)
```

```python
import functools
import math

import jax
import jax.numpy as jnp
from jax import lax
from jax.experimental import pallas as pl
from jax.experimental.pallas import tpu as pltpu

F32 = jnp.float32
BF16 = jnp.bfloat16
EPS = 1e-6
NEG = -1e30

VMEM_LIMIT_BYTES = 56 * 1024 * 1024
LANES = 128

RET_HEADS = 4
RET_QK_DIM = 256
RET_V_DIM = 512
RET_CHUNK = 128
ROPE_BASE = 10000.0
DIL_HEADS = 8
DIL_DIM = 128
DIL_PATTERNS = ((128, 1), (512, 4), (2048, 16))
DIL_STEPS = 128
REL_BUCKETS = 32
REL_MAX_DIST = 2048
AR_QK = RET_HEADS * RET_QK_DIM
AR_V = RET_HEADS * RET_V_DIM
DIL_W = DIL_HEADS * DIL_DIM
GDN_K_HEADS = 16
GDN_V_HEADS = 32
GDN_DIM = 128
GDN_CONV = 4
GDN_CHUNK = 128
GDN_QK_W = GDN_K_HEADS * GDN_DIM
GDN_V_W = GDN_V_HEADS * GDN_DIM
GDN_QKV = 2 * GDN_QK_W + GDN_V_W
XA_HEADS = 4
XA_DIM = 128


def _params(*semantics):
    return pltpu.CompilerParams(dimension_semantics=semantics, vmem_limit_bytes=VMEM_LIMIT_BYTES)


def _dot(a, b):
    return jnp.dot(a, b, preferred_element_type=F32)


def _dot_nt(a, b):
    return lax.dot_general(a, b, (((1,), (1,)), ((), ())), preferred_element_type=F32)


def _dot_tn(a, b):
    return lax.dot_general(a, b, (((0,), (0,)), ((), ())), preferred_element_type=F32)


def _dot_f32(a, b):
    return jnp.dot(a, b, preferred_element_type=F32, precision=lax.Precision.HIGHEST)


def _dot_nt_f32(a, b):
    return lax.dot_general(a, b, (((1,), (1,)), ((), ())), preferred_element_type=F32,
                           precision=lax.Precision.HIGHEST)


def _rms_rows(x):
    return x * lax.rsqrt(jnp.mean(x * x, axis=-1, keepdims=True) + EPS)


def _silu(x):
    return x * jax.nn.sigmoid(x)


def _norm_swiglu_kernel(x_ref, g_ref, w1_ref, w3_ref, o_ref, h_ref):
    @pl.when(pl.program_id(1) == 0)
    def _():
        h_ref[...] = (_rms_rows(x_ref[...]) * g_ref[...]).astype(BF16)

    h = h_ref[...]
    a = _dot(h, w1_ref[...])
    b = _dot(h, w3_ref[...])
    o_ref[...] = (_silu(a) * b).astype(o_ref.dtype)


def norm_swiglu(x, gain, w1, w3, layer, tm, tn):
    m, k = x.shape
    n = w1.shape[2]
    return pl.pallas_call(
        _norm_swiglu_kernel,
        grid=(m // tm, n // tn),
        in_specs=[pl.BlockSpec((tm, k), lambda i, j: (i, 0)),
                  pl.BlockSpec((1, k), lambda i, j: (0, 0)),
                  pl.BlockSpec((None, k, tn), lambda i, j: (layer, 0, j)),
                  pl.BlockSpec((None, k, tn), lambda i, j: (layer, 0, j))],
        out_specs=pl.BlockSpec((tm, tn), lambda i, j: (i, j)),
        out_shape=jax.ShapeDtypeStruct((m, n), BF16),
        scratch_shapes=[pltpu.VMEM((tm, k), BF16)],
        compiler_params=_params("parallel", "arbitrary"),
        name="norm_swiglu",
    )(x, gain.reshape(1, k), w1, w3)


def _matmul_residual_kernel(y_ref, w_ref, x_ref, o_ref):
    o_ref[...] = x_ref[...] + _dot(y_ref[...], w_ref[...])


def matmul_residual(y, w, layer, x, tm, tn):
    m, k = y.shape
    n = w.shape[2]
    return pl.pallas_call(
        _matmul_residual_kernel,
        grid=(m // tm, n // tn),
        in_specs=[pl.BlockSpec((tm, k), lambda i, j: (i, 0)),
                  pl.BlockSpec((None, k, tn), lambda i, j: (layer, 0, j)),
                  pl.BlockSpec((tm, tn), lambda i, j: (i, j))],
        out_specs=pl.BlockSpec((tm, tn), lambda i, j: (i, j)),
        out_shape=jax.ShapeDtypeStruct((m, n), F32),
        compiler_params=_params("parallel", "parallel"),
        name="matmul_residual",
    )(y, w, x)


def _matmul2_residual_kernel(ya_ref, yb_ref, wa_ref, wb_ref, x_ref, o_ref):
    o_ref[...] = x_ref[...] + _dot(ya_ref[...], wa_ref[...]) + _dot(yb_ref[...], wb_ref[...])


def matmul2_residual(ya, yb, w, layer, x, tm, tn):
    m, ka = ya.shape
    kb = yb.shape[1]
    n = w.shape[2]
    assert ka % kb == 0
    return pl.pallas_call(
        _matmul2_residual_kernel,
        grid=(m // tm, n // tn),
        in_specs=[pl.BlockSpec((tm, ka), lambda i, j: (i, 0)),
                  pl.BlockSpec((tm, kb), lambda i, j: (i, 0)),
                  pl.BlockSpec((None, ka, tn), lambda i, j: (layer, 0, j)),
                  pl.BlockSpec((None, kb, tn), lambda i, j: (layer, ka // kb, j)),
                  pl.BlockSpec((tm, tn), lambda i, j: (i, j))],
        out_specs=pl.BlockSpec((tm, tn), lambda i, j: (i, j)),
        out_shape=jax.ShapeDtypeStruct((m, n), F32),
        compiler_params=_params("parallel", "parallel"),
        name="matmul2_residual",
    )(ya, yb, w, w, x)


def _retention_kernel(lg_ref, q_ref, k_ref, v_ref, g_ref, cos_ref, sin_ref, o_ref, state_ref,
                      *, chunk, n_chunks):
    @pl.when(pl.program_id(1) == 0)
    def _():
        state_ref[...] = jnp.zeros_like(state_ref)

    c = chunk
    half = RET_QK_DIM // 2
    lg = lg_ref[0]
    lg_l = lg[:, :LANES]
    ii = lax.broadcasted_iota(jnp.int32, (c, c), 0)
    jj = lax.broadcasted_iota(jnp.int32, (c, c), 1)
    rel = (ii - jj).astype(F32)
    inner = jnp.where(rel >= 0, jnp.exp(lg[:, :c] * jnp.maximum(rel, 0.0)), 0.0)
    idx = lax.broadcasted_iota(jnp.int32, (c, LANES), 0).astype(F32)
    q_dec = jnp.exp(lg_l * (idx + 1.0))
    k_dec = jnp.exp(lg_l * (c - 1.0 - idx))
    c_dec = jnp.exp(lg * float(c))
    k_scale = RET_QK_DIM ** -0.5

    def rot(t, cos, sin):
        t1, t2 = t[:, :half], t[:, half:]
        return t1 * cos - t2 * sin, t1 * sin + t2 * cos

    rows = [pl.ds(ci * c, c) for ci in range(n_chunks)]
    qb, kb, qd, kd = [], [], [], []
    for r in rows:
        cos = cos_ref[r, :]
        sin = sin_ref[r, :]
        q1, q2 = rot(q_ref[r, :].astype(F32), cos, sin)
        k1, k2 = rot(k_ref[r, :].astype(F32) * k_scale, cos, sin)
        qb.append(jnp.concatenate([q1, q2], axis=-1).astype(BF16))
        kb.append(jnp.concatenate([k1, k2], axis=-1).astype(BF16))
        qd.append(jnp.concatenate([q1 * q_dec, q2 * q_dec], axis=-1).astype(BF16))
        kd.append(jnp.concatenate([k1 * k_dec, k2 * k_dec], axis=-1).astype(BF16))
    scores = [(_dot_nt(a, b) * inner).astype(BF16) for a, b in zip(qb, kb)]
    kv = [_dot_tn(a, v_ref[r, :]) for a, r in zip(kd, rows)]
    states = [state_ref[...]]
    for inc in kv:
        states.append(states[-1] * c_dec + inc)
    state_ref[...] = states[-1]
    outs = [_dot(sc, v_ref[r, :]) + _dot(a, st.astype(BF16))
            for sc, r, a, st in zip(scores, rows, qd, states)]
    for o, r in zip(outs, rows):
        gate = _silu(g_ref[r, :].astype(F32))
        o_ref[r, :] = (_rms_rows(o) * gate).astype(o_ref.dtype)


def retention_mixer(proj, cos, sin, rows_per_step):
    s = proj.shape[0]
    t = rows_per_step
    log_gamma = jnp.log(1.0 - 2.0 ** (-5.0 - jnp.arange(RET_HEADS, dtype=F32)))
    lg = jnp.broadcast_to(log_gamma[:, None, None], (RET_HEADS, 1, RET_V_DIM))
    qk_blocks = AR_QK // RET_QK_DIM
    v_off = 2 * AR_QK // RET_V_DIM
    g_off = v_off + AR_V // RET_V_DIM
    kern = functools.partial(_retention_kernel, chunk=RET_CHUNK, n_chunks=t // RET_CHUNK)
    return pl.pallas_call(
        kern,
        grid=(RET_HEADS, s // t),
        in_specs=[pl.BlockSpec((1, 1, RET_V_DIM), lambda h, i: (h, 0, 0)),
                  pl.BlockSpec((t, RET_QK_DIM), lambda h, i: (i, h)),
                  pl.BlockSpec((t, RET_QK_DIM), lambda h, i: (i, qk_blocks + h)),
                  pl.BlockSpec((t, RET_V_DIM), lambda h, i: (i, v_off + h)),
                  pl.BlockSpec((t, RET_V_DIM), lambda h, i: (i, g_off + h)),
                  pl.BlockSpec((t, RET_QK_DIM // 2), lambda h, i: (i, 0)),
                  pl.BlockSpec((t, RET_QK_DIM // 2), lambda h, i: (i, 0))],
        out_specs=pl.BlockSpec((t, RET_V_DIM), lambda h, i: (i, h)),
        out_shape=jax.ShapeDtypeStruct((s, AR_V), BF16),
        scratch_shapes=[pltpu.VMEM((RET_QK_DIM, RET_V_DIM), F32)],
        compiler_params=_params("parallel", "arbitrary"),
        name="retention",
    )(lg, proj, proj, proj, proj, cos, sin)


def _ar_in_proj_kernel(x_ref, g_ref, w_ref, gn_ref, oa_ref, o1_ref, o4_ref, o16_ref,
                       h_ref, buf, nb, nb4, *, n_a):
    j = pl.program_id(1)
    tm, tn = buf.shape
    dd = DIL_DIM

    @pl.when(j == 0)
    def _():
        h_ref[...] = (_rms_rows(x_ref[...]) * g_ref[...]).astype(BF16)

    @pl.when(j < n_a)
    def _():
        oa_ref[...] = _dot(h_ref[...], w_ref[...]).astype(oa_ref.dtype)

    def head_major(normed):
        buf[...] = _dot(h_ref[...], w_ref[...])
        for hd in range(tn // dd):
            slot = hd % 2
            x = buf[:, hd * dd:(hd + 1) * dd]
            nb[slot] = _rms_rows(x) * gn_ref[...] if normed else x
            o1_ref[hd] = nb[slot].astype(o1_ref.dtype)
            for r4 in range(4):
                stream = nb[slot, pl.ds(r4, tm // 4, stride=4), :]
                nb4[slot, r4] = stream
                o4_ref[hd, :, r4 * dd:(r4 + 1) * dd] = stream.astype(o4_ref.dtype)
            for r in range(16):
                sub = nb4[slot, r % 4, pl.ds(r // 4, tm // 16, stride=4), :]
                o16_ref[hd, :, r * dd:(r + 1) * dd] = sub.astype(o16_ref.dtype)

    @pl.when((j >= n_a) & (j < n_a + 2))
    def _():
        head_major(True)

    @pl.when(j == n_a + 2)
    def _():
        head_major(False)


def ar_in_proj(x, gain, w, q_gain, k_gain, tm):
    m, k = x.shape
    dd = DIL_DIM
    tn = DIL_W
    n_a = (2 * AR_QK + 2 * AR_V) // tn
    nh = 3 * DIL_HEADS
    gains = jnp.stack([q_gain * (dd ** -0.5), k_gain, jnp.ones_like(k_gain)]).reshape(3, 1, dd)

    def grp(j):
        return jnp.clip(j - n_a, 0, 2)

    return pl.pallas_call(
        functools.partial(_ar_in_proj_kernel, n_a=n_a),
        grid=(m // tm, n_a + 3),
        in_specs=[pl.BlockSpec((tm, k), lambda i, j: (i, 0)),
                  pl.BlockSpec((1, k), lambda i, j: (0, 0)),
                  pl.BlockSpec((None, k, tn), lambda i, j: (0, 0, j)),
                  pl.BlockSpec((None, 1, dd), lambda i, j: (grp(j), 0, 0))],
        out_specs=[pl.BlockSpec((tm, tn), lambda i, j: (i, jnp.minimum(j, n_a - 1)))]
        + [pl.BlockSpec((DIL_HEADS, tm // d, d * dd), lambda i, j: (grp(j), i, 0)) for d in (1, 4, 16)],
        out_shape=[jax.ShapeDtypeStruct((m, n_a * tn), BF16)]
        + [jax.ShapeDtypeStruct((nh, m // d, d * dd), BF16) for d in (1, 4, 16)],
        scratch_shapes=[pltpu.VMEM((tm, k), BF16),
                        pltpu.VMEM((tm, tn), F32),
                        pltpu.VMEM((2, tm, dd), F32),
                        pltpu.VMEM((2, 4, tm // 4, dd), F32)],
        compiler_params=_params("arbitrary", "arbitrary"),
        name="ar_in_proj",
    )(x, gain.reshape(1, k), w, gains)


def _dilated_kernel(bias_ref,
                    q1, k1, k1p, v1, v1p,
                    q4, k4, k4p, v4, v4p,
                    q16, k16, k16p, v16, v16p,
                    o_ref, acc_ref, m_ref, l_ref, *, tile, group):
    n = DIL_STEPS
    d_dim = DIL_DIM
    first = pl.program_id(1) == 0

    def run_group(bias_idx, blocks, mode):
        def ld(p):
            return p[0][p[1]]

        bias = bias_ref[bias_idx]
        in_prev = lax.broadcasted_iota(jnp.int32, (n, 2 * n), 1) < n
        keys = [jnp.concatenate([ld(b[1]), ld(b[2])], axis=0) for b in blocks]
        vals = [jnp.concatenate([ld(b[3]), ld(b[4])], axis=0) for b in blocks]
        sc = [_dot_nt(ld(b[0]), kk) + bias for b, kk in zip(blocks, keys)]
        sc = [jnp.where(first & in_prev, NEG, x) if b[5] else x for x, b in zip(sc, blocks)]
        m = [jnp.max(x, axis=-1, keepdims=True) for x in sc]
        pr = [jnp.exp(x - mm) for x, mm in zip(sc, m)]
        l = [jnp.sum(x, axis=-1, keepdims=True) for x in pr]
        o = [_dot(x.astype(BF16), vv) for x, vv in zip(pr, vals)]
        for b, mm, ll, oo in zip(blocks, m, l, o):
            rows = b[6]
            if mode == "init":
                m_ref[rows, :] = jnp.broadcast_to(mm, (n, d_dim))
                l_ref[rows, :] = jnp.broadcast_to(ll, (n, d_dim))
                acc_ref[rows, :] = oo
                continue
            m_old = m_ref[rows, :]
            m_new = jnp.maximum(m_old, mm)
            a_old = jnp.exp(m_old - m_new)
            a_cur = jnp.exp(mm - m_new)
            l_new = l_ref[rows, :] * a_old + ll * a_cur
            acc_new = acc_ref[rows, :] * a_old + oo * a_cur
            if mode == "merge":
                m_ref[rows, :] = m_new
                l_ref[rows, :] = l_new
                acc_ref[rows, :] = acc_new
            else:
                o_ref[rows, :] = (acc_new / l_new).astype(o_ref.dtype)

    every = slice(None)

    blocks = []
    for r in range(16):
        ls = slice(r * d_dim, (r + 1) * d_dim)
        blocks.append(((q16, (every, ls)), (k16p, (every, ls)), (k16, (every, ls)),
                       (v16p, (every, ls)), (v16, (every, ls)), True, pl.ds(r, n, stride=16)))
    for g0 in range(0, len(blocks), group):
        run_group(2, blocks[g0:g0 + group], "init")

    blocks = []
    for r in range(4):
        ls = slice(r * d_dim, (r + 1) * d_dim)
        for b in range(tile // (4 * n)):
            cur = pl.ds(b * n, n)
            prev = pl.ds((b - 1) * n, n)
            kp, vp = (((k4p, (every, ls)), (v4p, (every, ls))) if b == 0
                      else ((k4, (prev, ls)), (v4, (prev, ls))))
            blocks.append(((q4, (cur, ls)), kp, (k4, (cur, ls)), vp, (v4, (cur, ls)), b == 0,
                           pl.ds(b * 4 * n + r, n, stride=4)))
    for g0 in range(0, len(blocks), group):
        run_group(1, blocks[g0:g0 + group], "merge")

    blocks = []
    for b in range(tile // n):
        cur = pl.ds(b * n, n)
        prev = pl.ds((b - 1) * n, n)
        kp, vp = (((k1p, (every, every)), (v1p, (every, every))) if b == 0
                  else ((k1, (prev, every)), (v1, (prev, every))))
        blocks.append(((q1, (cur, every)), kp, (k1, (cur, every)), vp, (v1, (cur, every)), b == 0, cur))
    for g0 in range(0, len(blocks), group):
        run_group(0, blocks[g0:g0 + group], "final")


def _t5_bucket(dist):
    exact = REL_BUCKETS // 2
    large = exact + (jnp.log(jnp.maximum(dist, exact).astype(F32) / exact)
                     / math.log(REL_MAX_DIST / exact) * (REL_BUCKETS - exact)).astype(jnp.int32)
    large = jnp.minimum(large, REL_BUCKETS - 1)
    return jnp.where(dist < exact, dist, large)


def _dilated_bias(rel_bias):
    n = DIL_STEPS
    steps = jnp.arange(n + 1)
    vec = jnp.stack([rel_bias.astype(F32)[_t5_bucket(steps * dil)] for (_, dil) in DIL_PATTERNS])
    vec = vec.transpose(0, 2, 1)
    width = 3 * n
    p = jnp.full(vec.shape[:2] + (width,), NEG, F32)
    p = p.at[:, :, n - 1:2 * n].set(vec[:, :, ::-1])
    tiled = jnp.broadcast_to(p[:, :, None, :], vec.shape[:2] + (n, width))
    skew = tiled.reshape(vec.shape[:2] + (n * width,))[:, :, :n * (width - 1)]
    skew = skew.reshape(vec.shape[:2] + (n, width - 1))
    return skew[:, :, :, n - 1:3 * n - 1]


def dilated_attention(qkv1, qkv4, qkv16, rel_bias, tile, group):
    _, s, dd = qkv1.shape
    n = DIL_STEPS
    h_n = DIL_HEADS
    bias = _dilated_bias(rel_bias)
    args, specs = [bias], [pl.BlockSpec((len(DIL_PATTERNS), None, n, 2 * n), lambda h, i: (0, h, 0, 0))]
    for dil, view in ((1, qkv1), (4, qkv4), (16, qkv16)):
        rows = tile // dil
        per = rows // n

        def cur_spec(off, rows=rows, dil=dil):
            return pl.BlockSpec((None, rows, dil * dd), lambda h, i: (off + h, i, 0))

        def prev_spec(off, per=per, dil=dil):
            return pl.BlockSpec((None, n, dil * dd), lambda h, i: (off + h, jnp.maximum(i * per - 1, 0), 0))

        args += [view, view, view, view, view]
        specs += [cur_spec(0), cur_spec(h_n), prev_spec(h_n), cur_spec(2 * h_n), prev_spec(2 * h_n)]
    return pl.pallas_call(
        functools.partial(_dilated_kernel, tile=tile, group=group),
        grid=(h_n, s // tile),
        in_specs=specs,
        out_specs=pl.BlockSpec((tile, dd), lambda h, i: (i, h)),
        out_shape=jax.ShapeDtypeStruct((s, h_n * dd), BF16),
        scratch_shapes=[pltpu.VMEM((tile, dd), F32)] * 3,
        compiler_params=_params("parallel", "arbitrary"),
        name="dilated_attention",
    )(*args)


def _gdn_in_proj_kernel(x_ref, g_ref, w_ref, ws_ref, cw_ref, oc_ref, oz_ref, os_ref,
                        h_ref, buf, tails, *, n_norm, n_conv, q_tiles):
    i = pl.program_id(0)
    j = pl.program_id(1)
    tm = buf.shape[0] - 8
    tn = buf.shape[1]
    halo = 8
    sub = 128
    dd = GDN_DIM

    @pl.when((i == 0) & (j == 0))
    def _():
        tails[...] = jnp.zeros_like(tails)

    @pl.when(j == 0)
    def _():
        h = (_rms_rows(x_ref[...]) * g_ref[...]).astype(BF16)
        h_ref[...] = h
        os_ref[...] = _dot(h, ws_ref[...])

    def conv_tile(l2norm):
        buf[halo:halo + tm, :] = _dot(h_ref[...], w_ref[...])
        buf[0:halo, :] = tails[j]
        tails[j] = buf[tm:tm + halo, :]
        scale = jnp.where(j < q_tiles, dd ** -0.5, 1.0)
        for r0 in range(0, tm, sub):
            y = buf[halo + r0:halo + r0 + sub, :] * cw_ref[GDN_CONV - 1:GDN_CONV, :]
            for t in range(GDN_CONV - 1):
                off = halo + r0 - (GDN_CONV - 1) + t
                y = y + buf[off:off + sub, :] * cw_ref[t:t + 1, :]
            y = _silu(y)
            if l2norm:
                for h0 in range(0, tn, dd):
                    yh = y[:, h0:h0 + dd]
                    yh = yh * (lax.rsqrt(jnp.sum(yh * yh, axis=-1, keepdims=True) + EPS) * scale)
                    oc_ref[r0:r0 + sub, h0:h0 + dd] = yh.astype(oc_ref.dtype)
            else:
                oc_ref[r0:r0 + sub, :] = y.astype(oc_ref.dtype)

    @pl.when(j < n_norm)
    def _():
        conv_tile(True)

    @pl.when((j >= n_norm) & (j < n_conv))
    def _():
        conv_tile(False)

    @pl.when(j >= n_conv)
    def _():
        oz_ref[...] = _dot(h_ref[...], w_ref[...]).astype(oz_ref.dtype)


def gdn_in_proj(x, gain, w, w_side, conv_w, tm, tn):
    m, k = x.shape
    ns = w_side.shape[1]
    n_conv = GDN_QKV // tn
    n_norm = 2 * GDN_QK_W // tn
    n_tiles = (GDN_QKV + GDN_V_W) // tn
    kern = functools.partial(_gdn_in_proj_kernel, n_norm=n_norm, n_conv=n_conv, q_tiles=GDN_QK_W // tn)
    return pl.pallas_call(
        kern,
        grid=(m // tm, n_tiles),
        in_specs=[pl.BlockSpec((tm, k), lambda i, j: (i, 0)),
                  pl.BlockSpec((1, k), lambda i, j: (0, 0)),
                  pl.BlockSpec((None, k, tn), lambda i, j: (0, 0, j)),
                  pl.BlockSpec((k, ns), lambda i, j: (0, 0)),
                  pl.BlockSpec((GDN_CONV, tn), lambda i, j: (0, jnp.minimum(j, n_conv - 1)))],
        out_specs=[pl.BlockSpec((tm, tn), lambda i, j: (i, jnp.minimum(j, n_conv - 1))),
                   pl.BlockSpec((tm, tn), lambda i, j: (i, jnp.maximum(j - n_conv, 0))),
                   pl.BlockSpec((tm, ns), lambda i, j: (i, 0))],
        out_shape=[jax.ShapeDtypeStruct((m, GDN_QKV), BF16),
                   jax.ShapeDtypeStruct((m, GDN_V_W), BF16),
                   jax.ShapeDtypeStruct((m, ns), F32)],
        scratch_shapes=[pltpu.VMEM((tm, k), BF16),
                        pltpu.VMEM((8 + tm, tn), F32),
                        pltpu.VMEM((n_conv, 8, tn), F32)],
        compiler_params=_params("arbitrary", "arbitrary"),
        name="gdn_in_proj",
    )(x, gain.reshape(1, k), w, w_side, conv_w)


def _gdn_gates_kernel(ab_ref, alog_ref, dt_ref, beta_ref, gcum_ref, gct_ref, *, chunk):
    t = ab_ref.shape[0]
    hv = GDN_V_HEADS
    ab = ab_ref[...]
    beta_ref[...] = jax.nn.sigmoid(ab[:, :hv])
    z = ab[:, hv:] + dt_ref[...]
    softplus = jnp.maximum(z, 0.0) + jnp.log1p(jnp.exp(-jnp.abs(z)))
    g = -jnp.exp(alog_ref[...]) * softplus
    ii = lax.broadcasted_iota(jnp.int32, (chunk, chunk), 0)
    jj = lax.broadcasted_iota(jnp.int32, (chunk, chunk), 1)
    tri = (ii >= jj).astype(F32)
    eh = (lax.broadcasted_iota(jnp.int32, (hv, hv), 0) == lax.broadcasted_iota(jnp.int32, (hv, hv), 1)).astype(F32)
    gcs = [_dot_f32(tri, g[r0:r0 + chunk]) for r0 in range(0, t, chunk)]
    gts = [_dot_nt_f32(eh, gc) for gc in gcs]
    for n, (gc, gt) in enumerate(zip(gcs, gts)):
        gcum_ref[n * chunk:(n + 1) * chunk, :] = gc
        gct_ref[:, n * chunk:(n + 1) * chunk] = gt


def gdn_gates(ab, a_log, dt_bias, tm):
    s = ab.shape[0]
    hv = GDN_V_HEADS
    return pl.pallas_call(
        functools.partial(_gdn_gates_kernel, chunk=GDN_CHUNK),
        grid=(s // tm,),
        in_specs=[pl.BlockSpec((tm, 2 * hv), lambda i: (i, 0)),
                  pl.BlockSpec((1, hv), lambda i: (0, 0)),
                  pl.BlockSpec((1, hv), lambda i: (0, 0))],
        out_specs=[pl.BlockSpec((tm, hv), lambda i: (i, 0)),
                   pl.BlockSpec((tm, hv), lambda i: (i, 0)),
                   pl.BlockSpec((hv, tm), lambda i: (0, i))],
        out_shape=[jax.ShapeDtypeStruct((s, hv), F32), jax.ShapeDtypeStruct((s, hv), F32),
                   jax.ShapeDtypeStruct((hv, s), F32)],
        compiler_params=_params("parallel"),
        name="gdn_gates",
    )(ab, a_log.reshape(1, hv), dt_bias.reshape(1, hv))


def _unit_lower_inverses(a_list):
    c = a_list[0].shape[0]
    ii = lax.broadcasted_iota(jnp.int32, (c, c), 0)
    jj = lax.broadcasted_iota(jnp.int32, (c, c), 1)
    eye = jnp.where(ii == jj, 1.0, 0.0)
    p = [-a for a in a_list]
    t = [eye + x for x in p]
    pb = [x.astype(BF16) for x in p]
    p = [_dot(x, x) for x in pb]
    span = 4
    while span < c:
        pb = [x.astype(BF16) for x in p]
        st = [_dot(jnp.concatenate([x, y.astype(BF16)], axis=0), x) for x, y in zip(pb, t)]
        p = [x[:c] for x in st]
        t = [y + x[c:] for x, y in zip(st, t)]
        span *= 2
    pb = [x.astype(BF16) for x in p]
    return [y + _dot(y.astype(BF16), x) for x, y in zip(pb, t)]


def _gdn_kernel(q_ref, k_ref, v_ref, z_ref, beta_ref, gcum_ref, gct_ref, ng_ref, o_ref, state_ref,
                *, tb, chunk, kheads):
    grp = pl.program_id(0)
    c = chunk
    dd = GDN_DIM
    nh = 2 * kheads
    nc = tb // c

    @pl.when(pl.program_id(1) == 0)
    def _():
        state_ref[...] = jnp.zeros_like(state_ref)

    lane_h = lax.broadcasted_iota(jnp.int32, (1, GDN_V_HEADS), 1)
    ii = lax.broadcasted_iota(jnp.int32, (c, c), 0)
    jj = lax.broadcasted_iota(jnp.int32, (c, c), 1)
    causal = ii >= jj
    strict = ii > jj
    ng = ng_ref[...]

    def head_cols(h):
        return slice(h * dd, (h + 1) * dd)

    for ci in range(nc):
        r = slice(ci * c, (ci + 1) * c)
        beta_t = beta_ref[r, :]
        gcum_t = gcum_ref[r, :]
        beta_col, gc_col, gc_row = [], [], []
        for h in range(nh):
            hv = grp * nh + h
            sel = lane_h == hv
            beta_col.append(jnp.sum(jnp.where(sel, beta_t, 0.0), axis=-1, keepdims=True))
            gc_col.append(jnp.sum(jnp.where(sel, gcum_t, 0.0), axis=-1, keepdims=True))
            gc_row.append(gct_ref[pl.ds(hv, 1), :][:, r])
        eg_col = [jnp.exp(x) for x in gc_col]

        k16 = [k_ref[r, head_cols(kh)] for kh in range(kheads)]
        q16 = [q_ref[r, head_cols(kh)] for kh in range(kheads)]
        kf = [x.astype(F32) for x in k16]
        qkk = [_dot_nt(jnp.concatenate([q, k], axis=0), k) for q, k in zip(q16, k16)]

        decay, a_list = [], []
        for h in range(nh):
            dlog = gc_col[h] - gc_row[h]
            dec = jnp.where(causal, jnp.exp(jnp.where(causal, dlog, 0.0)), 0.0)
            decay.append(dec)
            a_list.append(jnp.where(strict, qkk[h // 2][c:] * beta_col[h] * dec, 0.0))
        t_list = _unit_lower_inverses(a_list)

        uw = []
        for h in range(nh):
            rhs = jnp.concatenate([v_ref[r, head_cols(h)].astype(F32) * beta_col[h],
                                   kf[h // 2] * (beta_col[h] * eg_col[h])], axis=1)
            uw.append(_dot(t_list[h].astype(BF16), rhs.astype(BF16)))

        states = [state_ref[h] for h in range(nh)]
        s16 = [s.astype(BF16) for s in states]
        wq = []
        for h in range(nh):
            lhs = jnp.concatenate([uw[h][:, dd:], q16[h // 2].astype(F32) * eg_col[h]], axis=0)
            wq.append(_dot(lhs.astype(BF16), s16[h]))
        vn16 = [(uw[h][:, :dd] - wq[h][:c]).astype(BF16) for h in range(nh)]
        outs = []
        for h in range(nh):
            intra = jnp.where(causal, qkk[h // 2][:c] * decay[h], 0.0)
            outs.append(wq[h][c:] + _dot(intra.astype(BF16), vn16[h]))
        for h in range(nh):
            g_last = gc_col[h][c - 1:c, :]
            kd = kf[h // 2] * jnp.exp(g_last - gc_col[h])
            state_ref[h] = states[h] * jnp.exp(g_last) + _dot_tn(kd.astype(BF16), vn16[h])
        for h in range(nh):
            zg = _silu(z_ref[r, head_cols(h)].astype(F32))
            o_ref[r, head_cols(h)] = (_rms_rows(outs[h]) * ng * zg).astype(o_ref.dtype)


def gdn_mixer(qkv, z, beta, gcum, gct, norm_gain, tb, kheads):
    s = qkv.shape[0]
    dd = GDN_DIM
    wk = kheads * dd
    wv = 2 * wk
    k0 = GDN_QK_W // wk
    v0 = 2 * GDN_QK_W // wv
    return pl.pallas_call(
        functools.partial(_gdn_kernel, tb=tb, chunk=GDN_CHUNK, kheads=kheads),
        grid=(GDN_K_HEADS // kheads, s // tb),
        in_specs=[pl.BlockSpec((tb, wk), lambda g, i: (i, g)),
                  pl.BlockSpec((tb, wk), lambda g, i: (i, k0 + g)),
                  pl.BlockSpec((tb, wv), lambda g, i: (i, v0 + g)),
                  pl.BlockSpec((tb, wv), lambda g, i: (i, g)),
                  pl.BlockSpec((tb, GDN_V_HEADS), lambda g, i: (i, 0)),
                  pl.BlockSpec((tb, GDN_V_HEADS), lambda g, i: (i, 0)),
                  pl.BlockSpec((GDN_V_HEADS, tb), lambda g, i: (0, i)),
                  pl.BlockSpec((1, dd), lambda g, i: (0, 0))],
        out_specs=pl.BlockSpec((tb, wv), lambda g, i: (i, g)),
        out_shape=jax.ShapeDtypeStruct((s, GDN_V_W), BF16),
        scratch_shapes=[pltpu.VMEM((2 * kheads, dd, dd), F32)],
        compiler_params=_params("parallel", "arbitrary"),
        name="gated_deltanet",
    )(qkv, qkv, qkv, z, beta, gcum, gct, norm_gain.reshape(1, dd))


def _xa_kv_kernel(mem_ref, mg_ref, w_ref, kg_ref, k_ref, v_ref):
    mem_n = (_rms_rows(mem_ref[...]) * mg_ref[...]).astype(BF16)
    kv = _dot(mem_n, w_ref[...])
    for h in range(XA_HEADS):
        k = kv[:, h * XA_DIM:(h + 1) * XA_DIM]
        k_ref[h] = (_rms_rows(k) * kg_ref[...]).astype(k_ref.dtype)
        v_ref[h] = kv[:, (XA_HEADS + h) * XA_DIM:(XA_HEADS + h + 1) * XA_DIM].astype(v_ref.dtype)


def xa_keys_values(mem, mem_gain, w_kv, layer, k_gain):
    m, d = mem.shape
    shape = jax.ShapeDtypeStruct((XA_HEADS, m, XA_DIM), BF16)
    nkv = w_kv.shape[2]
    return pl.pallas_call(
        _xa_kv_kernel,
        grid=(1,),
        in_specs=[pl.BlockSpec((m, d), lambda i: (0, 0)),
                  pl.BlockSpec((1, d), lambda i: (0, 0)),
                  pl.BlockSpec((None, d, nkv), lambda i: (layer, 0, 0)),
                  pl.BlockSpec((1, XA_DIM), lambda i: (0, 0))],
        out_specs=[pl.BlockSpec((XA_HEADS, m, XA_DIM), lambda i: (0, 0, 0))] * 2,
        out_shape=[shape, shape],
        compiler_params=_params("arbitrary"),
        name="xa_keys_values",
    )(mem, mem_gain.reshape(1, d), w_kv, k_gain.reshape(1, XA_DIM))


def _xattn_kernel(x_ref, g_ref, wq_ref, qg_ref, k_ref, v_ref, wo_ref, o_ref):
    x = x_ref[...]
    h = (_rms_rows(x) * g_ref[...]).astype(BF16)
    q = _dot(h, wq_ref[...])
    qg = qg_ref[...] * (XA_DIM ** -0.5)
    heads = range(XA_HEADS)
    qh = [(_rms_rows(q[:, hd * XA_DIM:(hd + 1) * XA_DIM]) * qg).astype(BF16) for hd in heads]
    sc = [_dot_nt(qh[hd], k_ref[hd]) for hd in heads]
    pr = [jnp.exp(s - jnp.max(s, axis=-1, keepdims=True)) for s in sc]
    pr = [p / jnp.sum(p, axis=-1, keepdims=True) for p in pr]
    outs = [_dot(pr[hd].astype(BF16), v_ref[hd]).astype(BF16) for hd in heads]
    o = jnp.concatenate(outs, axis=-1)
    o_ref[...] = x + _dot(o, wo_ref[...])


def cross_attention(x, gain, w_q, q_gain, k, v, w_o, layer, tm):
    s, d = x.shape
    hw = XA_HEADS * XA_DIM
    m = k.shape[1]
    return pl.pallas_call(
        _xattn_kernel,
        grid=(s // tm,),
        in_specs=[pl.BlockSpec((tm, d), lambda i: (i, 0)),
                  pl.BlockSpec((1, d), lambda i: (0, 0)),
                  pl.BlockSpec((None, d, hw), lambda i: (layer, 0, 0)),
                  pl.BlockSpec((1, XA_DIM), lambda i: (0, 0)),
                  pl.BlockSpec((XA_HEADS, m, XA_DIM), lambda i: (0, 0, 0)),
                  pl.BlockSpec((XA_HEADS, m, XA_DIM), lambda i: (0, 0, 0)),
                  pl.BlockSpec((None, hw, d), lambda i: (layer, 0, 0))],
        out_specs=pl.BlockSpec((tm, d), lambda i: (i, 0)),
        out_shape=jax.ShapeDtypeStruct((s, d), F32),
        compiler_params=_params("parallel"),
        name="cross_attention",
    )(x, gain.reshape(1, d), w_q, q_gain.reshape(1, XA_DIM), k, v, w_o)


def _pick_tile(total, target):
    t = min(total, target)
    while total % t:
        t //= 2
    return t


def _rotary_tables(s):
    inv = ROPE_BASE ** (-jnp.arange(0, RET_QK_DIM, 2, dtype=F32) / RET_QK_DIM)
    ang = jnp.arange(s).astype(F32)[:, None] * inv[None, :]
    return jnp.cos(ang), jnp.sin(ang)


def kernel(x, mem, norm_mix, norm_xa, norm_ffn, mem_norm, rel_bias, ar_w_in, ar_w_out, dil_q_gain, dil_k_gain, gdn_w_in, gdn_conv, gdn_a_log, gdn_dt_bias, gdn_norm, gdn_w_out, xa_w_q, xa_w_kv, xa_w_o, xa_q_gain, xa_k_gain, ffn_w1, ffn_w3, ffn_w2):
    b, s, d = x.shape
    assert b == 1
    xs = x.reshape(s, d)
    mem2 = mem.reshape(mem.shape[1], d)
    tm = _pick_tile(s, 1024)
    ar_w_in, ar_w_out, gdn_w_out, xa_w_q, xa_w_kv, xa_w_o, ffn_w1, ffn_w3, ffn_w2 = (
        w.astype(BF16) for w in (ar_w_in, ar_w_out, gdn_w_out, xa_w_q, xa_w_kv, xa_w_o, ffn_w1, ffn_w3, ffn_w2))

    def tail(xs, layer):
        k, v = xa_keys_values(mem2, mem_norm, xa_w_kv, layer, xa_k_gain[layer])
        xs = cross_attention(xs, norm_xa[layer], xa_w_q, xa_q_gain[layer], k, v, xa_w_o, layer,
                             _pick_tile(s, 512))
        hid = norm_swiglu(xs, norm_ffn[layer], ffn_w1, ffn_w3, layer, tm, _pick_tile(ffn_w1.shape[-1], 512))
        return matmul_residual(hid, ffn_w2, layer, xs, tm, _pick_tile(d, 512))

    proj, qkv1, qkv4, qkv16 = ar_in_proj(xs, norm_mix[0], ar_w_in, dil_q_gain[0], dil_k_gain[0], tm)
    cos, sin = _rotary_tables(s)
    ya = retention_mixer(proj, cos, sin, _pick_tile(s, 1024))
    yb = dilated_attention(qkv1, qkv4, qkv16, rel_bias, _pick_tile(s, 2048), 8)
    xs = matmul2_residual(ya, yb, ar_w_out, 0, xs, tm, _pick_tile(d, 1024))
    xs = tail(xs, 0)

    n_main = GDN_QKV + GDN_V_W
    w_gate = gdn_w_in[0, :, n_main:].astype(BF16)
    qkv, z, ab = gdn_in_proj(xs, norm_mix[1], gdn_w_in.astype(BF16), w_gate, gdn_conv[0], tm, 1024)
    beta, gcum, gct = gdn_gates(ab, gdn_a_log[0], gdn_dt_bias[0], _pick_tile(s, 1024))
    og = gdn_mixer(qkv, z, beta, gcum, gct, gdn_norm[0], 4 * GDN_CHUNK, 4)
    xs = matmul_residual(og, gdn_w_out, 0, xs, tm, _pick_tile(d, 512))
    xs = tail(xs, 1)
    return xs.reshape(b, s, d)
```

```python
import functools
import math

import jax
import jax.numpy as jnp
from jax import lax
from jax.experimental import pallas as pl
from jax.experimental.pallas import tpu as pltpu

F32 = jnp.float32
BF16 = jnp.bfloat16
EPS = 1e-6
NEG = -1e30

VMEM_LIMIT_BYTES = 58 * 1024 * 1024
LANES = 128

RET_HEADS = 4
RET_QK_DIM = 256
RET_V_DIM = 512
RET_CHUNK = 128
ROPE_BASE = 10000.0
DIL_HEADS = 8
DIL_DIM = 128
DIL_PATTERNS = ((128, 1), (512, 4), (2048, 16))
DIL_STEPS = 128
REL_BUCKETS = 32
REL_MAX_DIST = 2048
AR_QK = RET_HEADS * RET_QK_DIM
AR_V = RET_HEADS * RET_V_DIM
DIL_W = DIL_HEADS * DIL_DIM
GDN_K_HEADS = 16
GDN_V_HEADS = 32
GDN_DIM = 128
GDN_CONV = 4
GDN_CHUNK = 128
GDN_QK_W = GDN_K_HEADS * GDN_DIM
GDN_V_W = GDN_V_HEADS * GDN_DIM
GDN_QKV = 2 * GDN_QK_W + GDN_V_W
XA_HEADS = 4
XA_DIM = 128


def _params(*semantics):
    return pltpu.CompilerParams(dimension_semantics=semantics, vmem_limit_bytes=VMEM_LIMIT_BYTES)


def _dot(a, b):
    return jnp.dot(a, b, preferred_element_type=F32)


def _dot_nt(a, b):
    return lax.dot_general(a, b, (((1,), (1,)), ((), ())), preferred_element_type=F32)


def _dot_tn(a, b):
    return lax.dot_general(a, b, (((0,), (0,)), ((), ())), preferred_element_type=F32)


def _dot_f32(a, b):
    return jnp.dot(a, b, preferred_element_type=F32, precision=lax.Precision.HIGHEST)


def _dot_nt_f32(a, b):
    return lax.dot_general(a, b, (((1,), (1,)), ((), ())), preferred_element_type=F32,
                           precision=lax.Precision.HIGHEST)


def _rms_rows(x):
    return x * lax.rsqrt(jnp.mean(x * x, axis=-1, keepdims=True) + EPS)


def _silu(x):
    return x * jax.nn.sigmoid(x)


def _with_side_cast(kernel, n_in, n_out):
    def wrapped(*refs):
        side_in, side_out = refs[n_in], refs[n_in + 1 + n_out]
        side_out[...] = side_in[...].astype(side_out.dtype)
        kernel(*refs[:n_in], *refs[n_in + 1:n_in + 1 + n_out], *refs[n_in + 2 + n_out:])
    return wrapped


def _side_cast_specs(side, steps, step_of):
    rows, cols = side.shape
    blk = 8 * (-(-rows // (8 * steps)))
    assert rows % blk == 0
    last = rows // blk - 1

    def index(*ids):
        return (jnp.minimum(step_of(*ids), last), 0)

    spec = pl.BlockSpec((blk, cols), index)
    return spec, spec, jax.ShapeDtypeStruct((rows, cols), BF16)


def _norm_swiglu_kernel(x_ref, g_ref, w1_ref, w3_ref, o_ref, h_ref):
    @pl.when(pl.program_id(1) == 0)
    def _():
        h_ref[...] = (_rms_rows(x_ref[...]) * g_ref[...]).astype(BF16)

    h = h_ref[...]
    a = _dot(h, w1_ref[...])
    b = _dot(h, w3_ref[...])
    o_ref[...] = (_silu(a) * b).astype(o_ref.dtype)


def norm_swiglu(x, gain, w1, w3, layer, tm, tn, side=None):
    m, k = x.shape
    n = w1.shape[2]
    nj = n // tn
    in_specs = [pl.BlockSpec((tm, k), lambda i, j: (i, 0)),
                pl.BlockSpec((1, k), lambda i, j: (0, 0)),
                pl.BlockSpec((None, k, tn), lambda i, j: (layer, 0, j)),
                pl.BlockSpec((None, k, tn), lambda i, j: (layer, 0, j))]
    out_specs = [pl.BlockSpec((tm, tn), lambda i, j: (i, j))]
    out_shape = [jax.ShapeDtypeStruct((m, n), BF16)]
    args = [x, gain.reshape(1, k), w1, w3]
    kern = _norm_swiglu_kernel
    if side is not None:
        s_in, s_out, s_shape = _side_cast_specs(side, (m // tm) * nj, lambda i, j: i * nj + j)
        kern = _with_side_cast(kern, len(in_specs), len(out_specs))
        in_specs.append(s_in)
        out_specs.append(s_out)
        out_shape.append(s_shape)
        args.append(side)
    outs = pl.pallas_call(
        kern,
        grid=(m // tm, nj),
        in_specs=in_specs,
        out_specs=out_specs,
        out_shape=out_shape,
        scratch_shapes=[pltpu.VMEM((tm, k), BF16)],
        compiler_params=_params("arbitrary", "arbitrary"),
        name="norm_swiglu",
    )(*args)
    return outs if side is not None else outs[0]


def _matmul_residual_kernel(y_ref, w_ref, x_ref, o_ref):
    o_ref[...] = x_ref[...] + _dot(y_ref[...], w_ref[...])


def matmul_residual(y, w, layer, x, tm, tn):
    m, k = y.shape
    n = w.shape[2]
    return pl.pallas_call(
        _matmul_residual_kernel,
        grid=(m // tm, n // tn),
        in_specs=[pl.BlockSpec((tm, k), lambda i, j: (i, 0)),
                  pl.BlockSpec((None, k, tn), lambda i, j: (layer, 0, j)),
                  pl.BlockSpec((tm, tn), lambda i, j: (i, j))],
        out_specs=pl.BlockSpec((tm, tn), lambda i, j: (i, j)),
        out_shape=jax.ShapeDtypeStruct((m, n), F32),
        compiler_params=_params("parallel", "parallel"),
        name="matmul_residual",
    )(y, w, x)


def _matmul2_residual_kernel(ya_ref, yb_ref, wa_ref, wb_ref, x_ref, o_ref):
    o_ref[...] = x_ref[...] + _dot(ya_ref[...], wa_ref[...]) + _dot(yb_ref[...], wb_ref[...])


def matmul2_residual(ya, yb, w, layer, x, tm, tn, side):
    m, ka = ya.shape
    kb = yb.shape[1]
    n = w.shape[2]
    nj = n // tn
    assert ka % kb == 0
    in_specs = [pl.BlockSpec((tm, ka), lambda i, j: (i, 0)),
                pl.BlockSpec((tm, kb), lambda i, j: (i, 0)),
                pl.BlockSpec((None, ka, tn), lambda i, j: (layer, 0, j)),
                pl.BlockSpec((None, kb, tn), lambda i, j: (layer, ka // kb, j)),
                pl.BlockSpec((tm, tn), lambda i, j: (i, j))]
    s_in, s_out, s_shape = _side_cast_specs(side, (m // tm) * nj, lambda i, j: i * nj + j)
    return pl.pallas_call(
        _with_side_cast(_matmul2_residual_kernel, len(in_specs), 1),
        grid=(m // tm, nj),
        in_specs=in_specs + [s_in],
        out_specs=[pl.BlockSpec((tm, tn), lambda i, j: (i, j)), s_out],
        out_shape=[jax.ShapeDtypeStruct((m, n), F32), s_shape],
        compiler_params=_params("arbitrary", "arbitrary"),
        name="matmul2_residual",
    )(ya, yb, w, w, x, side)


def _retention_kernel(lg_ref, q_ref, k_ref, v_ref, g_ref, cos_ref, sin_ref, o_ref, state_ref,
                      *, chunk, n_chunks):
    @pl.when(pl.program_id(1) == 0)
    def _():
        state_ref[...] = jnp.zeros_like(state_ref)

    c = chunk
    half = RET_QK_DIM // 2
    lg = lg_ref[0]
    lg_l = lg[:, :LANES]
    ii = lax.broadcasted_iota(jnp.int32, (c, c), 0)
    jj = lax.broadcasted_iota(jnp.int32, (c, c), 1)
    rel = (ii - jj).astype(F32)
    inner = jnp.where(rel >= 0, jnp.exp(lg[:, :c] * jnp.maximum(rel, 0.0)), 0.0)
    idx = lax.broadcasted_iota(jnp.int32, (c, LANES), 0).astype(F32)
    q_dec = jnp.exp(lg_l * (idx + 1.0))
    k_dec = jnp.exp(lg_l * (c - 1.0 - idx))
    c_dec = jnp.exp(lg * float(c))
    k_scale = RET_QK_DIM ** -0.5

    def rot(t, cos, sin):
        t1, t2 = t[:, :half], t[:, half:]
        return t1 * cos - t2 * sin, t1 * sin + t2 * cos

    rows = [pl.ds(ci * c, c) for ci in range(n_chunks)]
    qb, kb, qd, kd = [], [], [], []
    for r in rows:
        cos = cos_ref[r, :]
        sin = sin_ref[r, :]
        q1, q2 = rot(q_ref[r, :].astype(F32), cos, sin)
        k1, k2 = rot(k_ref[r, :].astype(F32) * k_scale, cos, sin)
        qb.append(jnp.concatenate([q1, q2], axis=-1).astype(BF16))
        kb.append(jnp.concatenate([k1, k2], axis=-1).astype(BF16))
        qd.append(jnp.concatenate([q1 * q_dec, q2 * q_dec], axis=-1).astype(BF16))
        kd.append(jnp.concatenate([k1 * k_dec, k2 * k_dec], axis=-1).astype(BF16))
    scores = [(_dot_nt(a, b) * inner).astype(BF16) for a, b in zip(qb, kb)]
    kv = [_dot_tn(a, v_ref[r, :]) for a, r in zip(kd, rows)]
    states = [state_ref[...]]
    for inc in kv:
        states.append(states[-1] * c_dec + inc)
    state_ref[...] = states[-1]
    outs = [_dot(sc, v_ref[r, :]) + _dot(a, st.astype(BF16))
            for sc, r, a, st in zip(scores, rows, qd, states)]
    for o, r in zip(outs, rows):
        gate = _silu(g_ref[r, :].astype(F32))
        o_ref[r, :] = (_rms_rows(o) * gate).astype(o_ref.dtype)


def retention_mixer(proj, cos, sin, rows_per_step):
    s = proj.shape[0]
    t = rows_per_step
    log_gamma = jnp.log(1.0 - 2.0 ** (-5.0 - jnp.arange(RET_HEADS, dtype=F32)))
    lg = jnp.broadcast_to(log_gamma[:, None, None], (RET_HEADS, 1, RET_V_DIM))
    qk_blocks = AR_QK // RET_QK_DIM
    v_off = 2 * AR_QK // RET_V_DIM
    g_off = v_off + AR_V // RET_V_DIM
    kern = functools.partial(_retention_kernel, chunk=RET_CHUNK, n_chunks=t // RET_CHUNK)
    return pl.pallas_call(
        kern,
        grid=(RET_HEADS, s // t),
        in_specs=[pl.BlockSpec((1, 1, RET_V_DIM), lambda h, i: (h, 0, 0)),
                  pl.BlockSpec((t, RET_QK_DIM), lambda h, i: (i, h)),
                  pl.BlockSpec((t, RET_QK_DIM), lambda h, i: (i, qk_blocks + h)),
                  pl.BlockSpec((t, RET_V_DIM), lambda h, i: (i, v_off + h)),
                  pl.BlockSpec((t, RET_V_DIM), lambda h, i: (i, g_off + h)),
                  pl.BlockSpec((t, RET_QK_DIM // 2), lambda h, i: (i, 0)),
                  pl.BlockSpec((t, RET_QK_DIM // 2), lambda h, i: (i, 0))],
        out_specs=pl.BlockSpec((t, RET_V_DIM), lambda h, i: (i, h)),
        out_shape=jax.ShapeDtypeStruct((s, AR_V), BF16),
        scratch_shapes=[pltpu.VMEM((RET_QK_DIM, RET_V_DIM), F32)],
        compiler_params=_params("parallel", "arbitrary"),
        name="retention",
    )(lg, proj, proj, proj, proj, cos, sin)


def _ar_in_proj_kernel(x_ref, g_ref, w_ref, gn_ref, oa_ref, o1_ref, o4_ref, o16_ref,
                       h_ref, buf, nb, nb4, *, n_a):
    j = pl.program_id(1)
    tm, tn = buf.shape
    dd = DIL_DIM

    @pl.when(j == 0)
    def _():
        h_ref[...] = (_rms_rows(x_ref[...]) * g_ref[...]).astype(BF16)

    @pl.when(j < n_a)
    def _():
        oa_ref[...] = _dot(h_ref[...], w_ref[...]).astype(oa_ref.dtype)

    def head_major(normed):
        buf[...] = _dot(h_ref[...], w_ref[...])
        for hd in range(tn // dd):
            slot = hd % 2
            x = buf[:, hd * dd:(hd + 1) * dd]
            nb[slot] = _rms_rows(x) * gn_ref[...] if normed else x
            o1_ref[hd] = nb[slot].astype(o1_ref.dtype)
            for r4 in range(4):
                stream = nb[slot, pl.ds(r4, tm // 4, stride=4), :]
                nb4[slot, r4] = stream
                o4_ref[hd, :, r4 * dd:(r4 + 1) * dd] = stream.astype(o4_ref.dtype)
            for r in range(16):
                sub = nb4[slot, r % 4, pl.ds(r // 4, tm // 16, stride=4), :]
                o16_ref[hd, :, r * dd:(r + 1) * dd] = sub.astype(o16_ref.dtype)

    @pl.when((j >= n_a) & (j < n_a + 2))
    def _():
        head_major(True)

    @pl.when(j == n_a + 2)
    def _():
        head_major(False)


def ar_in_proj(x, gain, w, q_gain, k_gain, tm, side=None):
    m, k = x.shape
    dd = DIL_DIM
    tn = DIL_W
    n_a = (2 * AR_QK + 2 * AR_V) // tn
    nh = 3 * DIL_HEADS
    gains = jnp.stack([q_gain * (dd ** -0.5), k_gain, jnp.ones_like(k_gain)]).reshape(3, 1, dd)

    def grp(j):
        return jnp.clip(j - n_a, 0, 2)

    nj = n_a + 3
    in_specs = [pl.BlockSpec((tm, k), lambda i, j: (i, 0)),
                pl.BlockSpec((1, k), lambda i, j: (0, 0)),
                pl.BlockSpec((None, k, tn), lambda i, j: (0, 0, j)),
                pl.BlockSpec((None, 1, dd), lambda i, j: (grp(j), 0, 0))]
    out_specs = ([pl.BlockSpec((tm, tn), lambda i, j: (i, jnp.minimum(j, n_a - 1)))]
                 + [pl.BlockSpec((DIL_HEADS, tm // d, d * dd), lambda i, j: (grp(j), i, 0)) for d in (1, 4, 16)])
    out_shape = ([jax.ShapeDtypeStruct((m, n_a * tn), BF16)]
                 + [jax.ShapeDtypeStruct((nh, m // d, d * dd), BF16) for d in (1, 4, 16)])
    args = [x, gain.reshape(1, k), w, gains]
    kern = functools.partial(_ar_in_proj_kernel, n_a=n_a)
    if side is not None:
        s_in, s_out, s_shape = _side_cast_specs(side, (m // tm) * nj, lambda i, j: i * nj + j)
        kern = _with_side_cast(kern, len(in_specs), len(out_specs))
        in_specs.append(s_in)
        out_specs.append(s_out)
        out_shape.append(s_shape)
        args.append(side)
    return pl.pallas_call(
        kern,
        grid=(m // tm, nj),
        in_specs=in_specs,
        out_specs=out_specs,
        out_shape=out_shape,
        scratch_shapes=[pltpu.VMEM((tm, k), BF16),
                        pltpu.VMEM((tm, tn), F32),
                        pltpu.VMEM((2, tm, dd), F32),
                        pltpu.VMEM((2, 4, tm // 4, dd), F32)],
        compiler_params=_params("arbitrary", "arbitrary"),
        name="ar_in_proj",
    )(*args)


def _dilated_kernel(bias_ref,
                    q1, k1, k1p, v1, v1p,
                    q4, k4, k4p, v4, v4p,
                    q16, k16, k16p, v16, v16p,
                    o_ref, acc_ref, m_ref, l_ref, *, tile, group):
    n = DIL_STEPS
    d_dim = DIL_DIM
    first = pl.program_id(1) == 0

    def run_group(bias_idx, blocks, mode):
        def ld(p):
            return p[0][p[1]]

        bias = bias_ref[bias_idx]
        in_prev = lax.broadcasted_iota(jnp.int32, (n, 2 * n), 1) < n
        keys = [jnp.concatenate([ld(b[1]), ld(b[2])], axis=0) for b in blocks]
        vals = [jnp.concatenate([ld(b[3]), ld(b[4])], axis=0) for b in blocks]
        sc = [_dot_nt(ld(b[0]), kk) + bias for b, kk in zip(blocks, keys)]
        sc = [jnp.where(first & in_prev, NEG, x) if b[5] else x for x, b in zip(sc, blocks)]
        m = [jnp.max(x, axis=-1, keepdims=True) for x in sc]
        pr = [jnp.exp(x - mm) for x, mm in zip(sc, m)]
        l = [jnp.sum(x, axis=-1, keepdims=True) for x in pr]
        o = [_dot(x.astype(BF16), vv) for x, vv in zip(pr, vals)]
        for b, mm, ll, oo in zip(blocks, m, l, o):
            rows = b[6]
            if mode == "init":
                m_ref[rows, :] = jnp.broadcast_to(mm, (n, d_dim))
                l_ref[rows, :] = jnp.broadcast_to(ll, (n, d_dim))
                acc_ref[rows, :] = oo
                continue
            m_old = m_ref[rows, :]
            m_new = jnp.maximum(m_old, mm)
            a_old = jnp.exp(m_old - m_new)
            a_cur = jnp.exp(mm - m_new)
            l_new = l_ref[rows, :] * a_old + ll * a_cur
            acc_new = acc_ref[rows, :] * a_old + oo * a_cur
            if mode == "merge":
                m_ref[rows, :] = m_new
                l_ref[rows, :] = l_new
                acc_ref[rows, :] = acc_new
            else:
                o_ref[rows, :] = (acc_new / l_new).astype(o_ref.dtype)

    every = slice(None)

    blocks = []
    for r in range(16):
        ls = slice(r * d_dim, (r + 1) * d_dim)
        blocks.append(((q16, (every, ls)), (k16p, (every, ls)), (k16, (every, ls)),
                       (v16p, (every, ls)), (v16, (every, ls)), True, pl.ds(r, n, stride=16)))
    for g0 in range(0, len(blocks), group):
        run_group(2, blocks[g0:g0 + group], "init")

    blocks = []
    for r in range(4):
        ls = slice(r * d_dim, (r + 1) * d_dim)
        for b in range(tile // (4 * n)):
            cur = pl.ds(b * n, n)
            prev = pl.ds((b - 1) * n, n)
            kp, vp = (((k4p, (every, ls)), (v4p, (every, ls))) if b == 0
                      else ((k4, (prev, ls)), (v4, (prev, ls))))
            blocks.append(((q4, (cur, ls)), kp, (k4, (cur, ls)), vp, (v4, (cur, ls)), b == 0,
                           pl.ds(b * 4 * n + r, n, stride=4)))
    for g0 in range(0, len(blocks), group):
        run_group(1, blocks[g0:g0 + group], "merge")

    blocks = []
    for b in range(tile // n):
        cur = pl.ds(b * n, n)
        prev = pl.ds((b - 1) * n, n)
        kp, vp = (((k1p, (every, every)), (v1p, (every, every))) if b == 0
                  else ((k1, (prev, every)), (v1, (prev, every))))
        blocks.append(((q1, (cur, every)), kp, (k1, (cur, every)), vp, (v1, (cur, every)), b == 0, cur))
    for g0 in range(0, len(blocks), group):
        run_group(0, blocks[g0:g0 + group], "final")


def _t5_bucket(dist):
    exact = REL_BUCKETS // 2
    large = exact + (jnp.log(jnp.maximum(dist, exact).astype(F32) / exact)
                     / math.log(REL_MAX_DIST / exact) * (REL_BUCKETS - exact)).astype(jnp.int32)
    large = jnp.minimum(large, REL_BUCKETS - 1)
    return jnp.where(dist < exact, dist, large)


def _dilated_bias(rel_bias):
    n = DIL_STEPS
    steps = jnp.arange(n + 1)
    vec = jnp.stack([rel_bias.astype(F32)[_t5_bucket(steps * dil)] for (_, dil) in DIL_PATTERNS])
    vec = vec.transpose(0, 2, 1)
    width = 3 * n
    p = jnp.full(vec.shape[:2] + (width,), NEG, F32)
    p = p.at[:, :, n - 1:2 * n].set(vec[:, :, ::-1])
    tiled = jnp.broadcast_to(p[:, :, None, :], vec.shape[:2] + (n, width))
    skew = tiled.reshape(vec.shape[:2] + (n * width,))[:, :, :n * (width - 1)]
    skew = skew.reshape(vec.shape[:2] + (n, width - 1))
    return skew[:, :, :, n - 1:3 * n - 1]


def dilated_attention(qkv1, qkv4, qkv16, rel_bias, tile, group):
    _, s, dd = qkv1.shape
    n = DIL_STEPS
    h_n = DIL_HEADS
    bias = _dilated_bias(rel_bias)
    args, specs = [bias], [pl.BlockSpec((len(DIL_PATTERNS), None, n, 2 * n), lambda h, i: (0, h, 0, 0))]
    for dil, view in ((1, qkv1), (4, qkv4), (16, qkv16)):
        rows = tile // dil
        per = rows // n

        def cur_spec(off, rows=rows, dil=dil):
            return pl.BlockSpec((None, rows, dil * dd), lambda h, i: (off + h, i, 0))

        def prev_spec(off, per=per, dil=dil):
            return pl.BlockSpec((None, n, dil * dd), lambda h, i: (off + h, jnp.maximum(i * per - 1, 0), 0))

        args += [view, view, view, view, view]
        specs += [cur_spec(0), cur_spec(h_n), prev_spec(h_n), cur_spec(2 * h_n), prev_spec(2 * h_n)]
    return pl.pallas_call(
        functools.partial(_dilated_kernel, tile=tile, group=group),
        grid=(h_n, s // tile),
        in_specs=specs,
        out_specs=pl.BlockSpec((tile, dd), lambda h, i: (i, h)),
        out_shape=jax.ShapeDtypeStruct((s, h_n * dd), BF16),
        scratch_shapes=[pltpu.VMEM((tile, dd), F32)] * 3,
        compiler_params=_params("parallel", "arbitrary"),
        name="dilated_attention",
    )(*args)


def _gdn_in_proj_kernel(x_ref, g_ref, w_ref, ws_ref, cw_ref, oc_ref, oz_ref, os_ref,
                        h_ref, buf, tails, *, n_norm, n_conv, q_tiles):
    i = pl.program_id(0)
    j = pl.program_id(1)
    tm = buf.shape[0] - 8
    tn = buf.shape[1]
    halo = 8
    sub = 128
    dd = GDN_DIM

    @pl.when((i == 0) & (j == 0))
    def _():
        tails[...] = jnp.zeros_like(tails)

    @pl.when(j == 0)
    def _():
        h = (_rms_rows(x_ref[...]) * g_ref[...]).astype(BF16)
        h_ref[...] = h
        os_ref[...] = _dot(h, ws_ref[...])

    def conv_tile(l2norm):
        buf[halo:halo + tm, :] = _dot(h_ref[...], w_ref[...])
        buf[0:halo, :] = tails[j]
        tails[j] = buf[tm:tm + halo, :]
        scale = jnp.where(j < q_tiles, dd ** -0.5, 1.0)
        for r0 in range(0, tm, sub):
            y = buf[halo + r0:halo + r0 + sub, :] * cw_ref[GDN_CONV - 1:GDN_CONV, :]
            for t in range(GDN_CONV - 1):
                off = halo + r0 - (GDN_CONV - 1) + t
                y = y + buf[off:off + sub, :] * cw_ref[t:t + 1, :]
            y = _silu(y)
            if l2norm:
                for h0 in range(0, tn, dd):
                    yh = y[:, h0:h0 + dd]
                    yh = yh * (lax.rsqrt(jnp.sum(yh * yh, axis=-1, keepdims=True) + EPS) * scale)
                    oc_ref[r0:r0 + sub, h0:h0 + dd] = yh.astype(oc_ref.dtype)
            else:
                oc_ref[r0:r0 + sub, :] = y.astype(oc_ref.dtype)

    @pl.when(j < n_norm)
    def _():
        conv_tile(True)

    @pl.when((j >= n_norm) & (j < n_conv))
    def _():
        conv_tile(False)

    @pl.when(j >= n_conv)
    def _():
        oz_ref[...] = _dot(h_ref[...], w_ref[...]).astype(oz_ref.dtype)


def gdn_in_proj(x, gain, w, w_side, conv_w, tm, tn):
    m, k = x.shape
    ns = w_side.shape[1]
    n_conv = GDN_QKV // tn
    n_norm = 2 * GDN_QK_W // tn
    n_tiles = (GDN_QKV + GDN_V_W) // tn
    kern = functools.partial(_gdn_in_proj_kernel, n_norm=n_norm, n_conv=n_conv, q_tiles=GDN_QK_W // tn)
    return pl.pallas_call(
        kern,
        grid=(m // tm, n_tiles),
        in_specs=[pl.BlockSpec((tm, k), lambda i, j: (i, 0)),
                  pl.BlockSpec((1, k), lambda i, j: (0, 0)),
                  pl.BlockSpec((None, k, tn), lambda i, j: (0, 0, j)),
                  pl.BlockSpec((k, ns), lambda i, j: (0, 0)),
                  pl.BlockSpec((GDN_CONV, tn), lambda i, j: (0, jnp.minimum(j, n_conv - 1)))],
        out_specs=[pl.BlockSpec((tm, tn), lambda i, j: (i, jnp.minimum(j, n_conv - 1))),
                   pl.BlockSpec((tm, tn), lambda i, j: (i, jnp.maximum(j - n_conv, 0))),
                   pl.BlockSpec((tm, ns), lambda i, j: (i, 0))],
        out_shape=[jax.ShapeDtypeStruct((m, GDN_QKV), BF16),
                   jax.ShapeDtypeStruct((m, GDN_V_W), BF16),
                   jax.ShapeDtypeStruct((m, ns), F32)],
        scratch_shapes=[pltpu.VMEM((tm, k), BF16),
                        pltpu.VMEM((8 + tm, tn), F32),
                        pltpu.VMEM((n_conv, 8, tn), F32)],
        compiler_params=_params("arbitrary", "arbitrary"),
        name="gdn_in_proj",
    )(x, gain.reshape(1, k), w, w_side, conv_w)


def _gdn_gates_kernel(ab_ref, alog_ref, dt_ref, beta_ref, gcum_ref, gct_ref, *, chunk):
    t = ab_ref.shape[0]
    hv = GDN_V_HEADS
    ab = ab_ref[...]
    beta_ref[...] = jax.nn.sigmoid(ab[:, :hv])
    z = ab[:, hv:] + dt_ref[...]
    softplus = jnp.maximum(z, 0.0) + jnp.log1p(jnp.exp(-jnp.abs(z)))
    g = -jnp.exp(alog_ref[...]) * softplus
    ii = lax.broadcasted_iota(jnp.int32, (chunk, chunk), 0)
    jj = lax.broadcasted_iota(jnp.int32, (chunk, chunk), 1)
    tri = (ii >= jj).astype(F32)
    eh = (lax.broadcasted_iota(jnp.int32, (hv, hv), 0) == lax.broadcasted_iota(jnp.int32, (hv, hv), 1)).astype(F32)
    gcs = [_dot_f32(tri, g[r0:r0 + chunk]) for r0 in range(0, t, chunk)]
    gts = [_dot_nt_f32(eh, gc) for gc in gcs]
    for n, (gc, gt) in enumerate(zip(gcs, gts)):
        gcum_ref[n * chunk:(n + 1) * chunk, :] = gc
        gct_ref[:, n * chunk:(n + 1) * chunk] = gt


def gdn_gates(ab, a_log, dt_bias, tm):
    s = ab.shape[0]
    hv = GDN_V_HEADS
    return pl.pallas_call(
        functools.partial(_gdn_gates_kernel, chunk=GDN_CHUNK),
        grid=(s // tm,),
        in_specs=[pl.BlockSpec((tm, 2 * hv), lambda i: (i, 0)),
                  pl.BlockSpec((1, hv), lambda i: (0, 0)),
                  pl.BlockSpec((1, hv), lambda i: (0, 0))],
        out_specs=[pl.BlockSpec((tm, hv), lambda i: (i, 0)),
                   pl.BlockSpec((tm, hv), lambda i: (i, 0)),
                   pl.BlockSpec((hv, tm), lambda i: (0, i))],
        out_shape=[jax.ShapeDtypeStruct((s, hv), F32), jax.ShapeDtypeStruct((s, hv), F32),
                   jax.ShapeDtypeStruct((hv, s), F32)],
        compiler_params=_params("parallel"),
        name="gdn_gates",
    )(ab, a_log.reshape(1, hv), dt_bias.reshape(1, hv))


def _unit_lower_inverses(a_list):
    c = a_list[0].shape[0]
    ii = lax.broadcasted_iota(jnp.int32, (c, c), 0)
    jj = lax.broadcasted_iota(jnp.int32, (c, c), 1)
    eye = jnp.where(ii == jj, 1.0, 0.0)
    p = [-a for a in a_list]
    t = [eye + x for x in p]
    pb = [x.astype(BF16) for x in p]
    p = [_dot(x, x) for x in pb]
    span = 4
    while span < c:
        pb = [x.astype(BF16) for x in p]
        st = [_dot(jnp.concatenate([x, y.astype(BF16)], axis=0), x) for x, y in zip(pb, t)]
        p = [x[:c] for x in st]
        t = [y + x[c:] for x, y in zip(st, t)]
        span *= 2
    pb = [x.astype(BF16) for x in p]
    return [y + _dot(y.astype(BF16), x) for x, y in zip(pb, t)]


def _gdn_kernel(q_ref, k_ref, v_ref, z_ref, beta_ref, gcum_ref, gct_ref, ng_ref, o_ref, state_ref,
                *, tb, chunk, kheads):
    grp = pl.program_id(0)
    c = chunk
    dd = GDN_DIM
    nh = 2 * kheads
    nc = tb // c

    @pl.when(pl.program_id(1) == 0)
    def _():
        state_ref[...] = jnp.zeros_like(state_ref)

    lane_h = lax.broadcasted_iota(jnp.int32, (1, GDN_V_HEADS), 1)
    ii = lax.broadcasted_iota(jnp.int32, (c, c), 0)
    jj = lax.broadcasted_iota(jnp.int32, (c, c), 1)
    causal = ii >= jj
    strict = ii > jj
    ng = ng_ref[...]

    def head_cols(h):
        return slice(h * dd, (h + 1) * dd)

    for ci in range(nc):
        r = slice(ci * c, (ci + 1) * c)
        beta_t = beta_ref[r, :]
        gcum_t = gcum_ref[r, :]
        beta_col, gc_col, gc_row = [], [], []
        for h in range(nh):
            hv = grp * nh + h
            sel = lane_h == hv
            beta_col.append(jnp.sum(jnp.where(sel, beta_t, 0.0), axis=-1, keepdims=True))
            gc_col.append(jnp.sum(jnp.where(sel, gcum_t, 0.0), axis=-1, keepdims=True))
            gc_row.append(gct_ref[pl.ds(hv, 1), :][:, r])
        eg_col = [jnp.exp(x) for x in gc_col]

        k16 = [k_ref[r, head_cols(kh)] for kh in range(kheads)]
        q16 = [q_ref[r, head_cols(kh)] for kh in range(kheads)]
        kf = [x.astype(F32) for x in k16]
        qkk = [_dot_nt(jnp.concatenate([q, k], axis=0), k) for q, k in zip(q16, k16)]

        decay, a_list = [], []
        for h in range(nh):
            dlog = gc_col[h] - gc_row[h]
            dec = jnp.where(causal, jnp.exp(jnp.where(causal, dlog, 0.0)), 0.0)
            decay.append(dec)
            a_list.append(jnp.where(strict, qkk[h // 2][c:] * beta_col[h] * dec, 0.0))
        t_list = _unit_lower_inverses(a_list)

        uw = []
        for h in range(nh):
            rhs = jnp.concatenate([v_ref[r, head_cols(h)].astype(F32) * beta_col[h],
                                   kf[h // 2] * (beta_col[h] * eg_col[h])], axis=1)
            uw.append(_dot(t_list[h].astype(BF16), rhs.astype(BF16)))

        states = [state_ref[h] for h in range(nh)]
        s16 = [s.astype(BF16) for s in states]
        wq = []
        for h in range(nh):
            lhs = jnp.concatenate([uw[h][:, dd:], q16[h // 2].astype(F32) * eg_col[h]], axis=0)
            wq.append(_dot(lhs.astype(BF16), s16[h]))
        vn16 = [(uw[h][:, :dd] - wq[h][:c]).astype(BF16) for h in range(nh)]
        outs = []
        for h in range(nh):
            intra = jnp.where(causal, qkk[h // 2][:c] * decay[h], 0.0)
            outs.append(wq[h][c:] + _dot(intra.astype(BF16), vn16[h]))
        for h in range(nh):
            g_last = gc_col[h][c - 1:c, :]
            kd = kf[h // 2] * jnp.exp(g_last - gc_col[h])
            state_ref[h] = states[h] * jnp.exp(g_last) + _dot_tn(kd.astype(BF16), vn16[h])
        for h in range(nh):
            zg = _silu(z_ref[r, head_cols(h)].astype(F32))
            o_ref[r, head_cols(h)] = (_rms_rows(outs[h]) * ng * zg).astype(o_ref.dtype)


def gdn_mixer(qkv, z, beta, gcum, gct, norm_gain, tb, kheads):
    s = qkv.shape[0]
    dd = GDN_DIM
    wk = kheads * dd
    wv = 2 * wk
    k0 = GDN_QK_W // wk
    v0 = 2 * GDN_QK_W // wv
    return pl.pallas_call(
        functools.partial(_gdn_kernel, tb=tb, chunk=GDN_CHUNK, kheads=kheads),
        grid=(GDN_K_HEADS // kheads, s // tb),
        in_specs=[pl.BlockSpec((tb, wk), lambda g, i: (i, g)),
                  pl.BlockSpec((tb, wk), lambda g, i: (i, k0 + g)),
                  pl.BlockSpec((tb, wv), lambda g, i: (i, v0 + g)),
                  pl.BlockSpec((tb, wv), lambda g, i: (i, g)),
                  pl.BlockSpec((tb, GDN_V_HEADS), lambda g, i: (i, 0)),
                  pl.BlockSpec((tb, GDN_V_HEADS), lambda g, i: (i, 0)),
                  pl.BlockSpec((GDN_V_HEADS, tb), lambda g, i: (0, i)),
                  pl.BlockSpec((1, dd), lambda g, i: (0, 0))],
        out_specs=pl.BlockSpec((tb, wv), lambda g, i: (i, g)),
        out_shape=jax.ShapeDtypeStruct((s, GDN_V_W), BF16),
        scratch_shapes=[pltpu.VMEM((2 * kheads, dd, dd), F32)],
        compiler_params=_params("parallel", "arbitrary"),
        name="gated_deltanet",
    )(qkv, qkv, qkv, z, beta, gcum, gct, norm_gain.reshape(1, dd))


def _xa_kv_kernel(mem_ref, mg_ref, w_ref, kg_ref, k_ref, v_ref):
    mem_n = (_rms_rows(mem_ref[...]) * mg_ref[...]).astype(BF16)
    kv = _dot(mem_n, w_ref[...])
    for h in range(XA_HEADS):
        k = kv[:, h * XA_DIM:(h + 1) * XA_DIM]
        k_ref[h] = (_rms_rows(k) * kg_ref[...]).astype(k_ref.dtype)
        v_ref[h] = kv[:, (XA_HEADS + h) * XA_DIM:(XA_HEADS + h + 1) * XA_DIM].astype(v_ref.dtype)


def xa_keys_values(mem, mem_gain, w_kv, layer, k_gain):
    m, d = mem.shape
    shape = jax.ShapeDtypeStruct((XA_HEADS, m, XA_DIM), BF16)
    nkv = w_kv.shape[2]
    return pl.pallas_call(
        _xa_kv_kernel,
        grid=(1,),
        in_specs=[pl.BlockSpec((m, d), lambda i: (0, 0)),
                  pl.BlockSpec((1, d), lambda i: (0, 0)),
                  pl.BlockSpec((None, d, nkv), lambda i: (layer, 0, 0)),
                  pl.BlockSpec((1, XA_DIM), lambda i: (0, 0))],
        out_specs=[pl.BlockSpec((XA_HEADS, m, XA_DIM), lambda i: (0, 0, 0))] * 2,
        out_shape=[shape, shape],
        compiler_params=_params("arbitrary"),
        name="xa_keys_values",
    )(mem, mem_gain.reshape(1, d), w_kv, k_gain.reshape(1, XA_DIM))


def _xattn_kernel(x_ref, g_ref, wq_ref, qg_ref, k_ref, v_ref, wo_ref, o_ref):
    x = x_ref[...]
    h = (_rms_rows(x) * g_ref[...]).astype(BF16)
    q = _dot(h, wq_ref[...])
    qg = qg_ref[...] * (XA_DIM ** -0.5)
    heads = range(XA_HEADS)
    qh = [(_rms_rows(q[:, hd * XA_DIM:(hd + 1) * XA_DIM]) * qg).astype(BF16) for hd in heads]
    sc = [_dot_nt(qh[hd], k_ref[hd]) for hd in heads]
    pr = [jnp.exp(s - jnp.max(s, axis=-1, keepdims=True)) for s in sc]
    pr = [p / jnp.sum(p, axis=-1, keepdims=True) for p in pr]
    outs = [_dot(pr[hd].astype(BF16), v_ref[hd]).astype(BF16) for hd in heads]
    o = jnp.concatenate(outs, axis=-1)
    o_ref[...] = x + _dot(o, wo_ref[...])


def cross_attention(x, gain, w_q, q_gain, k, v, w_o, layer, tm):
    s, d = x.shape
    hw = XA_HEADS * XA_DIM
    m = k.shape[1]
    return pl.pallas_call(
        _xattn_kernel,
        grid=(s // tm,),
        in_specs=[pl.BlockSpec((tm, d), lambda i: (i, 0)),
                  pl.BlockSpec((1, d), lambda i: (0, 0)),
                  pl.BlockSpec((None, d, hw), lambda i: (layer, 0, 0)),
                  pl.BlockSpec((1, XA_DIM), lambda i: (0, 0)),
                  pl.BlockSpec((XA_HEADS, m, XA_DIM), lambda i: (0, 0, 0)),
                  pl.BlockSpec((XA_HEADS, m, XA_DIM), lambda i: (0, 0, 0)),
                  pl.BlockSpec((None, hw, d), lambda i: (layer, 0, 0))],
        out_specs=pl.BlockSpec((tm, d), lambda i: (i, 0)),
        out_shape=jax.ShapeDtypeStruct((s, d), F32),
        compiler_params=_params("parallel"),
        name="cross_attention",
    )(x, gain.reshape(1, d), w_q, q_gain.reshape(1, XA_DIM), k, v, w_o)


def _pick_tile(total, target):
    t = min(total, target)
    while total % t:
        t //= 2
    return t


def _rotary_tables(s):
    inv = ROPE_BASE ** (-jnp.arange(0, RET_QK_DIM, 2, dtype=F32) / RET_QK_DIM)
    ang = jnp.arange(s).astype(F32)[:, None] * inv[None, :]
    return jnp.cos(ang), jnp.sin(ang)


def kernel(x, mem, norm_mix, norm_xa, norm_ffn, mem_norm, rel_bias, ar_w_in, ar_w_out, dil_q_gain, dil_k_gain, gdn_w_in, gdn_conv, gdn_a_log, gdn_dt_bias, gdn_norm, gdn_w_out, xa_w_q, xa_w_kv, xa_w_o, xa_q_gain, xa_k_gain, ffn_w1, ffn_w3, ffn_w2):
    b, s, d = x.shape
    assert b == 1
    xs = x.reshape(s, d)
    mem2 = mem.reshape(mem.shape[1], d)
    tm = _pick_tile(s, 1024)
    ar_w_in, ar_w_out, gdn_w_out, xa_w_q, xa_w_kv, xa_w_o = (
        w.astype(BF16) for w in (ar_w_in, ar_w_out, gdn_w_out, xa_w_q, xa_w_kv, xa_w_o))
    depth, _, hidden = ffn_w1.shape

    def tail(xs, layer, w1, w3, w2):
        k, v = xa_keys_values(mem2, mem_norm, xa_w_kv, layer, xa_k_gain[layer])
        xs = cross_attention(xs, norm_xa[layer], xa_w_q, xa_q_gain[layer], k, v, xa_w_o, layer,
                             _pick_tile(s, 512))
        if w2.dtype == F32:
            hid, w2 = norm_swiglu(xs, norm_ffn[layer], w1, w3, layer, tm, _pick_tile(hidden, 512), side=w2)
            w2 = w2.reshape(depth, hidden, d)
        else:
            hid = norm_swiglu(xs, norm_ffn[layer], w1, w3, layer, tm, _pick_tile(hidden, 512))
        return matmul_residual(hid, w2, layer, xs, tm, _pick_tile(d, 512)), w2

    proj, qkv1, qkv4, qkv16, w1 = ar_in_proj(xs, norm_mix[0], ar_w_in, dil_q_gain[0], dil_k_gain[0], tm,
                                             side=ffn_w1.reshape(depth * d, hidden))
    w1 = w1.reshape(depth, d, hidden)
    cos, sin = _rotary_tables(s)
    ya = retention_mixer(proj, cos, sin, _pick_tile(s, 1024))
    yb = dilated_attention(qkv1, qkv4, qkv16, rel_bias, _pick_tile(s, 2048), 8)
    xs, w3 = matmul2_residual(ya, yb, ar_w_out, 0, xs, tm, _pick_tile(d, 1024),
                              side=ffn_w3.reshape(depth * d, hidden))
    w3 = w3.reshape(depth, d, hidden)
    xs, w2 = tail(xs, 0, w1, w3, ffn_w2.reshape(depth * hidden, d))

    n_main = GDN_QKV + GDN_V_W
    w_gate = gdn_w_in[0, :, n_main:].astype(BF16)
    qkv, z, ab = gdn_in_proj(xs, norm_mix[1], gdn_w_in.astype(BF16), w_gate, gdn_conv[0], tm, 1024)
    beta, gcum, gct = gdn_gates(ab, gdn_a_log[0], gdn_dt_bias[0], _pick_tile(s, 1024))
    og = gdn_mixer(qkv, z, beta, gcum, gct, gdn_norm[0], 4 * GDN_CHUNK, 4)
    xs = matmul_residual(og, gdn_w_out, 0, xs, tm, _pick_tile(d, 512))
    xs, _ = tail(xs, 1, w1, w3, w2)
    return xs.reshape(b, s, d)
```
